```python
import jax, jax.numpy as jnp
from jax import lax
import numpy as np

D_MODEL = 4096
BATCH = 2
SEQ = 4096
DEPTH = 1
DEC_BATCH = 16
DEC_SEQ = 32
PAST_LEN = 1024

CHUNK = 64
N_PREV_CHUNKS = 8
BAND = N_PREV_CHUNKS * CHUNK
ATTN_WIDTH = D_MODEL // 2
LRU_WIDTH = D_MODEL - ATTN_WIDTH
HEAD_DIM = 128
N_HEADS = ATTN_WIDTH // HEAD_DIM
LRU_BLOCKS = 16
LRU_BLOCK = LRU_WIDTH // LRU_BLOCKS
CONV_W = 4
LRU_C = 8.0
REL_CLIP = 128
N_REL = 2 * REL_CLIP + 1
D_FF = -(-8 * D_MODEL // (3 * 256)) * 256
IN_COLS = 3 * ATTN_WIDTH + 2 * LRU_WIDTH
N_MOD = 6
NEG_INF = -1e30

kernel_name = "hymba_style_streaming_attn_rglru_encoder_step"


def rms_norm(x, g, eps=1e-6):
    xf = x.astype(jnp.float32)
    y = xf * lax.rsqrt(jnp.mean(xf * xf, axis=-1, keepdims=True) + eps)
    return (y * g.astype(jnp.float32)).astype(x.dtype)


def band_attention(q, k, v, q_pos, k_pos, k_valid, rel_bias):
    s = jnp.einsum("bqhd,bkhd->bhqk", q, k).astype(jnp.float32) * (HEAD_DIM ** -0.5)
    rel = jnp.clip(q_pos[:, None] - k_pos[None, :], -REL_CLIP, REL_CLIP) + REL_CLIP
    s = s + rel_bias[:, rel].astype(jnp.float32)[None]
    s = jnp.where(k_valid[None, None, None, :], s, NEG_INF)
    p = jax.nn.softmax(s, axis=-1).astype(v.dtype)
    return jnp.einsum("bhqk,bkhd->bqhd", p, v)


def prompt_attention(q, k, v, rel_bias):
    B, S = q.shape[0], q.shape[1]
    n_chunks = S // CHUNK
    span = BAND + CHUNK
    pad = ((0, 0), (BAND, 0), (0, 0), (0, 0))
    kp = jnp.pad(k, pad)
    vp = jnp.pad(v, pad)

    def one_chunk(ci):
        start = ci * CHUNK
        qc = lax.dynamic_slice_in_dim(q, start, CHUNK, axis=1)
        kc = lax.dynamic_slice_in_dim(kp, start, span, axis=1)
        vc = lax.dynamic_slice_in_dim(vp, start, span, axis=1)
        q_pos = start + jnp.arange(CHUNK)
        k_pos = start - BAND + jnp.arange(span)
        return band_attention(qc, kc, vc, q_pos, k_pos, k_pos >= 0, rel_bias)

    out = lax.map(one_chunk, jnp.arange(n_chunks))
    return jnp.moveaxis(out, 0, 1).reshape(q.shape)


def sample_attention(q, k_new, v_new, k_cache, v_cache, rel_bias):
    T = q.shape[1]
    R = k_cache.shape[1]
    k = jnp.concatenate([k_cache.astype(k_new.dtype), k_new], axis=1)
    v = jnp.concatenate([v_cache.astype(v_new.dtype), v_new], axis=1)
    q_pos = PAST_LEN + jnp.arange(T)
    k_pos = PAST_LEN - R + jnp.arange(R + T)
    return band_attention(q, k, v, q_pos, k_pos, k_pos >= 0, rel_bias)


def causal_conv(x, x_prev, w, b):
    T = x.shape[1]
    xc = jnp.concatenate([x_prev.astype(x.dtype), x], axis=1)
    y = b + sum(xc[:, i:i + T] * w[i] for i in range(CONV_W))
    return y, xc[:, -(CONV_W - 1):]


def rg_lru(x, h0, w_a, b_a, w_x, b_x, lam):
    B, T, _ = x.shape
    xb = x.reshape(B, T, LRU_BLOCKS, LRU_BLOCK)
    r = jax.nn.sigmoid(jnp.einsum("btni,nij->btnj", xb, w_a).reshape(B, T, LRU_WIDTH) + b_a)
    i = jax.nn.sigmoid(jnp.einsum("btni,nij->btnj", xb, w_x).reshape(B, T, LRU_WIDTH) + b_x)
    log_a = -LRU_C * r.astype(jnp.float32) * jax.nn.softplus(-lam.astype(jnp.float32))
    a = jnp.exp(log_a)
    b_in = jnp.sqrt(-jnp.expm1(2.0 * log_a)) * (i * x).astype(jnp.float32)
    b_in = b_in.at[:, 0].add(a[:, 0] * h0.astype(jnp.float32))

    def combine(left, right):
        a1, b1 = left
        a2, b2 = right
        return a1 * a2, a2 * b1 + b2

    _, h = lax.associative_scan(combine, (a, b_in), axis=1)
    return h.astype(x.dtype), h[:, -1].astype(x.dtype)


def hybrid_layer(x, c, k_cache, v_cache, conv_prev, h0, lw):
    (norm_mix_g, norm_ffn_g, w_mod, b_mod, w_in, q_norm_g, k_norm_g, rel_bias,
     conv_w, conv_b, w_rg_a, b_rg_a, w_rg_x, b_rg_x, lru_lambda, w_out,
     w_ffn_gate, w_ffn_up, w_ffn_down) = lw
    B, T, _ = x.shape
    mod = jax.nn.silu(c) @ w_mod + b_mod
    sh1, sc1, g1, sh2, sc2, g2 = jnp.split(mod[:, None, :], N_MOD, axis=-1)

    u = rms_norm(x, norm_mix_g) * (1 + sc1) + sh1
    z = u @ w_in
    q, k, v, xr, gr = jnp.split(
        z, [ATTN_WIDTH, 2 * ATTN_WIDTH, 3 * ATTN_WIDTH, 3 * ATTN_WIDTH + LRU_WIDTH], axis=-1)
    q = rms_norm(q.reshape(B, T, N_HEADS, HEAD_DIM), q_norm_g)
    k = rms_norm(k.reshape(B, T, N_HEADS, HEAD_DIM), k_norm_g)
    v = v.reshape(B, T, N_HEADS, HEAD_DIM)

    if k_cache is None:
        attn = prompt_attention(q, k, v, rel_bias)
        keep = min(BAND, T)
        k_state, v_state = k[:, -keep:], v[:, -keep:]
        conv_prev = jnp.zeros((B, CONV_W - 1, LRU_WIDTH), x.dtype)
        h0 = jnp.zeros((B, LRU_WIDTH), x.dtype)
    else:
        attn = sample_attention(q, k, v, k_cache, v_cache, rel_bias)
        k_state, v_state = k, v

    xc, conv_state = causal_conv(xr, conv_prev, conv_w, conv_b)
    hr, h_state = rg_lru(xc, h0, w_rg_a, b_rg_a, w_rg_x, b_rg_x, lru_lambda)
    lru_out = hr * jax.nn.gelu(gr)

    mix = jnp.concatenate([attn.reshape(B, T, ATTN_WIDTH), lru_out], axis=-1) @ w_out
    x = x + g1 * mix

    u2 = rms_norm(x, norm_ffn_g) * (1 + sc2) + sh2
    ff = (jax.nn.silu(u2 @ w_ffn_gate) * (u2 @ w_ffn_up)) @ w_ffn_down
    x = x + g2 * ff
    return x, k_state, v_state, conv_state, h_state


def setup_inputs(seed: int = 0) -> dict:
    key = jax.random.key(seed)
    ks = jax.random.split(key, 32)
    f32 = jnp.float32
    nrm = lambda k, shape, s: (jax.random.normal(k, shape, f32) * s).astype(f32)
    cache_rows = min(BAND, PAST_LEN)
    a0 = jax.random.uniform(ks[20], (DEPTH, LRU_WIDTH), f32, 0.9, 0.999)
    sig = a0 ** (1.0 / LRU_C)
    lam = jnp.log(sig) - jnp.log1p(-sig)
    return {
        "x_prompt": nrm(ks[0], (BATCH, SEQ, D_MODEL), 1.0),
        "x_sample": nrm(ks[1], (DEC_BATCH, DEC_SEQ, D_MODEL), 1.0),
        "cache_k": nrm(ks[2], (DEPTH, DEC_BATCH, cache_rows, N_HEADS, HEAD_DIM), 1.0),
        "cache_v": nrm(ks[3], (DEPTH, DEC_BATCH, cache_rows, N_HEADS, HEAD_DIM), 1.0),
        "state_conv": nrm(ks[4], (DEPTH, DEC_BATCH, CONV_W - 1, LRU_WIDTH), 1.0),
        "state_h": nrm(ks[5], (DEPTH, DEC_BATCH, LRU_WIDTH), 0.5),
        "c_prompt": nrm(ks[6], (BATCH, D_MODEL), 1.0),
        "c_sample": nrm(ks[7], (DEC_BATCH, D_MODEL), 1.0),
        "norm_mix_g": 1.0 + nrm(ks[8], (DEPTH, D_MODEL), 0.02),
        "norm_ffn_g": 1.0 + nrm(ks[9], (DEPTH, D_MODEL), 0.02),
        "w_mod": nrm(ks[10], (DEPTH, D_MODEL, N_MOD * D_MODEL), 0.3 * D_MODEL ** -0.5),
        "b_mod": nrm(ks[11], (DEPTH, N_MOD * D_MODEL), 0.1),
        "w_in": nrm(ks[12], (DEPTH, D_MODEL, IN_COLS), D_MODEL ** -0.5),
        "q_norm_g": 1.0 + nrm(ks[13], (DEPTH, HEAD_DIM), 0.02),
        "k_norm_g": 1.0 + nrm(ks[14], (DEPTH, HEAD_DIM), 0.02),
        "rel_bias": nrm(ks[15], (DEPTH, N_HEADS, N_REL), 0.1),
        "conv_w": nrm(ks[16], (DEPTH, CONV_W, LRU_WIDTH), 0.5),
        "conv_b": nrm(ks[17], (DEPTH, LRU_WIDTH), 0.01),
        "w_rg_a": nrm(ks[18], (DEPTH, LRU_BLOCKS, LRU_BLOCK, LRU_BLOCK), LRU_BLOCK ** -0.5),
        "b_rg_a": nrm(ks[19], (DEPTH, LRU_WIDTH), 0.01),
        "w_rg_x": nrm(ks[21], (DEPTH, LRU_BLOCKS, LRU_BLOCK, LRU_BLOCK), LRU_BLOCK ** -0.5),
        "b_rg_x": nrm(ks[22], (DEPTH, LRU_WIDTH), 0.01),
        "lru_lambda": lam,
        "w_out": nrm(ks[23], (DEPTH, D_MODEL, D_MODEL), D_MODEL ** -0.5),
        "w_ffn_gate": nrm(ks[24], (DEPTH, D_MODEL, D_FF), D_MODEL ** -0.5),
        "w_ffn_up": nrm(ks[25], (DEPTH, D_MODEL, D_FF), D_MODEL ** -0.5),
        "w_ffn_down": nrm(ks[26], (DEPTH, D_FF, D_MODEL), D_FF ** -0.5),
    }


def reference(x_prompt, x_sample, cache_k, cache_v, state_conv, state_h, c_prompt, c_sample,
              norm_mix_g, norm_ffn_g, w_mod, b_mod, w_in, q_norm_g, k_norm_g, rel_bias,
              conv_w, conv_b, w_rg_a, b_rg_a, w_rg_x, b_rg_x, lru_lambda, w_out,
              w_ffn_gate, w_ffn_up, w_ffn_down):
    yp, ys = x_prompt, x_sample
    kp_l, vp_l, cp_l, hp_l = [], [], [], []
    ks_l, vs_l, cs_l, hs_l = [], [], [], []
    for l in range(DEPTH):
        lw = (norm_mix_g[l], norm_ffn_g[l], w_mod[l], b_mod[l], w_in[l], q_norm_g[l], k_norm_g[l],
              rel_bias[l], conv_w[l], conv_b[l], w_rg_a[l], b_rg_a[l], w_rg_x[l], b_rg_x[l],
              lru_lambda[l], w_out[l], w_ffn_gate[l], w_ffn_up[l], w_ffn_down[l])
        yp, kp, vp, cp, hp = hybrid_layer(yp, c_prompt, None, None, None, None, lw)
        ys, k_s, v_s, c_s, h_s = hybrid_layer(ys, c_sample, cache_k[l], cache_v[l],
                                              state_conv[l], state_h[l], lw)
        kp_l.append(kp); vp_l.append(vp); cp_l.append(cp); hp_l.append(hp)
        ks_l.append(k_s); vs_l.append(v_s); cs_l.append(c_s); hs_l.append(h_s)
    return (yp, ys,
            jnp.stack(kp_l), jnp.stack(vp_l), jnp.stack(cp_l), jnp.stack(hp_l),
            jnp.stack(ks_l), jnp.stack(vs_l), jnp.stack(cs_l), jnp.stack(hs_l))
```

```python
import functools
import math

import jax
import jax.numpy as jnp
from jax import lax
from jax.experimental import pallas as pl
from jax.experimental.pallas import tpu as pltpu

F32 = jnp.float32
BF16 = jnp.bfloat16

CHUNK = 64
N_PREV_CHUNKS = 8
HEAD_DIM = 128
LRU_BLOCK = 128
LRU_C = 8.0
NEG_INF = -1e30
EPS = 1e-6

LANES = 128
SUBLANES = 8
MIB = 1024 * 1024


def _params(semantics, vmem_mib):
    return pltpu.CompilerParams(dimension_semantics=semantics, vmem_limit_bytes=vmem_mib * MIB)


def _sigmoid(x):
    return 1.0 / (1.0 + jnp.exp(-x))


def _gelu_tanh(x):
    c = math.sqrt(2.0 / math.pi)
    return x * (0.5 * (1.0 + jnp.tanh(c * (x + 0.044715 * (x * x * x)))))


def _group_scale(y, g):
    rows, n = y.shape
    groups = g.shape[0]
    return (y.reshape(groups, rows // groups, n) * g[:, None, :]).reshape(rows, n)


def _mod_kernel(c_ref, w_ref, b_ref, o_ref, *, n_sample, n_prompt, groups_per_prompt):
    c = c_ref[...]
    s = (c * _sigmoid(c)).astype(BF16)
    m = jnp.dot(s, w_ref[...].astype(BF16), preferred_element_type=F32) + b_ref[...]
    tn = m.shape[1]
    for b in range(n_prompt):
        row = m[n_sample + b:n_sample + b + 1, :]
        o_ref[b * groups_per_prompt:(b + 1) * groups_per_prompt, :] = jnp.broadcast_to(
            row, (groups_per_prompt, tn))
    o_ref[n_prompt * groups_per_prompt:n_prompt * groups_per_prompt + n_sample, :] = m[:n_sample, :]


def _mod_table(c_prompt, c_sample, w_mod, b_mod, groups_per_prompt):
    n_prompt, d = c_prompt.shape
    n_sample = c_sample.shape[0]
    n_out = w_mod.shape[1]
    rows = n_sample + n_prompt
    rows_pad = -(-rows // SUBLANES) * SUBLANES
    c_all = jnp.concatenate(
        [c_sample, c_prompt, jnp.zeros((rows_pad - rows, d), c_prompt.dtype)], axis=0)
    n_groups = n_prompt * groups_per_prompt + n_sample
    tn = 512
    return pl.pallas_call(
        functools.partial(_mod_kernel, n_sample=n_sample, n_prompt=n_prompt,
                          groups_per_prompt=groups_per_prompt),
        out_shape=jax.ShapeDtypeStruct((n_groups, n_out), F32),
        grid=(n_out // tn,),
        in_specs=[
            pl.BlockSpec((rows_pad, d), lambda j: (0, 0)),
            pl.BlockSpec((d, tn), lambda j: (0, j)),
            pl.BlockSpec((1, tn), lambda j: (0, j)),
        ],
        out_specs=pl.BlockSpec((n_groups, tn), lambda j: (0, j)),
        compiler_params=_params(("arbitrary",), 40),
        name="mod_table",
    )(c_all, w_mod, b_mod.reshape(1, n_out))


def _norm_mod_kernel(x_ref, gain_ref, shift_ref, scale_ref, o_ref):
    x = x_ref[...]
    ms = jnp.mean(x * x, axis=-1, keepdims=True)
    y = x * lax.rsqrt(ms + EPS) * gain_ref[...]
    rows, d = y.shape
    groups = scale_ref.shape[0]
    y3 = y.reshape(groups, rows // groups, d)
    u = y3 * (1.0 + scale_ref[...][:, None, :]) + shift_ref[...][:, None, :]
    o_ref[...] = u.reshape(rows, d).astype(BF16)


def _norm_mod(x, gain, table, shift_col, scale_col, group):
    rows, d = x.shape
    tm = 256
    gt = tm // group
    return pl.pallas_call(
        _norm_mod_kernel,
        out_shape=jax.ShapeDtypeStruct((rows, d), BF16),
        grid=(rows // tm,),
        in_specs=[
            pl.BlockSpec((tm, d), lambda i: (i, 0)),
            pl.BlockSpec((1, d), lambda i: (0, 0)),
            pl.BlockSpec((gt, d), lambda i: (i, shift_col)),
            pl.BlockSpec((gt, d), lambda i: (i, scale_col)),
        ],
        out_specs=pl.BlockSpec((tm, d), lambda i: (i, 0)),
        compiler_params=_params(("arbitrary",), 40),
        name="norm_mod",
    )(x, gain.reshape(1, d), table, table)


def _in_proj_kernel(x_ref, w_ref, gain_ref, o_ref, wb_ref, *, n_norm_tiles):
    j = pl.program_id(0)
    i = pl.program_id(1)

    @pl.when(i == 0)
    def _():
        wb_ref[...] = w_ref[...].astype(BF16)

    acc = jnp.dot(x_ref[...], wb_ref[...], preferred_element_type=F32)
    tn = acc.shape[1]

    @pl.when(j < n_norm_tiles)
    def _():
        for g in range(tn // HEAD_DIM):
            sl = slice(g * HEAD_DIM, (g + 1) * HEAD_DIM)
            zg = acc[:, sl]
            ms = jnp.mean(zg * zg, axis=-1, keepdims=True)
            o_ref[:, sl] = zg * lax.rsqrt(ms + EPS) * gain_ref[:, sl]

    @pl.when(j >= n_norm_tiles)
    def _():
        o_ref[...] = acc


def _in_proj(u, w_in, qk_gain, attn_width):
    rows, d = u.shape
    n = w_in.shape[1]
    tm, tn = 512, 512
    n_norm_tiles = 2 * attn_width // tn
    return pl.pallas_call(
        functools.partial(_in_proj_kernel, n_norm_tiles=n_norm_tiles),
        out_shape=jax.ShapeDtypeStruct((rows, n), F32),
        grid=(n // tn, rows // tm),
        in_specs=[
            pl.BlockSpec((tm, d), lambda j, i: (i, 0)),
            pl.BlockSpec((d, tn), lambda j, i: (0, j)),
            pl.BlockSpec((1, tn), lambda j, i: (0, jnp.minimum(j, n_norm_tiles - 1))),
        ],
        out_specs=pl.BlockSpec((tm, tn), lambda j, i: (i, j)),
        scratch_shapes=[pltpu.VMEM((d, tn), BF16)],
        compiler_params=_params(("arbitrary", "arbitrary"), 48),
        name="in_proj",
    )(u, w_in, qk_gain)


def _bias_ramp(rel_bias, top, length):
    n_heads, n_rel = rel_bias.shape
    clip = (n_rel - 1) // 2
    lead = top - clip
    tail = length - lead - n_rel
    assert lead >= 0 and tail >= 0
    rev = rel_bias[:, ::-1]
    ext = jnp.concatenate([
        jnp.broadcast_to(rel_bias[:, n_rel - 1:], (n_heads, lead)),
        rev,
        jnp.broadcast_to(rel_bias[:, :1], (n_heads, tail)),
    ], axis=1)
    return ext.reshape(n_heads, 1, length)


def _toeplitz(base, rows):
    w = base.shape[1]
    return pltpu.roll(jnp.broadcast_to(base, (rows, w)), w - (rows - 1), 1, stride=1, stride_axis=0)


def _attn_prompt_kernel(q_ref, ka_ref, kb_ref, kc_ref, va_ref, vb_ref, vc_ref, ext_ref,
                        o_ref, bias_ref, *, scale):
    i = pl.program_id(2)
    tq = q_ref.shape[0]

    @pl.when(i == 0)
    def _():
        shift = CHUNK.bit_length() - 1
        rowc = jnp.right_shift(lax.broadcasted_iota(jnp.int32, (tq, tq), 0), shift)
        colc = jnp.right_shift(lax.broadcasted_iota(jnp.int32, (tq, tq), 1), shift)
        for m in range(3):
            off = (2 - m) * tq
            t = _toeplitz(ext_ref[0, :, off:off + 2 * tq], tq)[:, :tq]
            if m == 0:
                t = jnp.where(colc <= rowc, t, NEG_INF)
            if m == 2:
                t = jnp.where(rowc <= colc, t, NEG_INF)
            bias_ref[m] = t

    q = q_ref[...].astype(BF16)
    dn = (((1,), (1,)), ((), ()))
    scores = []
    for m, k_ref in ((2, ka_ref), (1, kb_ref), (0, kc_ref)):
        s = lax.dot_general(q, k_ref[...].astype(BF16), dn, preferred_element_type=F32)
        s = s * scale + bias_ref[m]
        if m > 0:
            s = s + jnp.where(i >= m, 0.0, NEG_INF)
        scores.append(s)
    mx = functools.reduce(jnp.maximum, [jnp.max(s, axis=-1, keepdims=True) for s in scores])
    probs = [jnp.exp(s - mx) for s in scores]
    denom = functools.reduce(jnp.add, [jnp.sum(p, axis=-1, keepdims=True) for p in probs])
    out = functools.reduce(jnp.add, [
        jnp.dot(p.astype(BF16), v_ref[...].astype(BF16), preferred_element_type=F32)
        for p, v_ref in zip(probs, (va_ref, vb_ref, vc_ref))])
    o_ref[...] = (out / denom).astype(BF16)


def _attn_prompt(z, rel_bias, n_batch, seq, n_heads):
    tq = 256
    assert tq == (N_PREV_CHUNKS * CHUNK) // 2 and seq % tq == 0
    nq = seq // tq
    clip = (rel_bias.shape[1] - 1) // 2
    ext = _bias_ramp(rel_bias, 3 * tq - 1, 4 * tq)
    assert 3 * tq - 1 >= clip

    def qmap(b, h, i):
        return (b * nq + i, h)

    def kvmap(back, col0):
        return lambda b, h, i: (b * nq + jnp.maximum(i - back, 0), col0 + h)

    blk = pl.BlockSpec((tq, HEAD_DIM), qmap)
    return pl.pallas_call(
        functools.partial(_attn_prompt_kernel, scale=HEAD_DIM ** -0.5),
        out_shape=jax.ShapeDtypeStruct((n_batch * seq, n_heads * HEAD_DIM), BF16),
        grid=(n_batch, n_heads, nq),
        in_specs=[
            blk,
            pl.BlockSpec((tq, HEAD_DIM), kvmap(2, n_heads)),
            pl.BlockSpec((tq, HEAD_DIM), kvmap(1, n_heads)),
            pl.BlockSpec((tq, HEAD_DIM), kvmap(0, n_heads)),
            pl.BlockSpec((tq, HEAD_DIM), kvmap(2, 2 * n_heads)),
            pl.BlockSpec((tq, HEAD_DIM), kvmap(1, 2 * n_heads)),
            pl.BlockSpec((tq, HEAD_DIM), kvmap(0, 2 * n_heads)),
            pl.BlockSpec((1, 1, 4 * tq), lambda b, h, i: (h, 0, 0)),
        ],
        out_specs=blk,
        scratch_shapes=[pltpu.VMEM((3, tq, tq), F32)],
        compiler_params=_params(("arbitrary", "arbitrary", "arbitrary"), 32),
        name="attn_prompt",
    )(z, z, z, z, z, z, z, ext)


def _attn_sample_kernel(q_ref, kn_ref, vn_ref, ck_ref, cv_ref, ext_ref, o_ref, *, scale, n_heads):
    t = q_ref.shape[0]
    r = ck_ref.shape[1]
    dn = (((1,), (1,)), ((), ()))
    for h in range(n_heads):
        sl = slice(h * HEAD_DIM, (h + 1) * HEAD_DIM)
        bias = _toeplitz(ext_ref[h], t)
        q = q_ref[:, sl].astype(BF16)
        s1 = lax.dot_general(q, ck_ref[0, :, sl].astype(BF16), dn, preferred_element_type=F32)
        s1 = s1 * scale + bias[:, :r]
        s2 = lax.dot_general(q, kn_ref[:, sl].astype(BF16), dn, preferred_element_type=F32)
        s2 = s2 * scale + bias[:, r:r + t]
        mx = jnp.maximum(jnp.max(s1, axis=-1, keepdims=True), jnp.max(s2, axis=-1, keepdims=True))
        p1 = jnp.exp(s1 - mx)
        p2 = jnp.exp(s2 - mx)
        denom = jnp.sum(p1, axis=-1, keepdims=True) + jnp.sum(p2, axis=-1, keepdims=True)
        out = (jnp.dot(p1.astype(BF16), cv_ref[0, :, sl].astype(BF16), preferred_element_type=F32)
               + jnp.dot(p2.astype(BF16), vn_ref[:, sl].astype(BF16), preferred_element_type=F32))
        o_ref[:, sl] = (out / denom).astype(BF16)


def _attn_sample(z, cache_k, cache_v, rel_bias, row0, n_batch, t, n_heads):
    r = cache_k.shape[1]
    width = n_heads * HEAD_DIM
    ck = cache_k.reshape(n_batch, r, width)
    cv = cache_v.reshape(n_batch, r, width)
    top = r + t - 1
    clip = (rel_bias.shape[1] - 1) // 2
    ramp = -(-max(r + 2 * t, top + clip + 1) // LANES) * LANES
    ext = _bias_ramp(rel_bias, top, ramp)
    rb0 = row0 // t
    return pl.pallas_call(
        functools.partial(_attn_sample_kernel, scale=HEAD_DIM ** -0.5, n_heads=n_heads),
        out_shape=jax.ShapeDtypeStruct((n_batch * t, width), BF16),
        grid=(n_batch,),
        in_specs=[
            pl.BlockSpec((t, width), lambda b: (rb0 + b, 0)),
            pl.BlockSpec((t, width), lambda b: (rb0 + b, 1)),
            pl.BlockSpec((t, width), lambda b: (rb0 + b, 2)),
            pl.BlockSpec((1, r, width), lambda b: (b, 0, 0)),
            pl.BlockSpec((1, r, width), lambda b: (b, 0, 0)),
            pl.BlockSpec((n_heads, 1, ramp), lambda b: (0, 0, 0)),
        ],
        out_specs=pl.BlockSpec((t, width), lambda b: (b, 0)),
        compiler_params=_params(("arbitrary",), 40),
        name="attn_sample",
    )(z, z, z, ck, cv, ext)


def _lru_tile(x, prev, h_in, gate_in, cw, cb, wa, ba, wx, bx, lam):
    t = x.shape[0]
    p = prev.shape[0]
    k = cw.shape[0]
    xcat = jnp.concatenate([prev, x], axis=0)
    xc = cb
    for i in range(k):
        lo = p - (k - 1) + i
        xc = xc + xcat[lo:lo + t] * cw[i:i + 1]
    xcb = xc.astype(BF16)
    rg = _sigmoid(jnp.dot(xcb, wa.astype(BF16), preferred_element_type=F32) + ba)
    ig = _sigmoid(jnp.dot(xcb, wx.astype(BF16), preferred_element_type=F32) + bx)
    neg = -lam
    softplus = jnp.maximum(neg, 0.0) + jnp.log1p(jnp.exp(-jnp.abs(neg)))
    log_a = (-LRU_C) * rg * softplus
    a = jnp.exp(log_a)
    b = jnp.sqrt(-jnp.tanh(log_a) * (a * a + 1.0)) * (ig * xc)
    row = lax.broadcasted_iota(jnp.int32, a.shape, 0)
    step = 1
    while step < t:
        keep = row >= step
        a_prev = jnp.where(keep, pltpu.roll(a, step, 0), 1.0)
        b_prev = jnp.where(keep, pltpu.roll(b, step, 0), 0.0)
        b = a * b_prev + b
        a = a * a_prev
        step *= 2
    h = a * h_in + b
    return h * _gelu_tanh(gate_in), h


def _lru_prompt_kernel(x_ref, g_ref, cw_ref, cb_ref, wa_ref, ba_ref, wx_ref, bx_ref, lam_ref,
                       o_ref, conv_ref, h_ref, px_ref, ph_ref):
    @pl.when(pl.program_id(2) == 0)
    def _():
        px_ref[...] = jnp.zeros_like(px_ref)
        ph_ref[...] = jnp.zeros_like(ph_ref)

    x = x_ref[...]
    t = x.shape[0]
    out, h = _lru_tile(x, px_ref[...], ph_ref[SUBLANES - 1:SUBLANES, :], g_ref[...],
                       cw_ref[...], cb_ref[...], wa_ref[0], ba_ref[...], wx_ref[0], bx_ref[...],
                       lam_ref[...])
    o_ref[...] = out.astype(BF16)
    px_ref[...] = x[t - SUBLANES:]
    ph_ref[...] = h[t - SUBLANES:]
    conv_ref[0] = x[t - SUBLANES:]
    h_ref[0] = h[t - SUBLANES:]


def _lru_prompt(z, lw, n_batch, seq, col_x, col_g):
    conv_w, conv_b, w_a, b_a, w_x, b_x, lam = lw
    n_blocks = w_a.shape[0]
    width = n_blocks * LRU_BLOCK
    tt = 256
    nt = seq // tt
    cx, cg = col_x // LRU_BLOCK, col_g // LRU_BLOCK
    vec = lambda a: a.reshape(1, width)
    vspec = pl.BlockSpec((1, LRU_BLOCK), lambda b, n, t: (0, n))
    wspec = pl.BlockSpec((1, LRU_BLOCK, LRU_BLOCK), lambda b, n, t: (n, 0, 0))
    sspec = pl.BlockSpec((1, SUBLANES, LRU_BLOCK), lambda b, n, t: (b, 0, n))
    return pl.pallas_call(
        _lru_prompt_kernel,
        out_shape=(
            jax.ShapeDtypeStruct((n_batch * seq, width), BF16),
            jax.ShapeDtypeStruct((n_batch, SUBLANES, width), F32),
            jax.ShapeDtypeStruct((n_batch, SUBLANES, width), F32),
        ),
        grid=(n_batch, n_blocks, nt),
        in_specs=[
            pl.BlockSpec((tt, LRU_BLOCK), lambda b, n, t: (b * nt + t, cx + n)),
            pl.BlockSpec((tt, LRU_BLOCK), lambda b, n, t: (b * nt + t, cg + n)),
            pl.BlockSpec((conv_w.shape[0], LRU_BLOCK), lambda b, n, t: (0, n)),
            vspec, wspec, vspec, wspec, vspec, vspec,
        ],
        out_specs=(
            pl.BlockSpec((tt, LRU_BLOCK), lambda b, n, t: (b * nt + t, n)),
            sspec, sspec,
        ),
        scratch_shapes=[pltpu.VMEM((SUBLANES, LRU_BLOCK), F32), pltpu.VMEM((SUBLANES, LRU_BLOCK), F32)],
        compiler_params=_params(("arbitrary", "arbitrary", "arbitrary"), 32),
        name="lru_prompt",
    )(z, z, conv_w, vec(conv_b), w_a, vec(b_a), w_x, vec(b_x), vec(lam))


def _lru_sample_kernel(x_ref, g_ref, prev_ref, h0_ref, cw_ref, cb_ref, wa_ref, ba_ref, wx_ref,
                       bx_ref, lam_ref, o_ref, conv_ref, h_ref, *, n_blocks):
    t = x_ref.shape[0]
    for n in range(n_blocks):
        sl = slice(n * LRU_BLOCK, (n + 1) * LRU_BLOCK)
        x = x_ref[:, sl]
        out, h = _lru_tile(x, prev_ref[0, :, sl], h0_ref[0, :, sl], g_ref[:, sl],
                           cw_ref[:, sl], cb_ref[:, sl], wa_ref[n], ba_ref[:, sl], wx_ref[n],
                           bx_ref[:, sl], lam_ref[:, sl])
        o_ref[:, sl] = out.astype(BF16)
        conv_ref[0, :, sl] = x[t - SUBLANES:]
        h_ref[0, :, sl] = h[t - SUBLANES:]


def _lru_sample(z, state_conv, state_h, lw, row0, n_batch, t, col_x, col_g):
    conv_w, conv_b, w_a, b_a, w_x, b_x, lam = lw
    n_blocks = w_a.shape[0]
    width = n_blocks * LRU_BLOCK
    k = conv_w.shape[0]
    assert t >= SUBLANES and k - 1 <= SUBLANES
    prev = jnp.pad(state_conv, ((0, 0), (SUBLANES - (k - 1), 0), (0, 0)))
    h0 = state_h.reshape(n_batch, 1, width)
    rb0 = row0 // t
    cx, cg = col_x // width, col_g // width
    vec = lambda a: a.reshape(1, width)
    vspec = pl.BlockSpec((1, width), lambda b: (0, 0))
    wspec = pl.BlockSpec((n_blocks, LRU_BLOCK, LRU_BLOCK), lambda b: (0, 0, 0))
    sspec = pl.BlockSpec((1, SUBLANES, width), lambda b: (b, 0, 0))
    return pl.pallas_call(
        functools.partial(_lru_sample_kernel, n_blocks=n_blocks),
        out_shape=(
            jax.ShapeDtypeStruct((n_batch * t, width), BF16),
            jax.ShapeDtypeStruct((n_batch, SUBLANES, width), F32),
            jax.ShapeDtypeStruct((n_batch, SUBLANES, width), F32),
        ),
        grid=(n_batch,),
        in_specs=[
            pl.BlockSpec((t, width), lambda b: (rb0 + b, cx)),
            pl.BlockSpec((t, width), lambda b: (rb0 + b, cg)),
            sspec,
            pl.BlockSpec((1, 1, width), lambda b: (b, 0, 0)),
            pl.BlockSpec((k, width), lambda b: (0, 0)),
            vspec, wspec, vspec, wspec, vspec, vspec,
        ],
        out_specs=(pl.BlockSpec((t, width), lambda b: (b, 0)), sspec, sspec),
        compiler_params=_params(("arbitrary",), 32),
        name="lru_sample",
    )(z, z, prev, h0, conv_w, vec(conv_b), w_a, vec(b_a), w_x, vec(b_x), vec(lam))


def _out_proj_kernel(m_ref, w_ref, x_ref, g_ref, o_ref, wb_ref):
    @pl.when(pl.program_id(1) == 0)
    def _():
        wb_ref[...] = w_ref[...].astype(BF16)

    acc = jnp.dot(m_ref[...], wb_ref[...], preferred_element_type=F32)
    o_ref[...] = x_ref[...] + _group_scale(acc, g_ref[...])


def _out_proj(mix, w_out, x, table, gate_col, group):
    rows, d = mix.shape
    n = w_out.shape[1]
    tm, tn = 512, 512
    gcol0 = gate_col * (n // tn)
    return pl.pallas_call(
        _out_proj_kernel,
        out_shape=jax.ShapeDtypeStruct((rows, n), F32),
        grid=(n // tn, rows // tm),
        in_specs=[
            pl.BlockSpec((tm, d), lambda j, i: (i, 0)),
            pl.BlockSpec((d, tn), lambda j, i: (0, j)),
            pl.BlockSpec((tm, tn), lambda j, i: (i, j)),
            pl.BlockSpec((tm // group, tn), lambda j, i: (i, gcol0 + j)),
        ],
        out_specs=pl.BlockSpec((tm, tn), lambda j, i: (i, j)),
        scratch_shapes=[pltpu.VMEM((d, tn), BF16)],
        compiler_params=_params(("arbitrary", "arbitrary"), 48),
        name="out_proj",
    )(mix, w_out, x, table)


def _ffn_up_kernel(x_ref, wg_ref, wu_ref, o_ref, wgb_ref, wub_ref, *, n_real):
    j = pl.program_id(0)

    @pl.when(pl.program_id(1) == 0)
    def _():
        wgb_ref[...] = wg_ref[...].astype(BF16)
        wub_ref[...] = wu_ref[...].astype(BF16)

    @pl.when(j < n_real)
    def _():
        x = x_ref[...]
        g = jnp.dot(x, wgb_ref[...], preferred_element_type=F32)
        u = jnp.dot(x, wub_ref[...], preferred_element_type=F32)
        o_ref[...] = (g * _sigmoid(g) * u).astype(BF16)

    @pl.when(j >= n_real)
    def _():
        o_ref[...] = jnp.zeros_like(o_ref)


def _ffn_up(u, w_gate, w_up, ff_pad):
    rows, d = u.shape
    ff = w_gate.shape[1]
    tm, tn = 512, 256
    n_real = ff // tn
    assert ff % tn == 0 and ff_pad % tn == 0
    wspec = pl.BlockSpec((d, tn), lambda j, i: (0, jnp.minimum(j, n_real - 1)))
    return pl.pallas_call(
        functools.partial(_ffn_up_kernel, n_real=n_real),
        out_shape=jax.ShapeDtypeStruct((rows, ff_pad), BF16),
        grid=(ff_pad // tn, rows // tm),
        in_specs=[pl.BlockSpec((tm, d), lambda j, i: (i, 0)), wspec, wspec],
        out_specs=pl.BlockSpec((tm, tn), lambda j, i: (i, j)),
        scratch_shapes=[pltpu.VMEM((d, tn), BF16), pltpu.VMEM((d, tn), BF16)],
        compiler_params=_params(("arbitrary", "arbitrary"), 48),
        name="ffn_up",
    )(u, w_gate, w_up)


def _ffn_down_kernel(h_ref, w_ref, x_ref, g_ref, o_ref, *, k_total):
    k = pl.program_id(2)
    tk = w_ref.shape[0]
    row = lax.broadcasted_iota(jnp.int32, w_ref.shape, 0)
    w = jnp.where(row < k_total - k * tk, w_ref[...], 0.0).astype(BF16)

    @pl.when(k == 0)
    def _():
        o_ref[...] = jnp.zeros_like(o_ref)

    tm = o_ref.shape[0]
    rows = min(tm, 512)
    for r0 in range(0, tm, rows):
        o_ref[r0:r0 + rows, :] += jnp.dot(h_ref[r0:r0 + rows, :], w, preferred_element_type=F32)

    @pl.when(k == pl.num_programs(2) - 1)
    def _():
        o_ref[...] = x_ref[...] + _group_scale(o_ref[...], g_ref[...])


def _ffn_down(h, w_down, x1, table, gate_col, group, row0, rows, tm):
    ff_pad = h.shape[1]
    ff, n = w_down.shape
    tn, tk = 1024, 1024
    assert row0 % tm == 0 and rows % tm == 0 and ff_pad % tk == 0
    rb0 = row0 // tm
    gcol0 = gate_col * (n // tn)
    return pl.pallas_call(
        functools.partial(_ffn_down_kernel, k_total=ff),
        out_shape=jax.ShapeDtypeStruct((rows, n), F32),
        grid=(rows // tm, n // tn, ff_pad // tk),
        in_specs=[
            pl.BlockSpec((tm, tk), lambda i, j, k: (rb0 + i, k)),
            pl.BlockSpec((tk, tn), lambda i, j, k: (k, j)),
            pl.BlockSpec((tm, tn), lambda i, j, k: (rb0 + i, j), pipeline_mode=pl.Buffered(1)),
            pl.BlockSpec((tm // group, tn), lambda i, j, k: (rb0 + i, gcol0 + j)),
        ],
        out_specs=pl.BlockSpec((tm, tn), lambda i, j, k: (i, j)),
        compiler_params=_params(("arbitrary", "arbitrary", "arbitrary"), 52),
        name="ffn_down",
    )(h, w_down, x1, table)


def _layer(x, c_prompt, c_sample, cache_k, cache_v, state_conv, state_h, lw, dims):
    (norm_mix_g, norm_ffn_g, w_mod, b_mod, w_in, q_norm_g, k_norm_g, rel_bias,
     conv_w, conv_b, w_rg_a, b_rg_a, w_rg_x, b_rg_x, lru_lambda, w_out,
     w_ffn_gate, w_ffn_up, w_ffn_down) = lw
    n_prompt, seq, n_sample, t_s = dims
    d = x.shape[1]
    group = t_s
    n_heads = rel_bias.shape[0]
    attn_w = n_heads * HEAD_DIM
    lru_w = w_rg_a.shape[0] * LRU_BLOCK
    rows_p = n_prompt * seq
    rows_s = n_sample * t_s

    table = _mod_table(c_prompt, c_sample, w_mod, b_mod, seq // group)
    u = _norm_mod(x, norm_mix_g, table, 0, 1, group)
    qk_gain = jnp.concatenate([jnp.tile(q_norm_g, n_heads), jnp.tile(k_norm_g, n_heads)]).reshape(1, -1)
    z = _in_proj(u, w_in, qk_gain, attn_w)

    attn_p = _attn_prompt(z, rel_bias, n_prompt, seq, n_heads)
    attn_s = _attn_sample(z, cache_k, cache_v, rel_bias, rows_p, n_sample, t_s, n_heads)
    lru_params = (conv_w, conv_b, w_rg_a, b_rg_a, w_rg_x, b_rg_x, lru_lambda)
    col_x, col_g = 3 * attn_w, 3 * attn_w + lru_w
    lru_p, conv_p, h_p = _lru_prompt(z, lru_params, n_prompt, seq, col_x, col_g)
    lru_s, conv_s, h_s = _lru_sample(z, state_conv, state_h, lru_params, rows_p, n_sample, t_s,
                                     col_x, col_g)
    mix = jnp.concatenate([jnp.concatenate([attn_p, lru_p], axis=1),
                           jnp.concatenate([attn_s, lru_s], axis=1)], axis=0)

    x1 = _out_proj(mix, w_out, x, table, 2, group)
    u2 = _norm_mod(x1, norm_ffn_g, table, 3, 4, group)
    ff = w_ffn_gate.shape[1]
    ff_pad = -(-ff // 1024) * 1024
    hidden = _ffn_up(u2, w_ffn_gate, w_ffn_up, ff_pad)
    y_p = _ffn_down(hidden, w_ffn_down, x1, table, 5, group, 0, rows_p, 2048)
    y_s = _ffn_down(hidden, w_ffn_down, x1, table, 5, group, rows_p, rows_s, rows_s)

    keep = min(N_PREV_CHUNKS * CHUNK, seq)
    kv_p = z[:rows_p, attn_w:3 * attn_w].reshape(n_prompt, seq, 2, n_heads, HEAD_DIM)[:, seq - keep:]
    kv_s = z[rows_p:, attn_w:3 * attn_w].reshape(n_sample, t_s, 2, n_heads, HEAD_DIM)
    k_conv = conv_w.shape[0] - 1
    state = dict(
        k_p=kv_p[:, :, 0], v_p=kv_p[:, :, 1], conv_p=conv_p[:, SUBLANES - k_conv:], h_p=h_p[:, SUBLANES - 1],
        k_s=kv_s[:, :, 0], v_s=kv_s[:, :, 1], conv_s=conv_s[:, SUBLANES - k_conv:], h_s=h_s[:, SUBLANES - 1])
    return y_p, y_s, state


def kernel(x_prompt, x_sample, cache_k, cache_v, state_conv, state_h, c_prompt, c_sample, norm_mix_g, norm_ffn_g, w_mod, b_mod, w_in, q_norm_g, k_norm_g, rel_bias, conv_w, conv_b, w_rg_a, b_rg_a, w_rg_x, b_rg_x, lru_lambda, w_out, w_ffn_gate, w_ffn_up, w_ffn_down):
    n_prompt, seq, d = x_prompt.shape
    n_sample, t_s, _ = x_sample.shape
    depth = w_in.shape[0]
    dims = (n_prompt, seq, n_sample, t_s)
    yp = x_prompt.reshape(n_prompt * seq, d)
    ys = x_sample.reshape(n_sample * t_s, d)
    states = []
    for l in range(depth):
        lw = (norm_mix_g[l], norm_ffn_g[l], w_mod[l], b_mod[l], w_in[l], q_norm_g[l], k_norm_g[l],
              rel_bias[l], conv_w[l], conv_b[l], w_rg_a[l], b_rg_a[l], w_rg_x[l], b_rg_x[l],
              lru_lambda[l], w_out[l], w_ffn_gate[l], w_ffn_up[l], w_ffn_down[l])
        x = jnp.concatenate([yp, ys], axis=0)
        yp, ys, st = _layer(x, c_prompt, c_sample, cache_k[l], cache_v[l], state_conv[l],
                            state_h[l], lw, dims)
        states.append(st)
    stack = lambda name: jnp.stack([s[name] for s in states])
    return (yp.reshape(n_prompt, seq, d), ys.reshape(n_sample, t_s, d),
            stack("k_p"), stack("v_p"), stack("conv_p"), stack("h_p"),
            stack("k_s"), stack("v_s"), stack("conv_s"), stack("h_s"))
```

```python
import functools
import math

import jax
import jax.numpy as jnp
from jax import lax
from jax.experimental import pallas as pl
from jax.experimental.pallas import tpu as pltpu

F32 = jnp.float32
BF16 = jnp.bfloat16

CHUNK = 64
N_PREV_CHUNKS = 8
HEAD_DIM = 128
LRU_BLOCK = 128
LRU_C = 8.0
NEG_INF = -1e30
EPS = 1e-6
LOG2E = math.log2(math.e)

LANES = 128
SUBLANES = 8
MIB = 1024 * 1024


def _params(semantics, vmem_mib):
    return pltpu.CompilerParams(dimension_semantics=semantics, vmem_limit_bytes=vmem_mib * MIB)


def _sigmoid(x):
    return 0.5 * jnp.tanh(0.5 * x) + 0.5


def _gelu_tanh(x):
    c = math.sqrt(2.0 / math.pi)
    return x * (0.5 * (1.0 + jnp.tanh(c * (x + 0.044715 * (x * x * x)))))


def _group_scale(y, g):
    rows, n = y.shape
    groups = g.shape[0]
    return (y.reshape(groups, rows // groups, n) * g[:, None, :]).reshape(rows, n)


def _mod_kernel(c_ref, w_ref, b_ref, o_ref, *, n_sample, n_prompt, groups_per_prompt):
    c = c_ref[...]
    s = (c * _sigmoid(c)).astype(BF16)
    m = jnp.dot(s, w_ref[...].astype(BF16), preferred_element_type=F32) + b_ref[...]
    tn = m.shape[1]
    for b in range(n_prompt):
        row = m[n_sample + b:n_sample + b + 1, :]
        o_ref[b * groups_per_prompt:(b + 1) * groups_per_prompt, :] = jnp.broadcast_to(
            row, (groups_per_prompt, tn))
    o_ref[n_prompt * groups_per_prompt:n_prompt * groups_per_prompt + n_sample, :] = m[:n_sample, :]


def _mod_table(c_prompt, c_sample, w_mod, b_mod, groups_per_prompt):
    n_prompt, d = c_prompt.shape
    n_sample = c_sample.shape[0]
    n_out = w_mod.shape[1]
    rows = n_sample + n_prompt
    rows_pad = -(-rows // SUBLANES) * SUBLANES
    c_all = jnp.concatenate(
        [c_sample, c_prompt, jnp.zeros((rows_pad - rows, d), c_prompt.dtype)], axis=0)
    n_groups = n_prompt * groups_per_prompt + n_sample
    tn = 512
    return pl.pallas_call(
        functools.partial(_mod_kernel, n_sample=n_sample, n_prompt=n_prompt,
                          groups_per_prompt=groups_per_prompt),
        out_shape=jax.ShapeDtypeStruct((n_groups, n_out), F32),
        grid=(n_out // tn,),
        in_specs=[
            pl.BlockSpec((rows_pad, d), lambda j: (0, 0)),
            pl.BlockSpec((d, tn), lambda j: (0, j)),
            pl.BlockSpec((1, tn), lambda j: (0, j)),
        ],
        out_specs=pl.BlockSpec((n_groups, tn), lambda j: (0, j)),
        compiler_params=_params(("arbitrary",), 40),
        name="mod_table",
    )(c_all, w_mod, b_mod.reshape(1, n_out))


def _norm_mod_kernel(x_ref, gain_ref, shift_ref, scale_ref, o_ref):
    x = x_ref[...]
    ms = jnp.mean(x * x, axis=-1, keepdims=True)
    y = x * lax.rsqrt(ms + EPS) * gain_ref[...]
    rows, d = y.shape
    groups = scale_ref.shape[0]
    y3 = y.reshape(groups, rows // groups, d)
    u = y3 * (1.0 + scale_ref[...][:, None, :]) + shift_ref[...][:, None, :]
    o_ref[...] = u.reshape(rows, d).astype(BF16)


def _norm_mod(x, gain, table, shift_col, scale_col, group):
    rows, d = x.shape
    tm = 256
    gt = tm // group
    return pl.pallas_call(
        _norm_mod_kernel,
        out_shape=jax.ShapeDtypeStruct((rows, d), BF16),
        grid=(rows // tm,),
        in_specs=[
            pl.BlockSpec((tm, d), lambda i: (i, 0)),
            pl.BlockSpec((1, d), lambda i: (0, 0)),
            pl.BlockSpec((gt, d), lambda i: (i, shift_col)),
            pl.BlockSpec((gt, d), lambda i: (i, scale_col)),
        ],
        out_specs=pl.BlockSpec((tm, d), lambda i: (i, 0)),
        compiler_params=_params(("arbitrary",), 40),
        name="norm_mod",
    )(x, gain.reshape(1, d), table, table)


def _in_proj_kernel(x_ref, w_ref, gain_ref, o_ref, wb_ref, *, n_norm_tiles):
    j = pl.program_id(0)
    i = pl.program_id(1)

    @pl.when(i == 0)
    def _():
        wb_ref[...] = w_ref[...].astype(BF16)

    acc = jnp.dot(x_ref[...], wb_ref[...], preferred_element_type=F32)
    tn = acc.shape[1]

    @pl.when(j < n_norm_tiles)
    def _():
        for g in range(tn // HEAD_DIM):
            sl = slice(g * HEAD_DIM, (g + 1) * HEAD_DIM)
            zg = acc[:, sl]
            ms = jnp.mean(zg * zg, axis=-1, keepdims=True)
            o_ref[:, sl] = zg * lax.rsqrt(ms + EPS) * gain_ref[:, sl]

    @pl.when(j >= n_norm_tiles)
    def _():
        o_ref[...] = acc


def _in_proj(u, w_in, qk_gain, attn_width):
    rows, d = u.shape
    n = w_in.shape[1]
    tm, tn = 512, 512
    n_norm_tiles = 2 * attn_width // tn
    return pl.pallas_call(
        functools.partial(_in_proj_kernel, n_norm_tiles=n_norm_tiles),
        out_shape=jax.ShapeDtypeStruct((rows, n), F32),
        grid=(n // tn, rows // tm),
        in_specs=[
            pl.BlockSpec((tm, d), lambda j, i: (i, 0)),
            pl.BlockSpec((d, tn), lambda j, i: (0, j)),
            pl.BlockSpec((1, tn), lambda j, i: (0, jnp.minimum(j, n_norm_tiles - 1))),
        ],
        out_specs=pl.BlockSpec((tm, tn), lambda j, i: (i, j)),
        scratch_shapes=[pltpu.VMEM((d, tn), BF16)],
        compiler_params=_params(("arbitrary", "arbitrary"), 48),
        name="in_proj",
    )(u, w_in, qk_gain)


def _bias_ramp(rel_bias, top, length):
    n_heads, n_rel = rel_bias.shape
    clip = (n_rel - 1) // 2
    lead = top - clip
    tail = length - lead - n_rel
    assert lead >= 0 and tail >= 0
    rev = rel_bias[:, ::-1]
    ext = jnp.concatenate([
        jnp.broadcast_to(rel_bias[:, n_rel - 1:], (n_heads, lead)),
        rev,
        jnp.broadcast_to(rel_bias[:, :1], (n_heads, tail)),
    ], axis=1)
    return ext.reshape(n_heads, 1, length)


def _lane_reduce(tiles, combine, reduce):
    parts = [t[:, c:c + LANES] for t in tiles for c in range(0, t.shape[1], LANES)]
    return reduce(functools.reduce(combine, parts), axis=-1, keepdims=True)


def _toeplitz(base, rows):
    w = base.shape[1]
    return pltpu.roll(jnp.broadcast_to(base, (rows, w)), w - (rows - 1), 1, stride=1, stride_axis=0)


def _attn_prompt_kernel(q_ref, ka_ref, kb_ref, kc_ref, va_ref, vb_ref, vc_ref, ext_ref,
                        o_ref, bias_ref, *, scale):
    i = pl.program_id(2)
    tq = q_ref.shape[0]
    heads = q_ref.shape[1] // HEAD_DIM

    @pl.when(i == 0)
    def _():
        shift = CHUNK.bit_length() - 1
        rowc = jnp.right_shift(lax.broadcasted_iota(jnp.int32, (tq, tq), 0), shift)
        colc = jnp.right_shift(lax.broadcasted_iota(jnp.int32, (tq, tq), 1), shift)
        for h in range(heads):
            for m in range(3):
                off = (2 - m) * tq
                t = _toeplitz(ext_ref[h, :, off:off + 2 * tq], tq)[:, :tq]
                if m == 0:
                    t = jnp.where(colc <= rowc, t, NEG_INF)
                if m == 2:
                    t = jnp.where(rowc <= colc, t, NEG_INF)
                bias_ref[h, m] = t * LOG2E

    dn = (((1,), (1,)), ((), ()))
    for h in range(heads):
        sl = slice(h * HEAD_DIM, (h + 1) * HEAD_DIM)
        q = q_ref[:, sl].astype(BF16)
        scores = []
        for m, k_ref in ((2, ka_ref), (1, kb_ref), (0, kc_ref)):
            s = lax.dot_general(q, k_ref[:, sl].astype(BF16), dn, preferred_element_type=F32)
            s = s * (scale * LOG2E) + bias_ref[h, m]
            if m > 0:
                s = s + jnp.where(i >= m, 0.0, NEG_INF)
            scores.append(s)
        mx = _lane_reduce(scores, jnp.maximum, jnp.max)
        probs = [jnp.exp2(s - mx) for s in scores]
        denom = _lane_reduce(probs, jnp.add, jnp.sum)
        out = functools.reduce(jnp.add, [
            jnp.dot(p.astype(BF16), v_ref[:, sl].astype(BF16), preferred_element_type=F32)
            for p, v_ref in zip(probs, (va_ref, vb_ref, vc_ref))])
        o_ref[:, sl] = (out / denom).astype(BF16)


def _attn_prompt(z, rel_bias, n_batch, seq, n_heads):
    tq = 256
    hb = 4
    assert tq == (N_PREV_CHUNKS * CHUNK) // 2 and seq % tq == 0 and n_heads % hb == 0
    nq = seq // tq
    ng = n_heads // hb
    clip = (rel_bias.shape[1] - 1) // 2
    ext = _bias_ramp(rel_bias, 3 * tq - 1, 4 * tq)
    assert 3 * tq - 1 >= clip

    def qmap(b, g, i):
        return (b * nq + i, g)

    def kvmap(back, col0):
        return lambda b, g, i: (b * nq + jnp.maximum(i - back, 0), col0 + g)

    blk = pl.BlockSpec((tq, hb * HEAD_DIM), qmap)
    kvspec = lambda back, col0: pl.BlockSpec((tq, hb * HEAD_DIM), kvmap(back, col0))
    return pl.pallas_call(
        functools.partial(_attn_prompt_kernel, scale=HEAD_DIM ** -0.5),
        out_shape=jax.ShapeDtypeStruct((n_batch * seq, n_heads * HEAD_DIM), BF16),
        grid=(n_batch, ng, nq),
        in_specs=[
            blk,
            kvspec(2, ng), kvspec(1, ng), kvspec(0, ng),
            kvspec(2, 2 * ng), kvspec(1, 2 * ng), kvspec(0, 2 * ng),
            pl.BlockSpec((hb, 1, 4 * tq), lambda b, g, i: (g, 0, 0)),
        ],
        out_specs=blk,
        scratch_shapes=[pltpu.VMEM((hb, 3, tq, tq), F32)],
        compiler_params=_params(("arbitrary", "arbitrary", "arbitrary"), 32),
        name="attn_prompt",
    )(z, z, z, z, z, z, z, ext)


def _attn_sample_kernel(q_ref, kn_ref, vn_ref, ck_ref, cv_ref, ext_ref, o_ref, *, scale, n_heads):
    t = q_ref.shape[0]
    r = ck_ref.shape[1] // n_heads
    dn = (((1,), (1,)), ((), ()))
    for h in range(n_heads):
        sl = slice(h * HEAD_DIM, (h + 1) * HEAD_DIM)
        rows_h = pl.ds(h, r, stride=n_heads)
        bias = _toeplitz(ext_ref[h], t)
        q = q_ref[:, sl].astype(BF16)
        s1 = lax.dot_general(q, ck_ref[0, rows_h, :].astype(BF16), dn, preferred_element_type=F32)
        s1 = s1 * scale + bias[:, :r]
        s2 = lax.dot_general(q, kn_ref[:, sl].astype(BF16), dn, preferred_element_type=F32)
        s2 = s2 * scale + bias[:, r:r + t]
        mx = jnp.maximum(jnp.max(s1, axis=-1, keepdims=True), jnp.max(s2, axis=-1, keepdims=True))
        p1 = jnp.exp(s1 - mx)
        p2 = jnp.exp(s2 - mx)
        denom = jnp.sum(p1, axis=-1, keepdims=True) + jnp.sum(p2, axis=-1, keepdims=True)
        out = (jnp.dot(p1.astype(BF16), cv_ref[0, rows_h, :].astype(BF16), preferred_element_type=F32)
               + jnp.dot(p2.astype(BF16), vn_ref[:, sl].astype(BF16), preferred_element_type=F32))
        o_ref[:, sl] = (out / denom).astype(BF16)


def _attn_sample(z, cache_k, cache_v, rel_bias, row0, n_batch, t, n_heads):
    r = cache_k.shape[1]
    width = n_heads * HEAD_DIM
    ck = cache_k.reshape(n_batch, r * n_heads, HEAD_DIM)
    cv = cache_v.reshape(n_batch, r * n_heads, HEAD_DIM)
    top = r + t - 1
    clip = (rel_bias.shape[1] - 1) // 2
    ramp = -(-max(r + 2 * t, top + clip + 1) // LANES) * LANES
    ext = _bias_ramp(rel_bias, top, ramp)
    rb0 = row0 // t
    return pl.pallas_call(
        functools.partial(_attn_sample_kernel, scale=HEAD_DIM ** -0.5, n_heads=n_heads),
        out_shape=jax.ShapeDtypeStruct((n_batch * t, width), BF16),
        grid=(n_batch,),
        in_specs=[
            pl.BlockSpec((t, width), lambda b: (rb0 + b, 0)),
            pl.BlockSpec((t, width), lambda b: (rb0 + b, 1)),
            pl.BlockSpec((t, width), lambda b: (rb0 + b, 2)),
            pl.BlockSpec((1, r * n_heads, HEAD_DIM), lambda b: (b, 0, 0)),
            pl.BlockSpec((1, r * n_heads, HEAD_DIM), lambda b: (b, 0, 0)),
            pl.BlockSpec((n_heads, 1, ramp), lambda b: (0, 0, 0)),
        ],
        out_specs=pl.BlockSpec((t, width), lambda b: (b, 0)),
        compiler_params=_params(("arbitrary",), 40),
        name="attn_sample",
    )(z, z, z, ck, cv, ext)


def _lru_tile(x, prev, h_in, gate_in, cw, cb, wa, ba, wx, bx, lam):
    t = x.shape[0]
    p = prev.shape[0]
    k = cw.shape[0]
    xcat = jnp.concatenate([prev, x], axis=0)
    xc = cb
    for i in range(k):
        lo = p - (k - 1) + i
        xc = xc + xcat[lo:lo + t] * cw[i:i + 1]
    xcb = xc.astype(BF16)
    rg = _sigmoid(jnp.dot(xcb, wa.astype(BF16), preferred_element_type=F32) + ba)
    ig = _sigmoid(jnp.dot(xcb, wx.astype(BF16), preferred_element_type=F32) + bx)
    neg = -lam
    softplus = jnp.maximum(neg, 0.0) + jnp.log1p(jnp.exp(-jnp.abs(neg)))
    log_a = (-LRU_C) * rg * softplus
    a = jnp.exp(log_a)
    var = -jnp.tanh(log_a) * (a * a + 1.0)
    std = jnp.where(var == 0.0, 0.0, var * lax.rsqrt(var))
    b = std * (ig * xc)
    h = _linear_scan(a, b, h_in)
    return h * _gelu_tanh(gate_in), h


def _doubling_scan(a, b, axis):
    n = a.shape[axis]
    idx = lax.broadcasted_iota(jnp.int32, a.shape, axis)
    step = 1
    while step < n:
        keep = idx >= step
        a_prev = jnp.where(keep, pltpu.roll(a, step, axis), 1.0)
        b_prev = jnp.where(keep, pltpu.roll(b, step, axis), 0.0)
        b = a * b_prev + b
        a = a * a_prev
        step *= 2
    return a, b


def _linear_scan(a, b, h_in):
    t, c = a.shape
    groups = t // SUBLANES
    a3, b3 = _doubling_scan(a.reshape(groups, SUBLANES, c), b.reshape(groups, SUBLANES, c), 1)
    ae = jnp.broadcast_to(a3[:, SUBLANES - 1:, :], a3.shape).reshape(t, c)
    be = jnp.broadcast_to(b3[:, SUBLANES - 1:, :], b3.shape).reshape(t, c)
    step = SUBLANES
    while step < t:
        be = jnp.concatenate([be[:step], ae[step:] * be[:-step] + be[step:]], axis=0)
        ae = jnp.concatenate([ae[:step], ae[step:] * ae[:-step]], axis=0)
        step *= 2
    h_end = ae * h_in + be
    h_start = jnp.concatenate([jnp.broadcast_to(h_in, (SUBLANES, c)), h_end[:t - SUBLANES]], axis=0)
    return a3.reshape(t, c) * h_start + b3.reshape(t, c)


def _lru_prompt_kernel(x_ref, g_ref, cw_ref, cb_ref, wa_ref, ba_ref, wx_ref, bx_ref, lam_ref,
                       o_ref, conv_ref, h_ref, px_ref, ph_ref):
    @pl.when(pl.program_id(2) == 0)
    def _():
        px_ref[...] = jnp.zeros_like(px_ref)
        ph_ref[...] = jnp.zeros_like(ph_ref)

    t = x_ref.shape[0]
    for n in range(x_ref.shape[1] // LRU_BLOCK):
        sl = slice(n * LRU_BLOCK, (n + 1) * LRU_BLOCK)
        x = x_ref[:, sl]
        out, h = _lru_tile(x, px_ref[:, sl], ph_ref[SUBLANES - 1:SUBLANES, sl], g_ref[:, sl],
                           cw_ref[:, sl], cb_ref[:, sl], wa_ref[n], ba_ref[:, sl], wx_ref[n],
                           bx_ref[:, sl], lam_ref[:, sl])
        o_ref[:, sl] = out.astype(BF16)
        px_ref[:, sl] = x[t - SUBLANES:]
        ph_ref[:, sl] = h[t - SUBLANES:]
        conv_ref[0, :, sl] = x[t - SUBLANES:]
        h_ref[0, :, sl] = h[t - SUBLANES:]


def _lru_prompt(z, lw, n_batch, seq, col_x, col_g):
    conv_w, conv_b, w_a, b_a, w_x, b_x, lam = lw
    n_blocks = w_a.shape[0]
    width = n_blocks * LRU_BLOCK
    tt = 256
    nb = 4
    lanes = nb * LRU_BLOCK
    assert seq % tt == 0 and n_blocks % nb == 0 and col_x % lanes == 0 and col_g % lanes == 0
    nt = seq // tt
    cx, cg = col_x // lanes, col_g // lanes
    vec = lambda a: a.reshape(1, width)
    vspec = pl.BlockSpec((1, lanes), lambda b, n, t: (0, n))
    wspec = pl.BlockSpec((nb, LRU_BLOCK, LRU_BLOCK), lambda b, n, t: (n, 0, 0))
    sspec = pl.BlockSpec((1, SUBLANES, lanes), lambda b, n, t: (b, 0, n))
    return pl.pallas_call(
        _lru_prompt_kernel,
        out_shape=(
            jax.ShapeDtypeStruct((n_batch * seq, width), BF16),
            jax.ShapeDtypeStruct((n_batch, SUBLANES, width), F32),
            jax.ShapeDtypeStruct((n_batch, SUBLANES, width), F32),
        ),
        grid=(n_batch, n_blocks // nb, nt),
        in_specs=[
            pl.BlockSpec((tt, lanes), lambda b, n, t: (b * nt + t, cx + n)),
            pl.BlockSpec((tt, lanes), lambda b, n, t: (b * nt + t, cg + n)),
            pl.BlockSpec((conv_w.shape[0], lanes), lambda b, n, t: (0, n)),
            vspec, wspec, vspec, wspec, vspec, vspec,
        ],
        out_specs=(
            pl.BlockSpec((tt, lanes), lambda b, n, t: (b * nt + t, n)),
            sspec, sspec,
        ),
        scratch_shapes=[pltpu.VMEM((SUBLANES, lanes), F32), pltpu.VMEM((SUBLANES, lanes), F32)],
        compiler_params=_params(("arbitrary", "arbitrary", "arbitrary"), 32),
        name="lru_prompt",
    )(z, z, conv_w, vec(conv_b), w_a, vec(b_a), w_x, vec(b_x), vec(lam))


def _lru_sample_kernel(x_ref, g_ref, prev_ref, h0_ref, cw_ref, cb_ref, wa_ref, ba_ref, wx_ref,
                       bx_ref, lam_ref, o_ref, conv_ref, h_ref, *, n_blocks):
    t = x_ref.shape[0]
    for n in range(n_blocks):
        sl = slice(n * LRU_BLOCK, (n + 1) * LRU_BLOCK)
        x = x_ref[:, sl]
        out, h = _lru_tile(x, prev_ref[0, :, sl], h0_ref[0, :, sl], g_ref[:, sl],
                           cw_ref[:, sl], cb_ref[:, sl], wa_ref[n], ba_ref[:, sl], wx_ref[n],
                           bx_ref[:, sl], lam_ref[:, sl])
        o_ref[:, sl] = out.astype(BF16)
        conv_ref[0, :, sl] = x[t - SUBLANES:]
        h_ref[0, :, sl] = h[t - SUBLANES:]


def _lru_sample(z, state_conv, state_h, lw, row0, n_batch, t, col_x, col_g):
    conv_w, conv_b, w_a, b_a, w_x, b_x, lam = lw
    n_blocks = w_a.shape[0]
    width = n_blocks * LRU_BLOCK
    k = conv_w.shape[0]
    assert t >= SUBLANES and k - 1 <= SUBLANES
    prev = jnp.pad(state_conv, ((0, 0), (SUBLANES - (k - 1), 0), (0, 0)))
    h0 = state_h.reshape(n_batch, 1, width)
    rb0 = row0 // t
    cx, cg = col_x // width, col_g // width
    vec = lambda a: a.reshape(1, width)
    vspec = pl.BlockSpec((1, width), lambda b: (0, 0))
    wspec = pl.BlockSpec((n_blocks, LRU_BLOCK, LRU_BLOCK), lambda b: (0, 0, 0))
    sspec = pl.BlockSpec((1, SUBLANES, width), lambda b: (b, 0, 0))
    return pl.pallas_call(
        functools.partial(_lru_sample_kernel, n_blocks=n_blocks),
        out_shape=(
            jax.ShapeDtypeStruct((n_batch * t, width), BF16),
            jax.ShapeDtypeStruct((n_batch, SUBLANES, width), F32),
            jax.ShapeDtypeStruct((n_batch, SUBLANES, width), F32),
        ),
        grid=(n_batch,),
        in_specs=[
            pl.BlockSpec((t, width), lambda b: (rb0 + b, cx)),
            pl.BlockSpec((t, width), lambda b: (rb0 + b, cg)),
            sspec,
            pl.BlockSpec((1, 1, width), lambda b: (b, 0, 0)),
            pl.BlockSpec((k, width), lambda b: (0, 0)),
            vspec, wspec, vspec, wspec, vspec, vspec,
        ],
        out_specs=(pl.BlockSpec((t, width), lambda b: (b, 0)), sspec, sspec),
        compiler_params=_params(("arbitrary",), 32),
        name="lru_sample",
    )(z, z, prev, h0, conv_w, vec(conv_b), w_a, vec(b_a), w_x, vec(b_x), vec(lam))


def _out_proj_kernel(m_ref, w_ref, x_ref, g_ref, o_ref, wb_ref):
    @pl.when(pl.program_id(1) == 0)
    def _():
        wb_ref[...] = w_ref[...].astype(BF16)

    acc = jnp.dot(m_ref[...], wb_ref[...], preferred_element_type=F32)
    o_ref[...] = x_ref[...] + _group_scale(acc, g_ref[...])


def _out_proj(mix, w_out, x, table, gate_col, group):
    rows, d = mix.shape
    n = w_out.shape[1]
    tm, tn = 512, 512
    gcol0 = gate_col * (n // tn)
    return pl.pallas_call(
        _out_proj_kernel,
        out_shape=jax.ShapeDtypeStruct((rows, n), F32),
        grid=(n // tn, rows // tm),
        in_specs=[
            pl.BlockSpec((tm, d), lambda j, i: (i, 0)),
            pl.BlockSpec((d, tn), lambda j, i: (0, j)),
            pl.BlockSpec((tm, tn), lambda j, i: (i, j)),
            pl.BlockSpec((tm // group, tn), lambda j, i: (i, gcol0 + j)),
        ],
        out_specs=pl.BlockSpec((tm, tn), lambda j, i: (i, j)),
        scratch_shapes=[pltpu.VMEM((d, tn), BF16)],
        compiler_params=_params(("arbitrary", "arbitrary"), 48),
        name="out_proj",
    )(mix, w_out, x, table)


def _ffn_up_kernel(x_ref, wg_ref, wu_ref, o_ref, wgb_ref, wub_ref, *, n_real):
    j = pl.program_id(0)

    @pl.when(pl.program_id(1) == 0)
    def _():
        wgb_ref[...] = wg_ref[...].astype(BF16)
        wub_ref[...] = wu_ref[...].astype(BF16)

    @pl.when(j < n_real)
    def _():
        x = x_ref[...]
        g = jnp.dot(x, wgb_ref[...], preferred_element_type=F32)
        u = jnp.dot(x, wub_ref[...], preferred_element_type=F32)
        o_ref[...] = (g * _sigmoid(g) * u).astype(BF16)

    @pl.when(j >= n_real)
    def _():
        o_ref[...] = jnp.zeros_like(o_ref)


def _ffn_up(u, w_gate, w_up, ff_pad):
    rows, d = u.shape
    ff = w_gate.shape[1]
    tm, tn = 512, 256
    n_real = ff // tn
    assert ff % tn == 0 and ff_pad % tn == 0
    wspec = pl.BlockSpec((d, tn), lambda j, i: (0, jnp.minimum(j, n_real - 1)))
    return pl.pallas_call(
        functools.partial(_ffn_up_kernel, n_real=n_real),
        out_shape=jax.ShapeDtypeStruct((rows, ff_pad), BF16),
        grid=(ff_pad // tn, rows // tm),
        in_specs=[pl.BlockSpec((tm, d), lambda j, i: (i, 0)), wspec, wspec],
        out_specs=pl.BlockSpec((tm, tn), lambda j, i: (i, j)),
        scratch_shapes=[pltpu.VMEM((d, tn), BF16), pltpu.VMEM((d, tn), BF16)],
        compiler_params=_params(("arbitrary", "arbitrary"), 48),
        name="ffn_up",
    )(u, w_gate, w_up)


def _ffn_down_kernel(h_ref, w_ref, x_ref, g_ref, o_ref, *, k_total):
    k = pl.program_id(2)
    tk = w_ref.shape[0]
    row = lax.broadcasted_iota(jnp.int32, w_ref.shape, 0)
    w = jnp.where(row < k_total - k * tk, w_ref[...], 0.0).astype(BF16)

    @pl.when(k == 0)
    def _():
        o_ref[...] = jnp.zeros_like(o_ref)

    tm = o_ref.shape[0]
    rows = min(tm, 512)
    for r0 in range(0, tm, rows):
        o_ref[r0:r0 + rows, :] += jnp.dot(h_ref[r0:r0 + rows, :], w, preferred_element_type=F32)

    @pl.when(k == pl.num_programs(2) - 1)
    def _():
        o_ref[...] = x_ref[...] + _group_scale(o_ref[...], g_ref[...])


def _ffn_down(h, w_down, x1, table, gate_col, group, row0, rows, tm):
    ff_pad = h.shape[1]
    ff, n = w_down.shape
    tn, tk = 1024, 1024
    assert row0 % tm == 0 and rows % tm == 0 and ff_pad % tk == 0
    rb0 = row0 // tm
    gcol0 = gate_col * (n // tn)
    return pl.pallas_call(
        functools.partial(_ffn_down_kernel, k_total=ff),
        out_shape=jax.ShapeDtypeStruct((rows, n), F32),
        grid=(rows // tm, n // tn, ff_pad // tk),
        in_specs=[
            pl.BlockSpec((tm, tk), lambda i, j, k: (rb0 + i, k)),
            pl.BlockSpec((tk, tn), lambda i, j, k: (k, j)),
            pl.BlockSpec((tm, tn), lambda i, j, k: (rb0 + i, j), pipeline_mode=pl.Buffered(1)),
            pl.BlockSpec((tm // group, tn), lambda i, j, k: (rb0 + i, gcol0 + j)),
        ],
        out_specs=pl.BlockSpec((tm, tn), lambda i, j, k: (i, j)),
        compiler_params=_params(("arbitrary", "arbitrary", "arbitrary"), 52),
        name="ffn_down",
    )(h, w_down, x1, table)


def _layer(x, c_prompt, c_sample, cache_k, cache_v, state_conv, state_h, lw, dims):
    (norm_mix_g, norm_ffn_g, w_mod, b_mod, w_in, q_norm_g, k_norm_g, rel_bias,
     conv_w, conv_b, w_rg_a, b_rg_a, w_rg_x, b_rg_x, lru_lambda, w_out,
     w_ffn_gate, w_ffn_up, w_ffn_down) = lw
    n_prompt, seq, n_sample, t_s = dims
    d = x.shape[1]
    group = t_s
    n_heads = rel_bias.shape[0]
    attn_w = n_heads * HEAD_DIM
    lru_w = w_rg_a.shape[0] * LRU_BLOCK
    rows_p = n_prompt * seq
    rows_s = n_sample * t_s

    table = _mod_table(c_prompt, c_sample, w_mod, b_mod, seq // group)
    u = _norm_mod(x, norm_mix_g, table, 0, 1, group)
    qk_gain = jnp.concatenate([jnp.tile(q_norm_g, n_heads), jnp.tile(k_norm_g, n_heads)]).reshape(1, -1)
    z = _in_proj(u, w_in, qk_gain, attn_w)

    attn_p = _attn_prompt(z, rel_bias, n_prompt, seq, n_heads)
    attn_s = _attn_sample(z, cache_k, cache_v, rel_bias, rows_p, n_sample, t_s, n_heads)
    lru_params = (conv_w, conv_b, w_rg_a, b_rg_a, w_rg_x, b_rg_x, lru_lambda)
    col_x, col_g = 3 * attn_w, 3 * attn_w + lru_w
    lru_p, conv_p, h_p = _lru_prompt(z, lru_params, n_prompt, seq, col_x, col_g)
    lru_s, conv_s, h_s = _lru_sample(z, state_conv, state_h, lru_params, rows_p, n_sample, t_s,
                                     col_x, col_g)
    mix = jnp.concatenate([jnp.concatenate([attn_p, lru_p], axis=1),
                           jnp.concatenate([attn_s, lru_s], axis=1)], axis=0)

    x1 = _out_proj(mix, w_out, x, table, 2, group)
    u2 = _norm_mod(x1, norm_ffn_g, table, 3, 4, group)
    ff = w_ffn_gate.shape[1]
    ff_pad = -(-ff // 1024) * 1024
    hidden = _ffn_up(u2, w_ffn_gate, w_ffn_up, ff_pad)
    y_p = _ffn_down(hidden, w_ffn_down, x1, table, 5, group, 0, rows_p, 2048)
    y_s = _ffn_down(hidden, w_ffn_down, x1, table, 5, group, rows_p, rows_s, rows_s)

    keep = min(N_PREV_CHUNKS * CHUNK, seq)
    kv_p = jnp.stack([z[(b + 1) * seq - keep:(b + 1) * seq, attn_w:3 * attn_w] for b in range(n_prompt)])
    kv_p = kv_p.reshape(n_prompt, keep, 2, n_heads, HEAD_DIM)
    kv_s = z[rows_p:, attn_w:3 * attn_w].reshape(n_sample, t_s, 2, n_heads, HEAD_DIM)
    k_conv = conv_w.shape[0] - 1
    state = dict(
        k_p=kv_p[:, :, 0], v_p=kv_p[:, :, 1], conv_p=conv_p[:, SUBLANES - k_conv:], h_p=h_p[:, SUBLANES - 1],
        k_s=kv_s[:, :, 0], v_s=kv_s[:, :, 1], conv_s=conv_s[:, SUBLANES - k_conv:], h_s=h_s[:, SUBLANES - 1])
    return y_p, y_s, state


def kernel(x_prompt, x_sample, cache_k, cache_v, state_conv, state_h, c_prompt, c_sample, norm_mix_g, norm_ffn_g, w_mod, b_mod, w_in, q_norm_g, k_norm_g, rel_bias, conv_w, conv_b, w_rg_a, b_rg_a, w_rg_x, b_rg_x, lru_lambda, w_out, w_ffn_gate, w_ffn_up, w_ffn_down):
    n_prompt, seq, d = x_prompt.shape
    n_sample, t_s, _ = x_sample.shape
    depth = w_in.shape[0]
    dims = (n_prompt, seq, n_sample, t_s)
    yp = x_prompt.reshape(n_prompt * seq, d)
    ys = x_sample.reshape(n_sample * t_s, d)
    states = []
    for l in range(depth):
        lw = (norm_mix_g[l], norm_ffn_g[l], w_mod[l], b_mod[l], w_in[l], q_norm_g[l], k_norm_g[l],
              rel_bias[l], conv_w[l], conv_b[l], w_rg_a[l], b_rg_a[l], w_rg_x[l], b_rg_x[l],
              lru_lambda[l], w_out[l], w_ffn_gate[l], w_ffn_up[l], w_ffn_down[l])
        x = jnp.concatenate([yp, ys], axis=0)
        yp, ys, st = _layer(x, c_prompt, c_sample, cache_k[l], cache_v[l], state_conv[l],
                            state_h[l], lw, dims)
        states.append(st)
    stack = lambda name: jnp.stack([s[name] for s in states])
    return (yp.reshape(n_prompt, seq, d), ys.reshape(n_sample, t_s, d),
            stack("k_p"), stack("v_p"), stack("conv_p"), stack("h_p"),
            stack("k_s"), stack("v_s"), stack("conv_s"), stack("h_s"))
```

```python
import functools
import math

import jax
import jax.numpy as jnp
from jax import lax
from jax.experimental import pallas as pl
from jax.experimental.pallas import tpu as pltpu

F32 = jnp.float32
BF16 = jnp.bfloat16

CHUNK = 64
N_PREV_CHUNKS = 8
HEAD_DIM = 128
LRU_BLOCK = 128
LRU_C = 8.0
NEG_INF = -1e30
EPS = 1e-6
LOG2E = math.log2(math.e)

LANES = 128
SUBLANES = 8
MIB = 1024 * 1024


def _params(semantics, vmem_mib):
    return pltpu.CompilerParams(dimension_semantics=semantics, vmem_limit_bytes=vmem_mib * MIB)


def _sigmoid(x):
    return 0.5 * jnp.tanh(0.5 * x) + 0.5


def _gelu_tanh(x):
    c = math.sqrt(2.0 / math.pi)
    return x * (0.5 * (1.0 + jnp.tanh(c * (x + 0.044715 * (x * x * x)))))


def _group_scale(y, g):
    rows, n = y.shape
    groups = g.shape[0]
    return (y.reshape(groups, rows // groups, n) * g[:, None, :]).reshape(rows, n)


PROMPT_TILE_ROWS = 1024


class _RowGroups:
    def __init__(self, rows_p, rows_s, tile=PROMPT_TILE_ROWS):
        assert rows_p % tile == 0
        self.rows_p, self.rows_s, self.tile = rows_p, rows_s, tile
        self.np_tiles = rows_p // tile
        self.steps = self.np_tiles + 1

    def prompt_spec(self, cols, col_block, rows_per_row=1):
        last = self.np_tiles - 1
        return pl.BlockSpec((self.tile // rows_per_row, cols),
                            lambda j, i: (jnp.minimum(i, last), col_block(j)))

    def sample_spec(self, cols, col_block, rows_per_row=1, row_block=0, resident=False):
        mode = dict(pipeline_mode=pl.Buffered(1)) if resident else {}
        return pl.BlockSpec((self.rows_s // rows_per_row, cols),
                            lambda j, i: (row_block, col_block(j)), **mode)


def _per_row_group(i, n_prompt_tiles, prompt_fn, sample_fn):
    pl.when(i < n_prompt_tiles)(prompt_fn)
    pl.when(i == n_prompt_tiles)(sample_fn)


def _mod_kernel(c_ref, w_ref, b_ref, o_ref, *, n_sample, n_prompt, groups_per_prompt):
    c = c_ref[...]
    s = (c * _sigmoid(c)).astype(BF16)
    m = jnp.dot(s, w_ref[...].astype(BF16), preferred_element_type=F32) + b_ref[...]
    tn = m.shape[1]
    for b in range(n_prompt):
        row = m[n_sample + b:n_sample + b + 1, :]
        o_ref[b * groups_per_prompt:(b + 1) * groups_per_prompt, :] = jnp.broadcast_to(
            row, (groups_per_prompt, tn))
    o_ref[n_prompt * groups_per_prompt:n_prompt * groups_per_prompt + n_sample, :] = m[:n_sample, :]


def _mod_table(c_prompt, c_sample, w_mod, b_mod, groups_per_prompt):
    n_prompt, d = c_prompt.shape
    n_sample = c_sample.shape[0]
    n_out = w_mod.shape[1]
    rows = n_sample + n_prompt
    rows_pad = -(-rows // SUBLANES) * SUBLANES
    c_all = jnp.concatenate(
        [c_sample, c_prompt, jnp.zeros((rows_pad - rows, d), c_prompt.dtype)], axis=0)
    n_groups = n_prompt * groups_per_prompt + n_sample
    tn = 512
    return pl.pallas_call(
        functools.partial(_mod_kernel, n_sample=n_sample, n_prompt=n_prompt,
                          groups_per_prompt=groups_per_prompt),
        out_shape=jax.ShapeDtypeStruct((n_groups, n_out), F32),
        grid=(n_out // tn,),
        in_specs=[
            pl.BlockSpec((rows_pad, d), lambda j: (0, 0)),
            pl.BlockSpec((d, tn), lambda j: (0, j)),
            pl.BlockSpec((1, tn), lambda j: (0, j)),
        ],
        out_specs=pl.BlockSpec((n_groups, tn), lambda j: (0, j)),
        compiler_params=_params(("arbitrary",), 40),
        name="mod_table",
    )(c_all, w_mod, b_mod.reshape(1, n_out))


def _norm_mod_kernel(x_ref, gain_ref, shift_ref, scale_ref, o_ref):
    x = x_ref[...]
    ms = jnp.mean(x * x, axis=-1, keepdims=True)
    y = x * lax.rsqrt(ms + EPS) * gain_ref[...]
    rows, d = y.shape
    groups = scale_ref.shape[0]
    y3 = y.reshape(groups, rows // groups, d)
    u = y3 * (1.0 + scale_ref[...][:, None, :]) + shift_ref[...][:, None, :]
    o_ref[...] = u.reshape(rows, d).astype(BF16)


def _norm_mod(x, gain, table, shift_col, scale_col, group, group0):
    rows, d = x.shape
    tm = 256
    gt = tm // group
    assert rows % tm == 0 and group0 % gt == 0
    g0 = group0 // gt
    return pl.pallas_call(
        _norm_mod_kernel,
        out_shape=jax.ShapeDtypeStruct((rows, d), BF16),
        grid=(rows // tm,),
        in_specs=[
            pl.BlockSpec((tm, d), lambda i: (i, 0)),
            pl.BlockSpec((1, d), lambda i: (0, 0)),
            pl.BlockSpec((gt, d), lambda i: (g0 + i, shift_col)),
            pl.BlockSpec((gt, d), lambda i: (g0 + i, scale_col)),
        ],
        out_specs=pl.BlockSpec((tm, d), lambda i: (i, 0)),
        compiler_params=_params(("arbitrary",), 40),
        name="norm_mod",
    )(x, gain.reshape(1, d), table, table)


def _in_proj_kernel(xp_ref, xs_ref, w_ref, gain_ref, op_ref, os_ref, wb_ref, *,
                    n_norm_tiles, n_prompt_tiles):
    j = pl.program_id(0)
    i = pl.program_id(1)

    @pl.when(i == 0)
    def _():
        wb_ref[...] = w_ref[...].astype(BF16)

    def tile(x_ref, o_ref):
        acc = jnp.dot(x_ref[...], wb_ref[...], preferred_element_type=F32)
        tn = acc.shape[1]

        @pl.when(j < n_norm_tiles)
        def _():
            for g in range(tn // HEAD_DIM):
                sl = slice(g * HEAD_DIM, (g + 1) * HEAD_DIM)
                zg = acc[:, sl]
                ms = jnp.mean(zg * zg, axis=-1, keepdims=True)
                o_ref[:, sl] = zg * lax.rsqrt(ms + EPS) * gain_ref[:, sl]

        @pl.when(j >= n_norm_tiles)
        def _():
            o_ref[...] = acc

    _per_row_group(i, n_prompt_tiles, lambda: tile(xp_ref, op_ref), lambda: tile(xs_ref, os_ref))


def _in_proj(u_p, u_s, w_in, qk_gain, attn_width):
    d, n = w_in.shape
    tn = 512
    n_norm_tiles = 2 * attn_width // tn
    rg = _RowGroups(u_p.shape[0], u_s.shape[0])
    return pl.pallas_call(
        functools.partial(_in_proj_kernel, n_norm_tiles=n_norm_tiles, n_prompt_tiles=rg.np_tiles),
        out_shape=(jax.ShapeDtypeStruct((rg.rows_p, n), F32), jax.ShapeDtypeStruct((rg.rows_s, n), F32)),
        grid=(n // tn, rg.steps),
        in_specs=[
            rg.prompt_spec(d, lambda j: 0),
            rg.sample_spec(d, lambda j: 0, resident=True),
            pl.BlockSpec((d, tn), lambda j, i: (0, j)),
            pl.BlockSpec((1, tn), lambda j, i: (0, jnp.minimum(j, n_norm_tiles - 1))),
        ],
        out_specs=(rg.prompt_spec(tn, lambda j: j), rg.sample_spec(tn, lambda j: j)),
        scratch_shapes=[pltpu.VMEM((d, tn), BF16)],
        compiler_params=_params(("arbitrary", "arbitrary"), 56),
        name="in_proj",
    )(u_p, u_s, w_in, qk_gain)


def _bias_ramp(rel_bias, top, length):
    n_heads, n_rel = rel_bias.shape
    clip = (n_rel - 1) // 2
    lead = top - clip
    tail = length - lead - n_rel
    assert lead >= 0 and tail >= 0
    rev = rel_bias[:, ::-1]
    ext = jnp.concatenate([
        jnp.broadcast_to(rel_bias[:, n_rel - 1:], (n_heads, lead)),
        rev,
        jnp.broadcast_to(rel_bias[:, :1], (n_heads, tail)),
    ], axis=1)
    return ext.reshape(n_heads, 1, length)


def _lane_reduce(tiles, combine, reduce):
    parts = [t[:, c:c + LANES] for t in tiles for c in range(0, t.shape[1], LANES)]
    return reduce(functools.reduce(combine, parts), axis=-1, keepdims=True)


def _toeplitz(base, rows):
    w = base.shape[1]
    return pltpu.roll(jnp.broadcast_to(base, (rows, w)), w - (rows - 1), 1, stride=1, stride_axis=0)


def _attn_prompt_kernel(q_ref, ka_ref, kb_ref, kc_ref, va_ref, vb_ref, vc_ref, ext_ref,
                        o_ref, bias_ref, *, scale):
    i = pl.program_id(2)
    tq = q_ref.shape[0]
    heads = q_ref.shape[1] // HEAD_DIM

    @pl.when(i == 0)
    def _():
        shift = CHUNK.bit_length() - 1
        rowc = jnp.right_shift(lax.broadcasted_iota(jnp.int32, (tq, tq), 0), shift)
        colc = jnp.right_shift(lax.broadcasted_iota(jnp.int32, (tq, tq), 1), shift)
        for h in range(heads):
            for m in range(3):
                off = (2 - m) * tq
                t = _toeplitz(ext_ref[h, :, off:off + 2 * tq], tq)[:, :tq]
                if m == 0:
                    t = jnp.where(colc <= rowc, t, NEG_INF)
                if m == 2:
                    t = jnp.where(rowc <= colc, t, NEG_INF)
                bias_ref[h, m] = t * LOG2E

    dn = (((1,), (1,)), ((), ()))
    for h in range(heads):
        sl = slice(h * HEAD_DIM, (h + 1) * HEAD_DIM)
        q = q_ref[:, sl].astype(BF16)
        scores = []
        for m, k_ref in ((2, ka_ref), (1, kb_ref), (0, kc_ref)):
            s = lax.dot_general(q, k_ref[:, sl].astype(BF16), dn, preferred_element_type=F32)
            s = s * (scale * LOG2E) + bias_ref[h, m]
            if m > 0:
                s = s + jnp.where(i >= m, 0.0, NEG_INF)
            scores.append(s)
        mx = _lane_reduce(scores, jnp.maximum, jnp.max)
        probs = [jnp.exp2(s - mx) for s in scores]
        denom = _lane_reduce(probs, jnp.add, jnp.sum)
        out = functools.reduce(jnp.add, [
            jnp.dot(p.astype(BF16), v_ref[:, sl].astype(BF16), preferred_element_type=F32)
            for p, v_ref in zip(probs, (va_ref, vb_ref, vc_ref))])
        o_ref[:, sl] = (out / denom).astype(BF16)


def _attn_prompt(z, rel_bias, n_batch, seq, n_heads):
    tq = 256
    hb = 4
    assert tq == (N_PREV_CHUNKS * CHUNK) // 2 and seq % tq == 0 and n_heads % hb == 0
    nq = seq // tq
    ng = n_heads // hb
    clip = (rel_bias.shape[1] - 1) // 2
    ext = _bias_ramp(rel_bias, 3 * tq - 1, 4 * tq)
    assert 3 * tq - 1 >= clip

    def qmap(b, g, i):
        return (b * nq + i, g)

    def kvmap(back, col0):
        return lambda b, g, i: (b * nq + jnp.maximum(i - back, 0), col0 + g)

    blk = pl.BlockSpec((tq, hb * HEAD_DIM), qmap)
    kvspec = lambda back, col0: pl.BlockSpec((tq, hb * HEAD_DIM), kvmap(back, col0))
    return pl.pallas_call(
        functools.partial(_attn_prompt_kernel, scale=HEAD_DIM ** -0.5),
        out_shape=jax.ShapeDtypeStruct((n_batch * seq, n_heads * HEAD_DIM), BF16),
        grid=(n_batch, ng, nq),
        in_specs=[
            blk,
            kvspec(2, ng), kvspec(1, ng), kvspec(0, ng),
            kvspec(2, 2 * ng), kvspec(1, 2 * ng), kvspec(0, 2 * ng),
            pl.BlockSpec((hb, 1, 4 * tq), lambda b, g, i: (g, 0, 0)),
        ],
        out_specs=blk,
        scratch_shapes=[pltpu.VMEM((hb, 3, tq, tq), F32)],
        compiler_params=_params(("arbitrary", "arbitrary", "arbitrary"), 32),
        name="attn_prompt",
    )(z, z, z, z, z, z, z, ext)


def _attn_sample_kernel(q_ref, kn_ref, vn_ref, ck_ref, cv_ref, ext_ref, o_ref, *, scale, n_heads):
    t = q_ref.shape[0]
    r = ck_ref.shape[1] // n_heads
    dn = (((1,), (1,)), ((), ()))
    for h in range(n_heads):
        sl = slice(h * HEAD_DIM, (h + 1) * HEAD_DIM)
        rows_h = pl.ds(h, r, stride=n_heads)
        bias = _toeplitz(ext_ref[h], t)
        q = q_ref[:, sl].astype(BF16)
        s1 = lax.dot_general(q, ck_ref[0, rows_h, :].astype(BF16), dn, preferred_element_type=F32)
        s1 = s1 * scale + bias[:, :r]
        s2 = lax.dot_general(q, kn_ref[:, sl].astype(BF16), dn, preferred_element_type=F32)
        s2 = s2 * scale + bias[:, r:r + t]
        mx = jnp.maximum(jnp.max(s1, axis=-1, keepdims=True), jnp.max(s2, axis=-1, keepdims=True))
        p1 = jnp.exp(s1 - mx)
        p2 = jnp.exp(s2 - mx)
        denom = jnp.sum(p1, axis=-1, keepdims=True) + jnp.sum(p2, axis=-1, keepdims=True)
        out = (jnp.dot(p1.astype(BF16), cv_ref[0, rows_h, :].astype(BF16), preferred_element_type=F32)
               + jnp.dot(p2.astype(BF16), vn_ref[:, sl].astype(BF16), preferred_element_type=F32))
        o_ref[:, sl] = (out / denom).astype(BF16)


def _attn_sample(z, cache_k, cache_v, rel_bias, row0, n_batch, t, n_heads):
    r = cache_k.shape[1]
    width = n_heads * HEAD_DIM
    ck = cache_k.reshape(n_batch, r * n_heads, HEAD_DIM)
    cv = cache_v.reshape(n_batch, r * n_heads, HEAD_DIM)
    top = r + t - 1
    clip = (rel_bias.shape[1] - 1) // 2
    ramp = -(-max(r + 2 * t, top + clip + 1) // LANES) * LANES
    ext = _bias_ramp(rel_bias, top, ramp)
    rb0 = row0 // t
    return pl.pallas_call(
        functools.partial(_attn_sample_kernel, scale=HEAD_DIM ** -0.5, n_heads=n_heads),
        out_shape=jax.ShapeDtypeStruct((n_batch * t, width), BF16),
        grid=(n_batch,),
        in_specs=[
            pl.BlockSpec((t, width), lambda b: (rb0 + b, 0)),
            pl.BlockSpec((t, width), lambda b: (rb0 + b, 1)),
            pl.BlockSpec((t, width), lambda b: (rb0 + b, 2)),
            pl.BlockSpec((1, r * n_heads, HEAD_DIM), lambda b: (b, 0, 0)),
            pl.BlockSpec((1, r * n_heads, HEAD_DIM), lambda b: (b, 0, 0)),
            pl.BlockSpec((n_heads, 1, ramp), lambda b: (0, 0, 0)),
        ],
        out_specs=pl.BlockSpec((t, width), lambda b: (b, 0)),
        compiler_params=_params(("arbitrary",), 40),
        name="attn_sample",
    )(z, z, z, ck, cv, ext)


def _lru_tile(x, prev, h_in, gate_in, cw, cb, wa, ba, wx, bx, lam):
    t = x.shape[0]
    p = prev.shape[0]
    k = cw.shape[0]
    xcat = jnp.concatenate([prev, x], axis=0)
    xc = cb
    for i in range(k):
        lo = p - (k - 1) + i
        xc = xc + xcat[lo:lo + t] * cw[i:i + 1]
    xcb = xc.astype(BF16)
    rg = _sigmoid(jnp.dot(xcb, wa.astype(BF16), preferred_element_type=F32) + ba)
    ig = _sigmoid(jnp.dot(xcb, wx.astype(BF16), preferred_element_type=F32) + bx)
    neg = -lam
    softplus = jnp.maximum(neg, 0.0) + jnp.log1p(jnp.exp(-jnp.abs(neg)))
    log_a = (-LRU_C) * rg * softplus
    a = jnp.exp(log_a)
    var = -jnp.tanh(log_a) * (a * a + 1.0)
    std = jnp.where(var == 0.0, 0.0, var * lax.rsqrt(var))
    b = std * (ig * xc)
    h = _linear_scan(a, b, h_in)
    return h * _gelu_tanh(gate_in), h


def _doubling_scan(a, b, axis):
    n = a.shape[axis]
    idx = lax.broadcasted_iota(jnp.int32, a.shape, axis)
    step = 1
    while step < n:
        keep = idx >= step
        a_prev = jnp.where(keep, pltpu.roll(a, step, axis), 1.0)
        b_prev = jnp.where(keep, pltpu.roll(b, step, axis), 0.0)
        b = a * b_prev + b
        a = a * a_prev
        step *= 2
    return a, b


def _linear_scan(a, b, h_in):
    t, c = a.shape
    groups = t // SUBLANES
    a3, b3 = _doubling_scan(a.reshape(groups, SUBLANES, c), b.reshape(groups, SUBLANES, c), 1)
    ae = jnp.broadcast_to(a3[:, SUBLANES - 1:, :], a3.shape).reshape(t, c)
    be = jnp.broadcast_to(b3[:, SUBLANES - 1:, :], b3.shape).reshape(t, c)
    step = SUBLANES
    while step < t:
        be = jnp.concatenate([be[:step], ae[step:] * be[:-step] + be[step:]], axis=0)
        ae = jnp.concatenate([ae[:step], ae[step:] * ae[:-step]], axis=0)
        step *= 2
    h_end = ae * h_in + be
    h_start = jnp.concatenate([jnp.broadcast_to(h_in, (SUBLANES, c)), h_end[:t - SUBLANES]], axis=0)
    return a3.reshape(t, c) * h_start + b3.reshape(t, c)


def _lru_prompt_kernel(x_ref, g_ref, cw_ref, cb_ref, wa_ref, ba_ref, wx_ref, bx_ref, lam_ref,
                       o_ref, conv_ref, h_ref, px_ref, ph_ref):
    @pl.when(pl.program_id(2) == 0)
    def _():
        px_ref[...] = jnp.zeros_like(px_ref)
        ph_ref[...] = jnp.zeros_like(ph_ref)

    t = x_ref.shape[0]
    for n in range(x_ref.shape[1] // LRU_BLOCK):
        sl = slice(n * LRU_BLOCK, (n + 1) * LRU_BLOCK)
        x = x_ref[:, sl]
        out, h = _lru_tile(x, px_ref[:, sl], ph_ref[SUBLANES - 1:SUBLANES, sl], g_ref[:, sl],
                           cw_ref[:, sl], cb_ref[:, sl], wa_ref[n], ba_ref[:, sl], wx_ref[n],
                           bx_ref[:, sl], lam_ref[:, sl])
        o_ref[:, sl] = out.astype(BF16)
        px_ref[:, sl] = x[t - SUBLANES:]
        ph_ref[:, sl] = h[t - SUBLANES:]
        conv_ref[0, :, sl] = x[t - SUBLANES:]
        h_ref[0, :, sl] = h[t - SUBLANES:]


def _lru_prompt(z, lw, n_batch, seq, col_x, col_g):
    conv_w, conv_b, w_a, b_a, w_x, b_x, lam = lw
    n_blocks = w_a.shape[0]
    width = n_blocks * LRU_BLOCK
    tt = 256
    nb = 4
    lanes = nb * LRU_BLOCK
    assert seq % tt == 0 and n_blocks % nb == 0 and col_x % lanes == 0 and col_g % lanes == 0
    nt = seq // tt
    cx, cg = col_x // lanes, col_g // lanes
    vec = lambda a: a.reshape(1, width)
    vspec = pl.BlockSpec((1, lanes), lambda b, n, t: (0, n))
    wspec = pl.BlockSpec((nb, LRU_BLOCK, LRU_BLOCK), lambda b, n, t: (n, 0, 0))
    sspec = pl.BlockSpec((1, SUBLANES, lanes), lambda b, n, t: (b, 0, n))
    return pl.pallas_call(
        _lru_prompt_kernel,
        out_shape=(
            jax.ShapeDtypeStruct((n_batch * seq, width), BF16),
            jax.ShapeDtypeStruct((n_batch, SUBLANES, width), F32),
            jax.ShapeDtypeStruct((n_batch, SUBLANES, width), F32),
        ),
        grid=(n_batch, n_blocks // nb, nt),
        in_specs=[
            pl.BlockSpec((tt, lanes), lambda b, n, t: (b * nt + t, cx + n)),
            pl.BlockSpec((tt, lanes), lambda b, n, t: (b * nt + t, cg + n)),
            pl.BlockSpec((conv_w.shape[0], lanes), lambda b, n, t: (0, n)),
            vspec, wspec, vspec, wspec, vspec, vspec,
        ],
        out_specs=(
            pl.BlockSpec((tt, lanes), lambda b, n, t: (b * nt + t, n)),
            sspec, sspec,
        ),
        scratch_shapes=[pltpu.VMEM((SUBLANES, lanes), F32), pltpu.VMEM((SUBLANES, lanes), F32)],
        compiler_params=_params(("arbitrary", "arbitrary", "arbitrary"), 32),
        name="lru_prompt",
    )(z, z, conv_w, vec(conv_b), w_a, vec(b_a), w_x, vec(b_x), vec(lam))


def _lru_sample_kernel(x_ref, g_ref, prev_ref, h0_ref, cw_ref, cb_ref, wa_ref, ba_ref, wx_ref,
                       bx_ref, lam_ref, o_ref, conv_ref, h_ref, *, n_blocks):
    t = x_ref.shape[0]
    for n in range(n_blocks):
        sl = slice(n * LRU_BLOCK, (n + 1) * LRU_BLOCK)
        x = x_ref[:, sl]
        out, h = _lru_tile(x, prev_ref[0, :, sl], h0_ref[0, :, sl], g_ref[:, sl],
                           cw_ref[:, sl], cb_ref[:, sl], wa_ref[n], ba_ref[:, sl], wx_ref[n],
                           bx_ref[:, sl], lam_ref[:, sl])
        o_ref[:, sl] = out.astype(BF16)
        conv_ref[0, :, sl] = x[t - SUBLANES:]
        h_ref[0, :, sl] = h[t - SUBLANES:]


def _lru_sample(z, state_conv, state_h, lw, row0, n_batch, t, col_x, col_g):
    conv_w, conv_b, w_a, b_a, w_x, b_x, lam = lw
    n_blocks = w_a.shape[0]
    width = n_blocks * LRU_BLOCK
    k = conv_w.shape[0]
    assert t >= SUBLANES and k - 1 <= SUBLANES
    prev = jnp.pad(state_conv, ((0, 0), (SUBLANES - (k - 1), 0), (0, 0)))
    h0 = state_h.reshape(n_batch, 1, width)
    rb0 = row0 // t
    cx, cg = col_x // width, col_g // width
    vec = lambda a: a.reshape(1, width)
    vspec = pl.BlockSpec((1, width), lambda b: (0, 0))
    wspec = pl.BlockSpec((n_blocks, LRU_BLOCK, LRU_BLOCK), lambda b: (0, 0, 0))
    sspec = pl.BlockSpec((1, SUBLANES, width), lambda b: (b, 0, 0))
    return pl.pallas_call(
        functools.partial(_lru_sample_kernel, n_blocks=n_blocks),
        out_shape=(
            jax.ShapeDtypeStruct((n_batch * t, width), BF16),
            jax.ShapeDtypeStruct((n_batch, SUBLANES, width), F32),
            jax.ShapeDtypeStruct((n_batch, SUBLANES, width), F32),
        ),
        grid=(n_batch,),
        in_specs=[
            pl.BlockSpec((t, width), lambda b: (rb0 + b, cx)),
            pl.BlockSpec((t, width), lambda b: (rb0 + b, cg)),
            sspec,
            pl.BlockSpec((1, 1, width), lambda b: (b, 0, 0)),
            pl.BlockSpec((k, width), lambda b: (0, 0)),
            vspec, wspec, vspec, wspec, vspec, vspec,
        ],
        out_specs=(pl.BlockSpec((t, width), lambda b: (b, 0)), sspec, sspec),
        compiler_params=_params(("arbitrary",), 32),
        name="lru_sample",
    )(z, z, prev, h0, conv_w, vec(conv_b), w_a, vec(b_a), w_x, vec(b_x), vec(lam))


def _out_proj_kernel(ap_ref, lp_ref, as_ref, ls_ref, w_ref, xp_ref, xs_ref, gp_ref, gs_ref,
                     op_ref, os_ref, wb_ref, *, n_prompt_tiles):
    i = pl.program_id(1)

    @pl.when(i == 0)
    def _():
        wb_ref[...] = w_ref[...].astype(BF16)

    ka = ap_ref.shape[1]

    def tile(a_ref, l_ref, x_ref, g_ref, o_ref):
        acc = (jnp.dot(a_ref[...], wb_ref[:ka, :], preferred_element_type=F32)
               + jnp.dot(l_ref[...], wb_ref[ka:, :], preferred_element_type=F32))
        o_ref[...] = x_ref[...] + _group_scale(acc, g_ref[...])

    _per_row_group(i, n_prompt_tiles,
                   lambda: tile(ap_ref, lp_ref, xp_ref, gp_ref, op_ref),
                   lambda: tile(as_ref, ls_ref, xs_ref, gs_ref, os_ref))


def _out_proj(attn_p, lru_p, attn_s, lru_s, w_out, x_p, x_s, table, gate_col, group):
    d, n = w_out.shape
    ka, kl = attn_p.shape[1], lru_p.shape[1]
    assert ka + kl == d
    tn = 512
    rg = _RowGroups(x_p.shape[0], x_s.shape[0])
    gcol = lambda j: gate_col * (n // tn) + j
    sample_table_block = (rg.rows_p // group) // (rg.rows_s // group)
    return pl.pallas_call(
        functools.partial(_out_proj_kernel, n_prompt_tiles=rg.np_tiles),
        out_shape=(jax.ShapeDtypeStruct((rg.rows_p, n), F32), jax.ShapeDtypeStruct((rg.rows_s, n), F32)),
        grid=(n // tn, rg.steps),
        in_specs=[
            rg.prompt_spec(ka, lambda j: 0),
            rg.prompt_spec(kl, lambda j: 0),
            rg.sample_spec(ka, lambda j: 0, resident=True),
            rg.sample_spec(kl, lambda j: 0, resident=True),
            pl.BlockSpec((d, tn), lambda j, i: (0, j)),
            rg.prompt_spec(tn, lambda j: j),
            rg.sample_spec(tn, lambda j: j),
            rg.prompt_spec(tn, gcol, rows_per_row=group),
            rg.sample_spec(tn, gcol, rows_per_row=group, row_block=sample_table_block),
        ],
        out_specs=(rg.prompt_spec(tn, lambda j: j), rg.sample_spec(tn, lambda j: j)),
        scratch_shapes=[pltpu.VMEM((d, tn), BF16)],
        compiler_params=_params(("arbitrary", "arbitrary"), 58),
        name="out_proj",
    )(attn_p, lru_p, attn_s, lru_s, w_out, x_p, x_s, table, table)


def _ffn_up_kernel(xp_ref, xs_ref, wg_ref, wu_ref, op_ref, os_ref, wgb_ref, wub_ref, *,
                   n_real, n_prompt_tiles):
    j = pl.program_id(0)
    i = pl.program_id(1)

    @pl.when(i == 0)
    def _():
        wgb_ref[...] = wg_ref[...].astype(BF16)
        wub_ref[...] = wu_ref[...].astype(BF16)

    def tile(x_ref, o_ref):
        @pl.when(j < n_real)
        def _():
            x = x_ref[...]
            g = jnp.dot(x, wgb_ref[...], preferred_element_type=F32)
            u = jnp.dot(x, wub_ref[...], preferred_element_type=F32)
            o_ref[...] = (g * _sigmoid(g) * u).astype(BF16)

        @pl.when(j >= n_real)
        def _():
            o_ref[...] = jnp.zeros_like(o_ref)

    _per_row_group(i, n_prompt_tiles, lambda: tile(xp_ref, op_ref), lambda: tile(xs_ref, os_ref))


def _ffn_up(u_p, u_s, w_gate, w_up, ff_pad):
    d, ff = w_gate.shape
    tn = 256
    n_real = ff // tn
    assert ff % tn == 0 and ff_pad % tn == 0
    rg = _RowGroups(u_p.shape[0], u_s.shape[0])
    wspec = pl.BlockSpec((d, tn), lambda j, i: (0, jnp.minimum(j, n_real - 1)))
    return pl.pallas_call(
        functools.partial(_ffn_up_kernel, n_real=n_real, n_prompt_tiles=rg.np_tiles),
        out_shape=(jax.ShapeDtypeStruct((rg.rows_p, ff_pad), BF16),
                   jax.ShapeDtypeStruct((rg.rows_s, ff_pad), BF16)),
        grid=(ff_pad // tn, rg.steps),
        in_specs=[rg.prompt_spec(d, lambda j: 0), rg.sample_spec(d, lambda j: 0, resident=True),
                  wspec, wspec],
        out_specs=(rg.prompt_spec(tn, lambda j: j), rg.sample_spec(tn, lambda j: j)),
        scratch_shapes=[pltpu.VMEM((d, tn), BF16), pltpu.VMEM((d, tn), BF16)],
        compiler_params=_params(("arbitrary", "arbitrary"), 52),
        name="ffn_up",
    )(u_p, u_s, w_gate, w_up)


def _ffn_down_kernel(h_ref, w_ref, x_ref, g_ref, o_ref, *, k_total):
    k = pl.program_id(2)
    tk = w_ref.shape[0]
    row = lax.broadcasted_iota(jnp.int32, w_ref.shape, 0)
    w = jnp.where(row < k_total - k * tk, w_ref[...], 0.0).astype(BF16)

    @pl.when(k == 0)
    def _():
        o_ref[...] = jnp.zeros_like(o_ref)

    tm = o_ref.shape[0]
    rows = min(tm, 512)
    for r0 in range(0, tm, rows):
        o_ref[r0:r0 + rows, :] += jnp.dot(h_ref[r0:r0 + rows, :], w, preferred_element_type=F32)

    @pl.when(k == pl.num_programs(2) - 1)
    def _():
        o_ref[...] = x_ref[...] + _group_scale(o_ref[...], g_ref[...])


def _ffn_down(h, w_down, x1, table, gate_col, group, group0, tm, tn):
    rows, ff_pad = h.shape
    ff, n = w_down.shape
    tk = 1024
    tn = min(tn, n)
    gt = tm // group
    assert rows % tm == 0 and ff_pad % tk == 0 and n % tn == 0 and group0 % gt == 0
    g0 = group0 // gt
    gcol0 = gate_col * (n // tn)
    return pl.pallas_call(
        functools.partial(_ffn_down_kernel, k_total=ff),
        out_shape=jax.ShapeDtypeStruct((rows, n), F32),
        grid=(rows // tm, n // tn, ff_pad // tk),
        in_specs=[
            pl.BlockSpec((tm, tk), lambda i, j, k: (i, k)),
            pl.BlockSpec((tk, tn), lambda i, j, k: (k, j)),
            pl.BlockSpec((tm, tn), lambda i, j, k: (i, j), pipeline_mode=pl.Buffered(1)),
            pl.BlockSpec((gt, tn), lambda i, j, k: (g0 + i, gcol0 + j)),
        ],
        out_specs=pl.BlockSpec((tm, tn), lambda i, j, k: (i, j)),
        compiler_params=_params(("arbitrary", "arbitrary", "arbitrary"), 52),
        name="ffn_down",
    )(h, w_down, x1, table)


def _layer(x_p, x_s, c_prompt, c_sample, cache_k, cache_v, state_conv, state_h, lw, dims):
    (norm_mix_g, norm_ffn_g, w_mod, b_mod, w_in, q_norm_g, k_norm_g, rel_bias,
     conv_w, conv_b, w_rg_a, b_rg_a, w_rg_x, b_rg_x, lru_lambda, w_out,
     w_ffn_gate, w_ffn_up, w_ffn_down) = lw
    n_prompt, seq, n_sample, t_s = dims
    group = t_s
    n_heads = rel_bias.shape[0]
    attn_w = n_heads * HEAD_DIM
    lru_w = w_rg_a.shape[0] * LRU_BLOCK
    rows_p = n_prompt * seq
    rows_s = n_sample * t_s
    groups_p = rows_p // group

    table = _mod_table(c_prompt, c_sample, w_mod, b_mod, seq // group)
    u_p = _norm_mod(x_p, norm_mix_g, table, 0, 1, group, 0)
    u_s = _norm_mod(x_s, norm_mix_g, table, 0, 1, group, groups_p)
    qk_gain = jnp.concatenate([jnp.tile(q_norm_g, n_heads), jnp.tile(k_norm_g, n_heads)]).reshape(1, -1)
    z_p, z_s = _in_proj(u_p, u_s, w_in, qk_gain, attn_w)

    attn_p = _attn_prompt(z_p, rel_bias, n_prompt, seq, n_heads)
    attn_s = _attn_sample(z_s, cache_k, cache_v, rel_bias, 0, n_sample, t_s, n_heads)
    lru_params = (conv_w, conv_b, w_rg_a, b_rg_a, w_rg_x, b_rg_x, lru_lambda)
    col_x, col_g = 3 * attn_w, 3 * attn_w + lru_w
    lru_p, conv_p, h_p = _lru_prompt(z_p, lru_params, n_prompt, seq, col_x, col_g)
    lru_s, conv_s, h_s = _lru_sample(z_s, state_conv, state_h, lru_params, 0, n_sample, t_s,
                                     col_x, col_g)

    x1_p, x1_s = _out_proj(attn_p, lru_p, attn_s, lru_s, w_out, x_p, x_s, table, 2, group)
    u2_p = _norm_mod(x1_p, norm_ffn_g, table, 3, 4, group, 0)
    u2_s = _norm_mod(x1_s, norm_ffn_g, table, 3, 4, group, groups_p)
    ff = w_ffn_gate.shape[1]
    ff_pad = -(-ff // 1024) * 1024
    hid_p, hid_s = _ffn_up(u2_p, u2_s, w_ffn_gate, w_ffn_up, ff_pad)
    y_p = _ffn_down(hid_p, w_ffn_down, x1_p, table, 5, group, 0, 2048, 1024)
    y_s = _ffn_down(hid_s, w_ffn_down, x1_s, table, 5, group, groups_p, rows_s, 2048)

    keep = min(N_PREV_CHUNKS * CHUNK, seq)
    kv_p = jnp.stack([z_p[(b + 1) * seq - keep:(b + 1) * seq, attn_w:3 * attn_w] for b in range(n_prompt)])
    kv_p = kv_p.reshape(n_prompt, keep, 2, n_heads, HEAD_DIM)
    kv_s = z_s[:, attn_w:3 * attn_w].reshape(n_sample, t_s, 2, n_heads, HEAD_DIM)
    k_conv = conv_w.shape[0] - 1
    state = dict(
        k_p=kv_p[:, :, 0], v_p=kv_p[:, :, 1], conv_p=conv_p[:, SUBLANES - k_conv:], h_p=h_p[:, SUBLANES - 1],
        k_s=kv_s[:, :, 0], v_s=kv_s[:, :, 1], conv_s=conv_s[:, SUBLANES - k_conv:], h_s=h_s[:, SUBLANES - 1])
    return y_p, y_s, state


def kernel(x_prompt, x_sample, cache_k, cache_v, state_conv, state_h, c_prompt, c_sample, norm_mix_g, norm_ffn_g, w_mod, b_mod, w_in, q_norm_g, k_norm_g, rel_bias, conv_w, conv_b, w_rg_a, b_rg_a, w_rg_x, b_rg_x, lru_lambda, w_out, w_ffn_gate, w_ffn_up, w_ffn_down):
    n_prompt, seq, d = x_prompt.shape
    n_sample, t_s, _ = x_sample.shape
    depth = w_in.shape[0]
    dims = (n_prompt, seq, n_sample, t_s)
    yp = x_prompt.reshape(n_prompt * seq, d)
    ys = x_sample.reshape(n_sample * t_s, d)
    states = []
    for l in range(depth):
        lw = (norm_mix_g[l], norm_ffn_g[l], w_mod[l], b_mod[l], w_in[l], q_norm_g[l], k_norm_g[l],
              rel_bias[l], conv_w[l], conv_b[l], w_rg_a[l], b_rg_a[l], w_rg_x[l], b_rg_x[l],
              lru_lambda[l], w_out[l], w_ffn_gate[l], w_ffn_up[l], w_ffn_down[l])
        yp, ys, st = _layer(yp, ys, c_prompt, c_sample, cache_k[l], cache_v[l], state_conv[l],
                            state_h[l], lw, dims)
        states.append(st)
    stack = lambda name: jnp.stack([s[name] for s in states])
    return (yp.reshape(n_prompt, seq, d), ys.reshape(n_sample, t_s, d),
            stack("k_p"), stack("v_p"), stack("conv_p"), stack("h_p"),
            stack("k_s"), stack("v_s"), stack("conv_s"), stack("h_s"))
```

```python
import functools
import math

import jax
import jax.numpy as jnp
from jax import lax
from jax.experimental import pallas as pl
from jax.experimental.pallas import tpu as pltpu

F32 = jnp.float32
BF16 = jnp.bfloat16

CHUNK = 64
N_PREV_CHUNKS = 8
HEAD_DIM = 128
LRU_BLOCK = 128
LRU_C = 8.0
NEG_INF = -1e30
EPS = 1e-6
LOG2E = math.log2(math.e)

LANES = 128
SUBLANES = 8
MIB = 1024 * 1024


def _params(semantics, vmem_mib):
    return pltpu.CompilerParams(dimension_semantics=semantics, vmem_limit_bytes=vmem_mib * MIB)


def _sigmoid(x):
    return 0.5 * jnp.tanh(0.5 * x) + 0.5


def _gelu_tanh(x):
    c = math.sqrt(2.0 / math.pi)
    return x * (0.5 * (1.0 + jnp.tanh(c * (x + 0.044715 * (x * x * x)))))


def _group_scale(y, g):
    rows, n = y.shape
    groups = g.shape[0]
    return (y.reshape(groups, rows // groups, n) * g[:, None, :]).reshape(rows, n)


PROMPT_TILE_ROWS = 1024


class _RowGroups:
    def __init__(self, rows_p, rows_s, tile=PROMPT_TILE_ROWS):
        assert rows_p % tile == 0
        self.rows_p, self.rows_s, self.tile = rows_p, rows_s, tile
        self.np_tiles = rows_p // tile
        self.steps = self.np_tiles + 1

    def prompt_spec(self, cols, col_block, rows_per_row=1, output=False):
        first = 0 if output else self.np_tiles - 1
        return pl.BlockSpec((self.tile // rows_per_row, cols),
                            lambda j, i: (jnp.where(i == 0, first, i - 1), col_block(j)))

    def sample_spec(self, cols, col_block, rows_per_row=1, row_block=0, resident=False):
        mode = dict(pipeline_mode=pl.Buffered(1)) if resident else {}
        return pl.BlockSpec((self.rows_s // rows_per_row, cols),
                            lambda j, i: (row_block, col_block(j)), **mode)


def _per_row_group(i, prompt_fn, sample_fn):
    pl.when(i == 0)(sample_fn)
    pl.when(i > 0)(prompt_fn)


def _mod_kernel(c_ref, w_ref, b_ref, o_ref, *, n_sample, n_prompt, groups_per_prompt):
    c = c_ref[...]
    s = (c * _sigmoid(c)).astype(BF16)
    m = jnp.dot(s, w_ref[...].astype(BF16), preferred_element_type=F32) + b_ref[...]
    tn = m.shape[1]
    for b in range(n_prompt):
        row = m[n_sample + b:n_sample + b + 1, :]
        o_ref[b * groups_per_prompt:(b + 1) * groups_per_prompt, :] = jnp.broadcast_to(
            row, (groups_per_prompt, tn))
    o_ref[n_prompt * groups_per_prompt:n_prompt * groups_per_prompt + n_sample, :] = m[:n_sample, :]


def _mod_table(c_prompt, c_sample, w_mod, b_mod, groups_per_prompt):
    n_prompt, d = c_prompt.shape
    n_sample = c_sample.shape[0]
    n_out = w_mod.shape[1]
    rows = n_sample + n_prompt
    rows_pad = -(-rows // SUBLANES) * SUBLANES
    c_all = jnp.concatenate(
        [c_sample, c_prompt, jnp.zeros((rows_pad - rows, d), c_prompt.dtype)], axis=0)
    n_groups = n_prompt * groups_per_prompt + n_sample
    tn = 512
    return pl.pallas_call(
        functools.partial(_mod_kernel, n_sample=n_sample, n_prompt=n_prompt,
                          groups_per_prompt=groups_per_prompt),
        out_shape=jax.ShapeDtypeStruct((n_groups, n_out), F32),
        grid=(n_out // tn,),
        in_specs=[
            pl.BlockSpec((rows_pad, d), lambda j: (0, 0)),
            pl.BlockSpec((d, tn), lambda j: (0, j)),
            pl.BlockSpec((1, tn), lambda j: (0, j)),
        ],
        out_specs=pl.BlockSpec((n_groups, tn), lambda j: (0, j)),
        compiler_params=_params(("arbitrary",), 40),
        name="mod_table",
    )(c_all, w_mod, b_mod.reshape(1, n_out))


def _norm_mod_kernel(x_ref, gain_ref, shift_ref, scale_ref, o_ref):
    x = x_ref[...]
    ms = jnp.mean(x * x, axis=-1, keepdims=True)
    y = x * lax.rsqrt(ms + EPS) * gain_ref[...]
    rows, d = y.shape
    groups = scale_ref.shape[0]
    y3 = y.reshape(groups, rows // groups, d)
    u = y3 * (1.0 + scale_ref[...][:, None, :]) + shift_ref[...][:, None, :]
    o_ref[...] = u.reshape(rows, d).astype(BF16)


def _norm_mod(x, gain, table, shift_col, scale_col, group, group0):
    rows, d = x.shape
    tm = 256
    gt = tm // group
    assert rows % tm == 0 and group0 % gt == 0
    g0 = group0 // gt
    return pl.pallas_call(
        _norm_mod_kernel,
        out_shape=jax.ShapeDtypeStruct((rows, d), BF16),
        grid=(rows // tm,),
        in_specs=[
            pl.BlockSpec((tm, d), lambda i: (i, 0)),
            pl.BlockSpec((1, d), lambda i: (0, 0)),
            pl.BlockSpec((gt, d), lambda i: (g0 + i, shift_col)),
            pl.BlockSpec((gt, d), lambda i: (g0 + i, scale_col)),
        ],
        out_specs=pl.BlockSpec((tm, d), lambda i: (i, 0)),
        compiler_params=_params(("arbitrary",), 40),
        name="norm_mod",
    )(x, gain.reshape(1, d), table, table)


def _in_proj_kernel(xp_ref, xs_ref, w_ref, gain_ref, op_ref, os_ref, wb_ref, *,
                    n_norm_tiles):
    j = pl.program_id(0)
    i = pl.program_id(1)

    @pl.when(i == 0)
    def _():
        wb_ref[...] = w_ref[...].astype(BF16)

    def tile(x_ref, o_ref):
        acc = jnp.dot(x_ref[...], wb_ref[...], preferred_element_type=F32)
        tn = acc.shape[1]

        @pl.when(j < n_norm_tiles)
        def _():
            for g in range(tn // HEAD_DIM):
                sl = slice(g * HEAD_DIM, (g + 1) * HEAD_DIM)
                zg = acc[:, sl]
                ms = jnp.mean(zg * zg, axis=-1, keepdims=True)
                o_ref[:, sl] = zg * lax.rsqrt(ms + EPS) * gain_ref[:, sl]

        @pl.when(j >= n_norm_tiles)
        def _():
            o_ref[...] = acc

    _per_row_group(i, lambda: tile(xp_ref, op_ref), lambda: tile(xs_ref, os_ref))


def _in_proj(u_p, u_s, w_in, qk_gain, attn_width):
    d, n = w_in.shape
    tn = 512
    n_norm_tiles = 2 * attn_width // tn
    rg = _RowGroups(u_p.shape[0], u_s.shape[0])
    return pl.pallas_call(
        functools.partial(_in_proj_kernel, n_norm_tiles=n_norm_tiles),
        out_shape=(jax.ShapeDtypeStruct((rg.rows_p, n), F32), jax.ShapeDtypeStruct((rg.rows_s, n), F32)),
        grid=(n // tn, rg.steps),
        in_specs=[
            rg.prompt_spec(d, lambda j: 0),
            rg.sample_spec(d, lambda j: 0, resident=True),
            pl.BlockSpec((d, tn), lambda j, i: (0, j)),
            pl.BlockSpec((1, tn), lambda j, i: (0, jnp.minimum(j, n_norm_tiles - 1))),
        ],
        out_specs=(rg.prompt_spec(tn, lambda j: j, output=True), rg.sample_spec(tn, lambda j: j)),
        scratch_shapes=[pltpu.VMEM((d, tn), BF16)],
        compiler_params=_params(("arbitrary", "arbitrary"), 56),
        name="in_proj",
    )(u_p, u_s, w_in, qk_gain)


def _bias_ramp(rel_bias, top, length):
    n_heads, n_rel = rel_bias.shape
    clip = (n_rel - 1) // 2
    lead = top - clip
    tail = length - lead - n_rel
    assert lead >= 0 and tail >= 0
    rev = rel_bias[:, ::-1]
    ext = jnp.concatenate([
        jnp.broadcast_to(rel_bias[:, n_rel - 1:], (n_heads, lead)),
        rev,
        jnp.broadcast_to(rel_bias[:, :1], (n_heads, tail)),
    ], axis=1)
    return ext.reshape(n_heads, 1, length)


def _lane_reduce(tiles, combine, reduce):
    parts = [t[:, c:c + LANES] for t in tiles for c in range(0, t.shape[1], LANES)]
    return reduce(functools.reduce(combine, parts), axis=-1, keepdims=True)


def _toeplitz(base, rows):
    w = base.shape[1]
    return pltpu.roll(jnp.broadcast_to(base, (rows, w)), w - (rows - 1), 1, stride=1, stride_axis=0)


def _attn_prompt_kernel(q_ref, ka_ref, kb_ref, kc_ref, va_ref, vb_ref, vc_ref, ext_ref,
                        o_ref, bias_ref, *, scale):
    i = pl.program_id(2)
    tq = q_ref.shape[0]
    heads = q_ref.shape[1] // HEAD_DIM

    @pl.when(i == 0)
    def _():
        shift = CHUNK.bit_length() - 1
        rowc = jnp.right_shift(lax.broadcasted_iota(jnp.int32, (tq, tq), 0), shift)
        colc = jnp.right_shift(lax.broadcasted_iota(jnp.int32, (tq, tq), 1), shift)
        for h in range(heads):
            for m in range(3):
                off = (2 - m) * tq
                t = _toeplitz(ext_ref[h, :, off:off + 2 * tq], tq)[:, :tq]
                if m == 0:
                    t = jnp.where(colc <= rowc, t, NEG_INF)
                if m == 2:
                    t = jnp.where(rowc <= colc, t, NEG_INF)
                bias_ref[h, m] = t * LOG2E

    dn = (((1,), (1,)), ((), ()))
    for h in range(heads):
        sl = slice(h * HEAD_DIM, (h + 1) * HEAD_DIM)
        q = q_ref[:, sl].astype(BF16)
        scores = []
        for m, k_ref in ((2, ka_ref), (1, kb_ref), (0, kc_ref)):
            s = lax.dot_general(q, k_ref[:, sl].astype(BF16), dn, preferred_element_type=F32)
            s = s * (scale * LOG2E) + bias_ref[h, m]
            if m > 0:
                s = s + jnp.where(i >= m, 0.0, NEG_INF)
            scores.append(s)
        mx = _lane_reduce(scores, jnp.maximum, jnp.max)
        probs = [jnp.exp2(s - mx) for s in scores]
        denom = _lane_reduce(probs, jnp.add, jnp.sum)
        out = functools.reduce(jnp.add, [
            jnp.dot(p.astype(BF16), v_ref[:, sl].astype(BF16), preferred_element_type=F32)
            for p, v_ref in zip(probs, (va_ref, vb_ref, vc_ref))])
        o_ref[:, sl] = (out / denom).astype(BF16)


def _attn_prompt(z, rel_bias, n_batch, seq, n_heads):
    tq = 256
    hb = 4
    assert tq == (N_PREV_CHUNKS * CHUNK) // 2 and seq % tq == 0 and n_heads % hb == 0
    nq = seq // tq
    ng = n_heads // hb
    clip = (rel_bias.shape[1] - 1) // 2
    ext = _bias_ramp(rel_bias, 3 * tq - 1, 4 * tq)
    assert 3 * tq - 1 >= clip

    def qmap(b, g, i):
        return (b * nq + i, g)

    def kvmap(back, col0):
        return lambda b, g, i: (b * nq + jnp.maximum(i - back, 0), col0 + g)

    blk = pl.BlockSpec((tq, hb * HEAD_DIM), qmap)
    kvspec = lambda back, col0: pl.BlockSpec((tq, hb * HEAD_DIM), kvmap(back, col0))
    return pl.pallas_call(
        functools.partial(_attn_prompt_kernel, scale=HEAD_DIM ** -0.5),
        out_shape=jax.ShapeDtypeStruct((n_batch * seq, n_heads * HEAD_DIM), BF16),
        grid=(n_batch, ng, nq),
        in_specs=[
            blk,
            kvspec(2, ng), kvspec(1, ng), kvspec(0, ng),
            kvspec(2, 2 * ng), kvspec(1, 2 * ng), kvspec(0, 2 * ng),
            pl.BlockSpec((hb, 1, 4 * tq), lambda b, g, i: (g, 0, 0)),
        ],
        out_specs=blk,
        scratch_shapes=[pltpu.VMEM((hb, 3, tq, tq), F32)],
        compiler_params=_params(("arbitrary", "arbitrary", "arbitrary"), 32),
        name="attn_prompt",
    )(z, z, z, z, z, z, z, ext)


def _attn_sample_kernel(q_ref, kn_ref, vn_ref, ck_ref, cv_ref, ext_ref, o_ref, *, scale, n_heads):
    t = q_ref.shape[0]
    r = ck_ref.shape[1] // n_heads
    dn = (((1,), (1,)), ((), ()))
    for h in range(n_heads):
        sl = slice(h * HEAD_DIM, (h + 1) * HEAD_DIM)
        rows_h = pl.ds(h, r, stride=n_heads)
        bias = _toeplitz(ext_ref[h], t)
        q = q_ref[:, sl].astype(BF16)
        s1 = lax.dot_general(q, ck_ref[0, rows_h, :].astype(BF16), dn, preferred_element_type=F32)
        s1 = s1 * scale + bias[:, :r]
        s2 = lax.dot_general(q, kn_ref[:, sl].astype(BF16), dn, preferred_element_type=F32)
        s2 = s2 * scale + bias[:, r:r + t]
        mx = jnp.maximum(jnp.max(s1, axis=-1, keepdims=True), jnp.max(s2, axis=-1, keepdims=True))
        p1 = jnp.exp(s1 - mx)
        p2 = jnp.exp(s2 - mx)
        denom = jnp.sum(p1, axis=-1, keepdims=True) + jnp.sum(p2, axis=-1, keepdims=True)
        out = (jnp.dot(p1.astype(BF16), cv_ref[0, rows_h, :].astype(BF16), preferred_element_type=F32)
               + jnp.dot(p2.astype(BF16), vn_ref[:, sl].astype(BF16), preferred_element_type=F32))
        o_ref[:, sl] = (out / denom).astype(BF16)


def _attn_sample(z, cache_k, cache_v, rel_bias, row0, n_batch, t, n_heads):
    r = cache_k.shape[1]
    width = n_heads * HEAD_DIM
    ck = cache_k.reshape(n_batch, r * n_heads, HEAD_DIM)
    cv = cache_v.reshape(n_batch, r * n_heads, HEAD_DIM)
    top = r + t - 1
    clip = (rel_bias.shape[1] - 1) // 2
    ramp = -(-max(r + 2 * t, top + clip + 1) // LANES) * LANES
    ext = _bias_ramp(rel_bias, top, ramp)
    rb0 = row0 // t
    return pl.pallas_call(
        functools.partial(_attn_sample_kernel, scale=HEAD_DIM ** -0.5, n_heads=n_heads),
        out_shape=jax.ShapeDtypeStruct((n_batch * t, width), BF16),
        grid=(n_batch,),
        in_specs=[
            pl.BlockSpec((t, width), lambda b: (rb0 + b, 0)),
            pl.BlockSpec((t, width), lambda b: (rb0 + b, 1)),
            pl.BlockSpec((t, width), lambda b: (rb0 + b, 2)),
            pl.BlockSpec((1, r * n_heads, HEAD_DIM), lambda b: (b, 0, 0)),
            pl.BlockSpec((1, r * n_heads, HEAD_DIM), lambda b: (b, 0, 0)),
            pl.BlockSpec((n_heads, 1, ramp), lambda b: (0, 0, 0)),
        ],
        out_specs=pl.BlockSpec((t, width), lambda b: (b, 0)),
        compiler_params=_params(("arbitrary",), 40),
        name="attn_sample",
    )(z, z, z, ck, cv, ext)


def _lru_tile(x, prev, h_in, gate_in, cw, cb, wa, ba, wx, bx, lam):
    t = x.shape[0]
    p = prev.shape[0]
    k = cw.shape[0]
    xcat = jnp.concatenate([prev, x], axis=0)
    xc = cb
    for i in range(k):
        lo = p - (k - 1) + i
        xc = xc + xcat[lo:lo + t] * cw[i:i + 1]
    xcb = xc.astype(BF16)
    rg = _sigmoid(jnp.dot(xcb, wa.astype(BF16), preferred_element_type=F32) + ba)
    ig = _sigmoid(jnp.dot(xcb, wx.astype(BF16), preferred_element_type=F32) + bx)
    neg = -lam
    softplus = jnp.maximum(neg, 0.0) + jnp.log1p(jnp.exp(-jnp.abs(neg)))
    log_a = (-LRU_C) * rg * softplus
    a = jnp.exp(log_a)
    var = -jnp.tanh(log_a) * (a * a + 1.0)
    std = jnp.where(var == 0.0, 0.0, var * lax.rsqrt(var))
    b = std * (ig * xc)
    h = _linear_scan(a, b, h_in)
    return h * _gelu_tanh(gate_in), h


def _doubling_scan(a, b, axis):
    n = a.shape[axis]
    idx = lax.broadcasted_iota(jnp.int32, a.shape, axis)
    step = 1
    while step < n:
        keep = idx >= step
        a_prev = jnp.where(keep, pltpu.roll(a, step, axis), 1.0)
        b_prev = jnp.where(keep, pltpu.roll(b, step, axis), 0.0)
        b = a * b_prev + b
        a = a * a_prev
        step *= 2
    return a, b


def _linear_scan(a, b, h_in):
    t, c = a.shape
    groups = t // SUBLANES
    a3, b3 = _doubling_scan(a.reshape(groups, SUBLANES, c), b.reshape(groups, SUBLANES, c), 1)
    ae = jnp.broadcast_to(a3[:, SUBLANES - 1:, :], a3.shape).reshape(t, c)
    be = jnp.broadcast_to(b3[:, SUBLANES - 1:, :], b3.shape).reshape(t, c)
    step = SUBLANES
    while step < t:
        be = jnp.concatenate([be[:step], ae[step:] * be[:-step] + be[step:]], axis=0)
        ae = jnp.concatenate([ae[:step], ae[step:] * ae[:-step]], axis=0)
        step *= 2
    h_end = ae * h_in + be
    h_start = jnp.concatenate([jnp.broadcast_to(h_in, (SUBLANES, c)), h_end[:t - SUBLANES]], axis=0)
    return a3.reshape(t, c) * h_start + b3.reshape(t, c)


def _lru_prompt_kernel(x_ref, g_ref, cw_ref, cb_ref, wa_ref, ba_ref, wx_ref, bx_ref, lam_ref,
                       o_ref, conv_ref, h_ref, px_ref, ph_ref):
    @pl.when(pl.program_id(2) == 0)
    def _():
        px_ref[...] = jnp.zeros_like(px_ref)
        ph_ref[...] = jnp.zeros_like(ph_ref)

    t = x_ref.shape[0]
    for n in range(x_ref.shape[1] // LRU_BLOCK):
        sl = slice(n * LRU_BLOCK, (n + 1) * LRU_BLOCK)
        x = x_ref[:, sl]
        out, h = _lru_tile(x, px_ref[:, sl], ph_ref[SUBLANES - 1:SUBLANES, sl], g_ref[:, sl],
                           cw_ref[:, sl], cb_ref[:, sl], wa_ref[n], ba_ref[:, sl], wx_ref[n],
                           bx_ref[:, sl], lam_ref[:, sl])
        o_ref[:, sl] = out.astype(BF16)
        px_ref[:, sl] = x[t - SUBLANES:]
        ph_ref[:, sl] = h[t - SUBLANES:]
        conv_ref[0, :, sl] = x[t - SUBLANES:]
        h_ref[0, :, sl] = h[t - SUBLANES:]


def _lru_prompt(z, lw, n_batch, seq, col_x, col_g):
    conv_w, conv_b, w_a, b_a, w_x, b_x, lam = lw
    n_blocks = w_a.shape[0]
    width = n_blocks * LRU_BLOCK
    tt = 256
    nb = 4
    lanes = nb * LRU_BLOCK
    assert seq % tt == 0 and n_blocks % nb == 0 and col_x % lanes == 0 and col_g % lanes == 0
    nt = seq // tt
    cx, cg = col_x // lanes, col_g // lanes
    vec = lambda a: a.reshape(1, width)
    vspec = pl.BlockSpec((1, lanes), lambda b, n, t: (0, n))
    wspec = pl.BlockSpec((nb, LRU_BLOCK, LRU_BLOCK), lambda b, n, t: (n, 0, 0))
    sspec = pl.BlockSpec((1, SUBLANES, lanes), lambda b, n, t: (b, 0, n))
    return pl.pallas_call(
        _lru_prompt_kernel,
        out_shape=(
            jax.ShapeDtypeStruct((n_batch * seq, width), BF16),
            jax.ShapeDtypeStruct((n_batch, SUBLANES, width), F32),
            jax.ShapeDtypeStruct((n_batch, SUBLANES, width), F32),
        ),
        grid=(n_batch, n_blocks // nb, nt),
        in_specs=[
            pl.BlockSpec((tt, lanes), lambda b, n, t: (b * nt + t, cx + n)),
            pl.BlockSpec((tt, lanes), lambda b, n, t: (b * nt + t, cg + n)),
            pl.BlockSpec((conv_w.shape[0], lanes), lambda b, n, t: (0, n)),
            vspec, wspec, vspec, wspec, vspec, vspec,
        ],
        out_specs=(
            pl.BlockSpec((tt, lanes), lambda b, n, t: (b * nt + t, n)),
            sspec, sspec,
        ),
        scratch_shapes=[pltpu.VMEM((SUBLANES, lanes), F32), pltpu.VMEM((SUBLANES, lanes), F32)],
        compiler_params=_params(("arbitrary", "arbitrary", "arbitrary"), 32),
        name="lru_prompt",
    )(z, z, conv_w, vec(conv_b), w_a, vec(b_a), w_x, vec(b_x), vec(lam))


def _lru_sample_kernel(x_ref, g_ref, prev_ref, h0_ref, cw_ref, cb_ref, wa_ref, ba_ref, wx_ref,
                       bx_ref, lam_ref, o_ref, conv_ref, h_ref, *, n_blocks):
    t = x_ref.shape[0]
    for n in range(n_blocks):
        sl = slice(n * LRU_BLOCK, (n + 1) * LRU_BLOCK)
        x = x_ref[:, sl]
        out, h = _lru_tile(x, prev_ref[0, :, sl], h0_ref[0, :, sl], g_ref[:, sl],
                           cw_ref[:, sl], cb_ref[:, sl], wa_ref[n], ba_ref[:, sl], wx_ref[n],
                           bx_ref[:, sl], lam_ref[:, sl])
        o_ref[:, sl] = out.astype(BF16)
        conv_ref[0, :, sl] = x[t - SUBLANES:]
        h_ref[0, :, sl] = h[t - SUBLANES:]


def _lru_sample(z, state_conv, state_h, lw, row0, n_batch, t, col_x, col_g):
    conv_w, conv_b, w_a, b_a, w_x, b_x, lam = lw
    n_blocks = w_a.shape[0]
    width = n_blocks * LRU_BLOCK
    k = conv_w.shape[0]
    assert t >= SUBLANES and k - 1 <= SUBLANES
    prev = jnp.pad(state_conv, ((0, 0), (SUBLANES - (k - 1), 0), (0, 0)))
    h0 = state_h.reshape(n_batch, 1, width)
    rb0 = row0 // t
    cx, cg = col_x // width, col_g // width
    vec = lambda a: a.reshape(1, width)
    vspec = pl.BlockSpec((1, width), lambda b: (0, 0))
    wspec = pl.BlockSpec((n_blocks, LRU_BLOCK, LRU_BLOCK), lambda b: (0, 0, 0))
    sspec = pl.BlockSpec((1, SUBLANES, width), lambda b: (b, 0, 0))
    return pl.pallas_call(
        functools.partial(_lru_sample_kernel, n_blocks=n_blocks),
        out_shape=(
            jax.ShapeDtypeStruct((n_batch * t, width), BF16),
            jax.ShapeDtypeStruct((n_batch, SUBLANES, width), F32),
            jax.ShapeDtypeStruct((n_batch, SUBLANES, width), F32),
        ),
        grid=(n_batch,),
        in_specs=[
            pl.BlockSpec((t, width), lambda b: (rb0 + b, cx)),
            pl.BlockSpec((t, width), lambda b: (rb0 + b, cg)),
            sspec,
            pl.BlockSpec((1, 1, width), lambda b: (b, 0, 0)),
            pl.BlockSpec((k, width), lambda b: (0, 0)),
            vspec, wspec, vspec, wspec, vspec, vspec,
        ],
        out_specs=(pl.BlockSpec((t, width), lambda b: (b, 0)), sspec, sspec),
        compiler_params=_params(("arbitrary",), 32),
        name="lru_sample",
    )(z, z, prev, h0, conv_w, vec(conv_b), w_a, vec(b_a), w_x, vec(b_x), vec(lam))


def _out_proj_kernel(ap_ref, lp_ref, as_ref, ls_ref, w_ref, xp_ref, xs_ref, gp_ref, gs_ref,
                     op_ref, os_ref, wb_ref):
    i = pl.program_id(1)

    @pl.when(i == 0)
    def _():
        wb_ref[...] = w_ref[...].astype(BF16)

    ka = ap_ref.shape[1]

    def tile(a_ref, l_ref, x_ref, g_ref, o_ref):
        acc = (jnp.dot(a_ref[...], wb_ref[:ka, :], preferred_element_type=F32)
               + jnp.dot(l_ref[...], wb_ref[ka:, :], preferred_element_type=F32))
        o_ref[...] = x_ref[...] + _group_scale(acc, g_ref[...])

    _per_row_group(i,
                   lambda: tile(ap_ref, lp_ref, xp_ref, gp_ref, op_ref),
                   lambda: tile(as_ref, ls_ref, xs_ref, gs_ref, os_ref))


def _out_proj(attn_p, lru_p, attn_s, lru_s, w_out, x_p, x_s, table, gate_col, group):
    d, n = w_out.shape
    ka, kl = attn_p.shape[1], lru_p.shape[1]
    assert ka + kl == d
    tn = 512
    rg = _RowGroups(x_p.shape[0], x_s.shape[0])
    gcol = lambda j: gate_col * (n // tn) + j
    sample_table_block = (rg.rows_p // group) // (rg.rows_s // group)
    return pl.pallas_call(
        _out_proj_kernel,
        out_shape=(jax.ShapeDtypeStruct((rg.rows_p, n), F32), jax.ShapeDtypeStruct((rg.rows_s, n), F32)),
        grid=(n // tn, rg.steps),
        in_specs=[
            rg.prompt_spec(ka, lambda j: 0),
            rg.prompt_spec(kl, lambda j: 0),
            rg.sample_spec(ka, lambda j: 0, resident=True),
            rg.sample_spec(kl, lambda j: 0, resident=True),
            pl.BlockSpec((d, tn), lambda j, i: (0, j)),
            rg.prompt_spec(tn, lambda j: j),
            rg.sample_spec(tn, lambda j: j),
            rg.prompt_spec(tn, gcol, rows_per_row=group),
            rg.sample_spec(tn, gcol, rows_per_row=group, row_block=sample_table_block),
        ],
        out_specs=(rg.prompt_spec(tn, lambda j: j, output=True), rg.sample_spec(tn, lambda j: j)),
        scratch_shapes=[pltpu.VMEM((d, tn), BF16)],
        compiler_params=_params(("arbitrary", "arbitrary"), 58),
        name="out_proj",
    )(attn_p, lru_p, attn_s, lru_s, w_out, x_p, x_s, table, table)


def _ffn_up_kernel(xp_ref, xs_ref, wg_ref, wu_ref, op_ref, os_ref, wgb_ref, wub_ref, *,
                   n_real):
    j = pl.program_id(0)
    i = pl.program_id(1)

    @pl.when(i == 0)
    def _():
        wgb_ref[...] = wg_ref[...].astype(BF16)
        wub_ref[...] = wu_ref[...].astype(BF16)

    def tile(x_ref, o_ref):
        @pl.when(j < n_real)
        def _():
            x = x_ref[...]
            g = jnp.dot(x, wgb_ref[...], preferred_element_type=F32)
            u = jnp.dot(x, wub_ref[...], preferred_element_type=F32)
            o_ref[...] = (g * _sigmoid(g) * u).astype(BF16)

        @pl.when(j >= n_real)
        def _():
            o_ref[...] = jnp.zeros_like(o_ref)

    _per_row_group(i, lambda: tile(xp_ref, op_ref), lambda: tile(xs_ref, os_ref))


def _ffn_up(u_p, u_s, w_gate, w_up, ff_pad):
    d, ff = w_gate.shape
    tn = 256
    n_real = ff // tn
    assert ff % tn == 0 and ff_pad % tn == 0
    rg = _RowGroups(u_p.shape[0], u_s.shape[0])
    wspec = pl.BlockSpec((d, tn), lambda j, i: (0, jnp.minimum(j, n_real - 1)))
    return pl.pallas_call(
        functools.partial(_ffn_up_kernel, n_real=n_real),
        out_shape=(jax.ShapeDtypeStruct((rg.rows_p, ff_pad), BF16),
                   jax.ShapeDtypeStruct((rg.rows_s, ff_pad), BF16)),
        grid=(ff_pad // tn, rg.steps),
        in_specs=[rg.prompt_spec(d, lambda j: 0), rg.sample_spec(d, lambda j: 0, resident=True),
                  wspec, wspec],
        out_specs=(rg.prompt_spec(tn, lambda j: j, output=True), rg.sample_spec(tn, lambda j: j)),
        scratch_shapes=[pltpu.VMEM((d, tn), BF16), pltpu.VMEM((d, tn), BF16)],
        compiler_params=_params(("arbitrary", "arbitrary"), 52),
        name="ffn_up",
    )(u_p, u_s, w_gate, w_up)


def _ffn_down_kernel(h_ref, w_ref, x_ref, g_ref, o_ref, *, k_total):
    k = pl.program_id(2)
    tk = w_ref.shape[0]
    row = lax.broadcasted_iota(jnp.int32, w_ref.shape, 0)
    w = jnp.where(row < k_total - k * tk, w_ref[...], 0.0).astype(BF16)

    @pl.when(k == 0)
    def _():
        o_ref[...] = jnp.zeros_like(o_ref)

    tm = o_ref.shape[0]
    rows = min(tm, 512)
    for r0 in range(0, tm, rows):
        o_ref[r0:r0 + rows, :] += jnp.dot(h_ref[r0:r0 + rows, :], w, preferred_element_type=F32)

    @pl.when(k == pl.num_programs(2) - 1)
    def _():
        o_ref[...] = x_ref[...] + _group_scale(o_ref[...], g_ref[...])


def _ffn_down(h, w_down, x1, table, gate_col, group, group0, tm, tn):
    rows, ff_pad = h.shape
    ff, n = w_down.shape
    tk = 1024
    tn = min(tn, n)
    gt = tm // group
    assert rows % tm == 0 and ff_pad % tk == 0 and n % tn == 0 and group0 % gt == 0
    g0 = group0 // gt
    gcol0 = gate_col * (n // tn)
    return pl.pallas_call(
        functools.partial(_ffn_down_kernel, k_total=ff),
        out_shape=jax.ShapeDtypeStruct((rows, n), F32),
        grid=(rows // tm, n // tn, ff_pad // tk),
        in_specs=[
            pl.BlockSpec((tm, tk), lambda i, j, k: (i, k)),
            pl.BlockSpec((tk, tn), lambda i, j, k: (k, j)),
            pl.BlockSpec((tm, tn), lambda i, j, k: (i, j)),
            pl.BlockSpec((gt, tn), lambda i, j, k: (g0 + i, gcol0 + j)),
        ],
        out_specs=pl.BlockSpec((tm, tn), lambda i, j, k: (i, j)),
        compiler_params=_params(("arbitrary", "arbitrary", "arbitrary"), 58),
        name="ffn_down",
    )(h, w_down, x1, table)


def _layer(x_p, x_s, c_prompt, c_sample, cache_k, cache_v, state_conv, state_h, lw, dims):
    (norm_mix_g, norm_ffn_g, w_mod, b_mod, w_in, q_norm_g, k_norm_g, rel_bias,
     conv_w, conv_b, w_rg_a, b_rg_a, w_rg_x, b_rg_x, lru_lambda, w_out,
     w_ffn_gate, w_ffn_up, w_ffn_down) = lw
    n_prompt, seq, n_sample, t_s = dims
    group = t_s
    n_heads = rel_bias.shape[0]
    attn_w = n_heads * HEAD_DIM
    lru_w = w_rg_a.shape[0] * LRU_BLOCK
    rows_p = n_prompt * seq
    rows_s = n_sample * t_s
    groups_p = rows_p // group

    table = _mod_table(c_prompt, c_sample, w_mod, b_mod, seq // group)
    u_p = _norm_mod(x_p, norm_mix_g, table, 0, 1, group, 0)
    u_s = _norm_mod(x_s, norm_mix_g, table, 0, 1, group, groups_p)
    qk_gain = jnp.concatenate([jnp.tile(q_norm_g, n_heads), jnp.tile(k_norm_g, n_heads)]).reshape(1, -1)
    z_p, z_s = _in_proj(u_p, u_s, w_in, qk_gain, attn_w)

    attn_p = _attn_prompt(z_p, rel_bias, n_prompt, seq, n_heads)
    attn_s = _attn_sample(z_s, cache_k, cache_v, rel_bias, 0, n_sample, t_s, n_heads)
    lru_params = (conv_w, conv_b, w_rg_a, b_rg_a, w_rg_x, b_rg_x, lru_lambda)
    col_x, col_g = 3 * attn_w, 3 * attn_w + lru_w
    lru_p, conv_p, h_p = _lru_prompt(z_p, lru_params, n_prompt, seq, col_x, col_g)
    lru_s, conv_s, h_s = _lru_sample(z_s, state_conv, state_h, lru_params, 0, n_sample, t_s,
                                     col_x, col_g)

    x1_p, x1_s = _out_proj(attn_p, lru_p, attn_s, lru_s, w_out, x_p, x_s, table, 2, group)
    u2_p = _norm_mod(x1_p, norm_ffn_g, table, 3, 4, group, 0)
    u2_s = _norm_mod(x1_s, norm_ffn_g, table, 3, 4, group, groups_p)
    ff = w_ffn_gate.shape[1]
    ff_pad = -(-ff // 1024) * 1024
    hid_p, hid_s = _ffn_up(u2_p, u2_s, w_ffn_gate, w_ffn_up, ff_pad)
    y_p = _ffn_down(hid_p, w_ffn_down, x1_p, table, 5, group, 0, 2048, 1024)
    y_s = _ffn_down(hid_s, w_ffn_down, x1_s, table, 5, group, groups_p, rows_s, 2048)

    keep = min(N_PREV_CHUNKS * CHUNK, seq)
    kv_p = jnp.stack([z_p[(b + 1) * seq - keep:(b + 1) * seq, attn_w:3 * attn_w] for b in range(n_prompt)])
    kv_p = kv_p.reshape(n_prompt, keep, 2, n_heads, HEAD_DIM)
    kv_s = z_s[:, attn_w:3 * attn_w].reshape(n_sample, t_s, 2, n_heads, HEAD_DIM)
    k_conv = conv_w.shape[0] - 1
    state = dict(
        k_p=kv_p[:, :, 0], v_p=kv_p[:, :, 1], conv_p=conv_p[:, SUBLANES - k_conv:], h_p=h_p[:, SUBLANES - 1],
        k_s=kv_s[:, :, 0], v_s=kv_s[:, :, 1], conv_s=conv_s[:, SUBLANES - k_conv:], h_s=h_s[:, SUBLANES - 1])
    return y_p, y_s, state


def kernel(x_prompt, x_sample, cache_k, cache_v, state_conv, state_h, c_prompt, c_sample, norm_mix_g, norm_ffn_g, w_mod, b_mod, w_in, q_norm_g, k_norm_g, rel_bias, conv_w, conv_b, w_rg_a, b_rg_a, w_rg_x, b_rg_x, lru_lambda, w_out, w_ffn_gate, w_ffn_up, w_ffn_down):
    n_prompt, seq, d = x_prompt.shape
    n_sample, t_s, _ = x_sample.shape
    depth = w_in.shape[0]
    dims = (n_prompt, seq, n_sample, t_s)
    yp = x_prompt.reshape(n_prompt * seq, d)
    ys = x_sample.reshape(n_sample * t_s, d)
    states = []
    for l in range(depth):
        lw = (norm_mix_g[l], norm_ffn_g[l], w_mod[l], b_mod[l], w_in[l], q_norm_g[l], k_norm_g[l],
              rel_bias[l], conv_w[l], conv_b[l], w_rg_a[l], b_rg_a[l], w_rg_x[l], b_rg_x[l],
              lru_lambda[l], w_out[l], w_ffn_gate[l], w_ffn_up[l], w_ffn_down[l])
        yp, ys, st = _layer(yp, ys, c_prompt, c_sample, cache_k[l], cache_v[l], state_conv[l],
                            state_h[l], lw, dims)
        states.append(st)
    stack = lambda name: jnp.stack([s[name] for s in states])
    return (yp.reshape(n_prompt, seq, d), ys.reshape(n_sample, t_s, d),
            stack("k_p"), stack("v_p"), stack("conv_p"), stack("h_p"),
            stack("k_s"), stack("v_s"), stack("conv_s"), stack("h_s"))
```

```python
import functools
import math

import jax
import jax.numpy as jnp
from jax import lax
from jax.experimental import pallas as pl
from jax.experimental.pallas import tpu as pltpu

F32 = jnp.float32
BF16 = jnp.bfloat16

CHUNK = 64
N_PREV_CHUNKS = 8
HEAD_DIM = 128
LRU_BLOCK = 128
LRU_C = 8.0
NEG_INF = -1e30
EPS = 1e-6
LOG2E = math.log2(math.e)

LANES = 128
SUBLANES = 8
MIB = 1024 * 1024


def _params(semantics, vmem_mib):
    return pltpu.CompilerParams(dimension_semantics=semantics, vmem_limit_bytes=vmem_mib * MIB)


def _sigmoid(x):
    return 0.5 * jnp.tanh(0.5 * x) + 0.5


def _gelu_tanh(x):
    c = math.sqrt(2.0 / math.pi)
    return x * (0.5 * (1.0 + jnp.tanh(c * (x + 0.044715 * (x * x * x)))))


def _group_scale(y, g):
    rows, n = y.shape
    groups = g.shape[0]
    return (y.reshape(groups, rows // groups, n) * g[:, None, :]).reshape(rows, n)


PROMPT_TILE_ROWS = 1024


class _RowGroups:
    def __init__(self, rows_p, rows_s, tile=PROMPT_TILE_ROWS):
        assert rows_p % tile == 0
        self.rows_p, self.rows_s, self.tile = rows_p, rows_s, tile
        self.np_tiles = rows_p // tile
        self.steps = self.np_tiles + 1

    def prompt_spec(self, cols, col_block, rows_per_row=1, output=False):
        first = 0 if output else self.np_tiles - 1
        return pl.BlockSpec((self.tile // rows_per_row, cols),
                            lambda j, i: (jnp.where(i == 0, first, i - 1), col_block(j)))

    def sample_spec(self, cols, col_block, rows_per_row=1, row_block=0, resident=False):
        mode = dict(pipeline_mode=pl.Buffered(1)) if resident else {}
        return pl.BlockSpec((self.rows_s // rows_per_row, cols),
                            lambda j, i: (row_block, col_block(j)), **mode)


def _per_row_group(i, prompt_fn, sample_fn):
    pl.when(i == 0)(sample_fn)
    pl.when(i > 0)(prompt_fn)


def _mod_kernel(c_ref, w_ref, b_ref, o_ref, *, n_sample, n_prompt, groups_per_prompt):
    c = c_ref[...]
    s = (c * _sigmoid(c)).astype(BF16)
    m = jnp.dot(s, w_ref[...].astype(BF16), preferred_element_type=F32) + b_ref[...]
    tn = m.shape[1]
    for b in range(n_prompt):
        row = m[n_sample + b:n_sample + b + 1, :]
        o_ref[b * groups_per_prompt:(b + 1) * groups_per_prompt, :] = jnp.broadcast_to(
            row, (groups_per_prompt, tn))
    o_ref[n_prompt * groups_per_prompt:n_prompt * groups_per_prompt + n_sample, :] = m[:n_sample, :]


def _mod_inputs(c_prompt, c_sample):
    rows = c_sample.shape[0] + c_prompt.shape[0]
    rows_pad = -(-rows // SUBLANES) * SUBLANES
    return jnp.concatenate(
        [c_sample, c_prompt, jnp.zeros((rows_pad - rows, c_prompt.shape[1]), c_prompt.dtype)], axis=0)


def _mod_table(c_all, w_mod, b_mod, n_cols, n_sample, n_prompt, groups_per_prompt):
    rows_pad, d = c_all.shape
    n_groups = n_prompt * groups_per_prompt + n_sample
    tn = 512
    return pl.pallas_call(
        functools.partial(_mod_kernel, n_sample=n_sample, n_prompt=n_prompt,
                          groups_per_prompt=groups_per_prompt),
        out_shape=jax.ShapeDtypeStruct((n_groups, n_cols), F32),
        grid=(n_cols // tn,),
        in_specs=[
            pl.BlockSpec((rows_pad, d), lambda j: (0, 0)),
            pl.BlockSpec((d, tn), lambda j: (0, j)),
            pl.BlockSpec((1, tn), lambda j: (0, j)),
        ],
        out_specs=pl.BlockSpec((n_groups, tn), lambda j: (0, j)),
        compiler_params=_params(("arbitrary",), 40),
        name="mod_table",
    )(c_all, w_mod, b_mod)


def _norm_mod_kernel(x_ref, gain_ref, shift_ref, scale_ref, o_ref):
    x = x_ref[...]
    ms = jnp.mean(x * x, axis=-1, keepdims=True)
    y = x * lax.rsqrt(ms + EPS) * gain_ref[...]
    rows, d = y.shape
    groups = scale_ref.shape[0]
    y3 = y.reshape(groups, rows // groups, d)
    u = y3 * (1.0 + scale_ref[...][:, None, :]) + shift_ref[...][:, None, :]
    o_ref[...] = u.reshape(rows, d).astype(BF16)


def _norm_mod(x, gain, table, shift_col, scale_col, group, group0):
    rows, d = x.shape
    tm = 512
    gt = tm // group
    assert rows % tm == 0 and group0 % gt == 0
    g0 = group0 // gt
    return pl.pallas_call(
        _norm_mod_kernel,
        out_shape=jax.ShapeDtypeStruct((rows, d), BF16),
        grid=(rows // tm,),
        in_specs=[
            pl.BlockSpec((tm, d), lambda i: (i, 0)),
            pl.BlockSpec((1, d), lambda i: (0, 0)),
            pl.BlockSpec((gt, d), lambda i: (g0 + i, shift_col)),
            pl.BlockSpec((gt, d), lambda i: (g0 + i, scale_col)),
        ],
        out_specs=pl.BlockSpec((tm, d), lambda i: (i, 0)),
        compiler_params=_params(("arbitrary",), 40),
        name="norm_mod",
    )(x, gain.reshape(1, d), table, table)


def _in_proj_kernel(xp_ref, xs_ref, w_ref, gain_ref, op_ref, os_ref, wb_ref, *,
                    n_norm_tiles):
    j = pl.program_id(0)
    i = pl.program_id(1)

    @pl.when(i == 0)
    def _():
        wb_ref[...] = w_ref[...].astype(BF16)

    def tile(x_ref, o_ref):
        acc = jnp.dot(x_ref[...], wb_ref[...], preferred_element_type=F32)
        tn = acc.shape[1]

        @pl.when(j < n_norm_tiles)
        def _():
            for g in range(tn // HEAD_DIM):
                sl = slice(g * HEAD_DIM, (g + 1) * HEAD_DIM)
                zg = acc[:, sl]
                ms = jnp.mean(zg * zg, axis=-1, keepdims=True)
                o_ref[:, sl] = zg * lax.rsqrt(ms + EPS) * gain_ref[:, sl]

        @pl.when(j >= n_norm_tiles)
        def _():
            o_ref[...] = acc

    _per_row_group(i, lambda: tile(xp_ref, op_ref), lambda: tile(xs_ref, os_ref))


def _in_proj(u_p, u_s, w_in, qk_gain, attn_width):
    d, n = w_in.shape
    tn = 512
    n_norm_tiles = 2 * attn_width // tn
    rg = _RowGroups(u_p.shape[0], u_s.shape[0])
    return pl.pallas_call(
        functools.partial(_in_proj_kernel, n_norm_tiles=n_norm_tiles),
        out_shape=(jax.ShapeDtypeStruct((rg.rows_p, n), F32), jax.ShapeDtypeStruct((rg.rows_s, n), F32)),
        grid=(n // tn, rg.steps),
        in_specs=[
            rg.prompt_spec(d, lambda j: 0),
            rg.sample_spec(d, lambda j: 0, resident=True),
            pl.BlockSpec((d, tn), lambda j, i: (0, j)),
            pl.BlockSpec((1, tn), lambda j, i: (0, jnp.minimum(j, n_norm_tiles - 1))),
        ],
        out_specs=(rg.prompt_spec(tn, lambda j: j, output=True), rg.sample_spec(tn, lambda j: j)),
        scratch_shapes=[pltpu.VMEM((d, tn), BF16)],
        compiler_params=_params(("arbitrary", "arbitrary"), 56),
        name="in_proj",
    )(u_p, u_s, w_in, qk_gain)


def _bias_ramp(rel_bias, top, length):
    n_heads, n_rel = rel_bias.shape
    clip = (n_rel - 1) // 2
    lead = top - clip
    tail = length - lead - n_rel
    assert lead >= 0 and tail >= 0
    rev = rel_bias[:, ::-1]
    ext = jnp.concatenate([
        jnp.broadcast_to(rel_bias[:, n_rel - 1:], (n_heads, lead)),
        rev,
        jnp.broadcast_to(rel_bias[:, :1], (n_heads, tail)),
    ], axis=1)
    return ext.reshape(n_heads, 1, length)


def _lane_reduce(tiles, combine, reduce):
    parts = [t[:, c:c + LANES] for t in tiles for c in range(0, t.shape[1], LANES)]
    return reduce(functools.reduce(combine, parts), axis=-1, keepdims=True)


def _toeplitz(base, rows):
    w = base.shape[1]
    return pltpu.roll(jnp.broadcast_to(base, (rows, w)), w - (rows - 1), 1, stride=1, stride_axis=0)


def _attn_prompt_bias(ext_ref, bias_ref, tq):
    shift = CHUNK.bit_length() - 1
    rowc = jnp.right_shift(lax.broadcasted_iota(jnp.int32, (tq, tq), 0), shift)
    colc = jnp.right_shift(lax.broadcasted_iota(jnp.int32, (tq, tq), 1), shift)
    for h in range(bias_ref.shape[0]):
        for m in range(3):
            off = (2 - m) * tq
            t = _toeplitz(ext_ref[h, :, off:off + 2 * tq], tq)[:, :tq]
            if m == 0:
                t = jnp.where(colc <= rowc, t, NEG_INF)
            if m == 2:
                t = jnp.where(rowc <= colc, t, NEG_INF)
            bias_ref[h, m] = t * LOG2E


def _attn_prompt_head(h, i, q_ref, k_refs, v_refs, bias_ref, o_ref, scale):
    dn = (((1,), (1,)), ((), ()))
    sl = slice(h * HEAD_DIM, (h + 1) * HEAD_DIM)
    q = q_ref[:, sl].astype(BF16)
    scores = []
    for m, k_ref in zip((2, 1, 0), k_refs):
        s = lax.dot_general(q, k_ref[:, sl].astype(BF16), dn, preferred_element_type=F32)
        s = s * (scale * LOG2E) + bias_ref[h, m]
        if m > 0:
            s = s + jnp.where(i >= m, 0.0, NEG_INF)
        scores.append(s)
    mx = _lane_reduce(scores, jnp.maximum, jnp.max)
    probs = [jnp.exp2(s - mx) for s in scores]
    denom = _lane_reduce(probs, jnp.add, jnp.sum)
    out = functools.reduce(jnp.add, [
        jnp.dot(p.astype(BF16), v_ref[:, sl].astype(BF16), preferred_element_type=F32)
        for p, v_ref in zip(probs, v_refs)])
    o_ref[:, sl] = (out / denom).astype(BF16)


def _attn_sample_kernel(q_ref, kn_ref, vn_ref, ck_ref, cv_ref, ext_ref, o_ref, *, scale, n_heads):
    t = q_ref.shape[0]
    r = ck_ref.shape[1] // n_heads
    dn = (((1,), (1,)), ((), ()))
    for h in range(n_heads):
        sl = slice(h * HEAD_DIM, (h + 1) * HEAD_DIM)
        rows_h = pl.ds(h, r, stride=n_heads)
        bias = _toeplitz(ext_ref[h], t)
        q = q_ref[:, sl].astype(BF16)
        s1 = lax.dot_general(q, ck_ref[0, rows_h, :].astype(BF16), dn, preferred_element_type=F32)
        s1 = s1 * scale + bias[:, :r]
        s2 = lax.dot_general(q, kn_ref[:, sl].astype(BF16), dn, preferred_element_type=F32)
        s2 = s2 * scale + bias[:, r:r + t]
        mx = jnp.maximum(jnp.max(s1, axis=-1, keepdims=True), jnp.max(s2, axis=-1, keepdims=True))
        p1 = jnp.exp(s1 - mx)
        p2 = jnp.exp(s2 - mx)
        denom = jnp.sum(p1, axis=-1, keepdims=True) + jnp.sum(p2, axis=-1, keepdims=True)
        out = (jnp.dot(p1.astype(BF16), cv_ref[0, rows_h, :].astype(BF16), preferred_element_type=F32)
               + jnp.dot(p2.astype(BF16), vn_ref[:, sl].astype(BF16), preferred_element_type=F32))
        o_ref[:, sl] = (out / denom).astype(BF16)


def _attn_sample(z, cache_k, cache_v, rel_bias, row0, n_batch, t, n_heads):
    r = cache_k.shape[1]
    width = n_heads * HEAD_DIM
    ck = cache_k.reshape(n_batch, r * n_heads, HEAD_DIM)
    cv = cache_v.reshape(n_batch, r * n_heads, HEAD_DIM)
    top = r + t - 1
    clip = (rel_bias.shape[1] - 1) // 2
    ramp = -(-max(r + 2 * t, top + clip + 1) // LANES) * LANES
    ext = _bias_ramp(rel_bias, top, ramp)
    rb0 = row0 // t
    return pl.pallas_call(
        functools.partial(_attn_sample_kernel, scale=HEAD_DIM ** -0.5, n_heads=n_heads),
        out_shape=jax.ShapeDtypeStruct((n_batch * t, width), BF16),
        grid=(n_batch,),
        in_specs=[
            pl.BlockSpec((t, width), lambda b: (rb0 + b, 0)),
            pl.BlockSpec((t, width), lambda b: (rb0 + b, 1)),
            pl.BlockSpec((t, width), lambda b: (rb0 + b, 2)),
            pl.BlockSpec((1, r * n_heads, HEAD_DIM), lambda b: (b, 0, 0)),
            pl.BlockSpec((1, r * n_heads, HEAD_DIM), lambda b: (b, 0, 0)),
            pl.BlockSpec((n_heads, 1, ramp), lambda b: (0, 0, 0)),
        ],
        out_specs=pl.BlockSpec((t, width), lambda b: (b, 0)),
        compiler_params=_params(("arbitrary",), 40),
        name="attn_sample",
    )(z, z, z, ck, cv, ext)


def _lru_tile(x, prev, h_in, gate_in, cw, cb, wa, ba, wx, bx, lam, scan_scratch=None):
    t = x.shape[0]
    p = prev.shape[0]
    k = cw.shape[0]
    xcat = jnp.concatenate([prev, x], axis=0)
    xc = cb
    for i in range(k):
        lo = p - (k - 1) + i
        xc = xc + xcat[lo:lo + t] * cw[i:i + 1]
    xcb = xc.astype(BF16)
    rg = _sigmoid(jnp.dot(xcb, wa.astype(BF16), preferred_element_type=F32) + ba)
    ig = _sigmoid(jnp.dot(xcb, wx.astype(BF16), preferred_element_type=F32) + bx)
    neg = -lam
    softplus = jnp.maximum(neg, 0.0) + jnp.log1p(jnp.exp(-jnp.abs(neg)))
    log_a = (-LRU_C) * rg * softplus
    a = jnp.exp(log_a)
    var = -jnp.tanh(log_a) * (a * a + 1.0)
    std = jnp.where(var == 0.0, 0.0, var * lax.rsqrt(var))
    b = std * (ig * xc)
    h = _linear_scan(a, b, h_in, scan_scratch)
    return h * _gelu_tanh(gate_in), h


def _doubling_scan(a, b, axis):
    n = a.shape[axis]
    idx = lax.broadcasted_iota(jnp.int32, a.shape, axis)
    step = 1
    while step < n:
        keep = idx >= step
        a_prev = jnp.where(keep, pltpu.roll(a, step, axis), 1.0)
        b_prev = jnp.where(keep, pltpu.roll(b, step, axis), 0.0)
        b = a * b_prev + b
        a = a * a_prev
        step *= 2
    return a, b


def _linear_scan(a, b, h_in, scratch=None):
    t, c = a.shape
    groups = t // SUBLANES
    a3, b3 = _doubling_scan(a.reshape(groups, SUBLANES, c), b.reshape(groups, SUBLANES, c), 1)
    if scratch is not None:
        ab_ref, hs_ref = scratch
        a2, b2 = a3.reshape(t, c), b3.reshape(t, c)
        ab_ref[0] = a2
        ab_ref[1] = b2
        ends = pl.ds(SUBLANES - 1, groups, stride=SUBLANES)
        a_cum, b_cum = _doubling_scan(ab_ref[0, ends, :], ab_ref[1, ends, :], 0)
        h_end = a_cum * h_in + b_cum
        first = lax.broadcasted_iota(jnp.int32, h_end.shape, 0) == 0
        hs_ref[...] = jnp.where(first, h_in, pltpu.roll(h_end, 1, 0))
        h_start = jnp.concatenate(
            [jnp.broadcast_to(hs_ref[g:g + 1, :], (SUBLANES, c)) for g in range(groups)], axis=0)
        return a2 * h_start + b2
    ae = jnp.broadcast_to(a3[:, SUBLANES - 1:, :], a3.shape).reshape(t, c)
    be = jnp.broadcast_to(b3[:, SUBLANES - 1:, :], b3.shape).reshape(t, c)
    step = SUBLANES
    while step < t:
        be = jnp.concatenate([be[:step], ae[step:] * be[:-step] + be[step:]], axis=0)
        ae = jnp.concatenate([ae[:step], ae[step:] * ae[:-step]], axis=0)
        step *= 2
    h_end = ae * h_in + be
    h_start = jnp.concatenate([jnp.broadcast_to(h_in, (SUBLANES, c)), h_end[:t - SUBLANES]], axis=0)
    return a3.reshape(t, c) * h_start + b3.reshape(t, c)


def _lru_prompt_block(n, x_ref, g_ref, cw_ref, cb_ref, wa_ref, ba_ref, wx_ref, bx_ref, lam_ref,
                      o_ref, conv_ref, h_ref, px_ref, ph_ref, ab_ref, hs_ref):
    t = x_ref.shape[0]
    sl = slice(n * LRU_BLOCK, (n + 1) * LRU_BLOCK)
    x = x_ref[:, sl]
    out, h = _lru_tile(x, px_ref[:, sl], ph_ref[SUBLANES - 1:SUBLANES, sl], g_ref[:, sl],
                       cw_ref[:, sl], cb_ref[:, sl], wa_ref[n], ba_ref[:, sl], wx_ref[n],
                       bx_ref[:, sl], lam_ref[:, sl], (ab_ref.at[n], hs_ref.at[n]))
    o_ref[:, sl] = out.astype(BF16)
    px_ref[:, sl] = x[t - SUBLANES:]
    ph_ref[:, sl] = h[t - SUBLANES:]
    conv_ref[0, :, sl] = x[t - SUBLANES:]
    h_ref[0, :, sl] = h[t - SUBLANES:]


def _mix_prompt_kernel(q_ref, ka_ref, kb_ref, kc_ref, va_ref, vb_ref, vc_ref, ext_ref,
                       x_ref, g_ref, cw_ref, cb_ref, wa_ref, ba_ref, wx_ref, bx_ref, lam_ref,
                       c_ref, wm_ref, bm_ref,
                       attn_ref, lru_ref, conv_ref, h_ref, table_ref,
                       bias_ref, px_ref, ph_ref, ab_ref, hs_ref, *, scale, mod_dims):
    i = pl.program_id(2)

    @pl.when(i == 0)
    def _():
        _attn_prompt_bias(ext_ref, bias_ref, q_ref.shape[0])
        px_ref[...] = jnp.zeros_like(px_ref)
        ph_ref[...] = jnp.zeros_like(ph_ref)

    _mod_kernel(c_ref, wm_ref, bm_ref, table_ref, **mod_dims)

    for n in range(q_ref.shape[1] // HEAD_DIM):
        _attn_prompt_head(n, i, q_ref, (ka_ref, kb_ref, kc_ref), (va_ref, vb_ref, vc_ref),
                          bias_ref, attn_ref, scale)
        _lru_prompt_block(n, x_ref, g_ref, cw_ref, cb_ref, wa_ref, ba_ref, wx_ref, bx_ref, lam_ref,
                          lru_ref, conv_ref, h_ref, px_ref, ph_ref, ab_ref, hs_ref)


def _mix_prompt(z, rel_bias, lw, c_all, w_mod, b_mod, mod_col0, mod_dims,
                n_batch, seq, n_heads, col_x, col_g):
    conv_w, conv_b, w_a, b_a, w_x, b_x, lam = lw
    n_blocks = w_a.shape[0]
    tt = 256
    nb = 4
    lanes = nb * HEAD_DIM
    assert HEAD_DIM == LRU_BLOCK and n_heads == n_blocks and n_heads % nb == 0
    assert tt == (N_PREV_CHUNKS * CHUNK) // 2 and seq % tt == 0
    assert col_x % lanes == 0 and col_g % lanes == 0
    nt = seq // tt
    ng = n_heads // nb
    width = n_heads * HEAD_DIM
    ext = _bias_ramp(rel_bias, 3 * tt - 1, 4 * tt)
    cx, cg = col_x // lanes, col_g // lanes
    vec = lambda a: a.reshape(1, width)

    steps = n_batch * ng * nt
    d_model, mod_cols = w_mod.shape
    mtn = (mod_cols - mod_col0) // steps
    assert mtn % LANES == 0 and mtn * steps == mod_cols - mod_col0 and mod_col0 % mtn == 0
    n_groups = mod_dims["n_prompt"] * mod_dims["groups_per_prompt"] + mod_dims["n_sample"]
    step = lambda b, g, t: (b * ng + g) * nt + t

    def rows(back, col0):
        return pl.BlockSpec((tt, lanes),
                            lambda b, g, t: (b * nt + jnp.maximum(t - back, 0), col0 + g))

    vspec = pl.BlockSpec((1, lanes), lambda b, g, t: (0, g))
    wspec = pl.BlockSpec((nb, LRU_BLOCK, LRU_BLOCK), lambda b, g, t: (g, 0, 0))
    sspec = pl.BlockSpec((1, SUBLANES, lanes), lambda b, g, t: (b, 0, g))
    return pl.pallas_call(
        functools.partial(_mix_prompt_kernel, scale=HEAD_DIM ** -0.5, mod_dims=mod_dims),
        out_shape=(
            jax.ShapeDtypeStruct((n_batch * seq, width), BF16),
            jax.ShapeDtypeStruct((n_batch * seq, width), BF16),
            jax.ShapeDtypeStruct((n_batch, SUBLANES, width), F32),
            jax.ShapeDtypeStruct((n_batch, SUBLANES, width), F32),
            jax.ShapeDtypeStruct((n_groups, mod_cols - mod_col0), F32),
        ),
        grid=(n_batch, ng, nt),
        in_specs=[
            rows(0, 0),
            rows(2, ng), rows(1, ng), rows(0, ng),
            rows(2, 2 * ng), rows(1, 2 * ng), rows(0, 2 * ng),
            pl.BlockSpec((nb, 1, 4 * tt), lambda b, g, t: (g, 0, 0)),
            rows(0, cx), rows(0, cg),
            pl.BlockSpec((conv_w.shape[0], lanes), lambda b, g, t: (0, g)),
            vspec, wspec, vspec, wspec, vspec, vspec,
            pl.BlockSpec(c_all.shape, lambda b, g, t: (0, 0)),
            pl.BlockSpec((d_model, mtn), lambda b, g, t: (0, mod_col0 // mtn + step(b, g, t))),
            pl.BlockSpec((1, mtn), lambda b, g, t: (0, mod_col0 // mtn + step(b, g, t))),
        ],
        out_specs=(rows(0, 0), rows(0, 0), sspec, sspec,
                   pl.BlockSpec((n_groups, mtn), lambda b, g, t: (0, step(b, g, t)))),
        scratch_shapes=[pltpu.VMEM((nb, 3, tt, tt), F32),
                        pltpu.VMEM((SUBLANES, lanes), F32), pltpu.VMEM((SUBLANES, lanes), F32),
                        pltpu.VMEM((nb, 2, tt, LRU_BLOCK), F32),
                        pltpu.VMEM((nb, tt // SUBLANES, LRU_BLOCK), F32)],
        compiler_params=_params(("arbitrary", "arbitrary", "arbitrary"), 40),
        name="mix_prompt",
    )(z, z, z, z, z, z, z, ext, z, z, conv_w, vec(conv_b), w_a, vec(b_a), w_x, vec(b_x), vec(lam),
      c_all, w_mod, b_mod)


def _lru_sample_kernel(x_ref, g_ref, prev_ref, h0_ref, cw_ref, cb_ref, wa_ref, ba_ref, wx_ref,
                       bx_ref, lam_ref, o_ref, conv_ref, h_ref, *, n_blocks):
    t = x_ref.shape[0]
    for n in range(n_blocks):
        sl = slice(n * LRU_BLOCK, (n + 1) * LRU_BLOCK)
        x = x_ref[:, sl]
        out, h = _lru_tile(x, prev_ref[0, :, sl], h0_ref[0, :, sl], g_ref[:, sl],
                           cw_ref[:, sl], cb_ref[:, sl], wa_ref[n], ba_ref[:, sl], wx_ref[n],
                           bx_ref[:, sl], lam_ref[:, sl])
        o_ref[:, sl] = out.astype(BF16)
        conv_ref[0, :, sl] = x[t - SUBLANES:]
        h_ref[0, :, sl] = h[t - SUBLANES:]


def _lru_sample(z, state_conv, state_h, lw, row0, n_batch, t, col_x, col_g):
    conv_w, conv_b, w_a, b_a, w_x, b_x, lam = lw
    n_blocks = w_a.shape[0]
    width = n_blocks * LRU_BLOCK
    k = conv_w.shape[0]
    assert t >= SUBLANES and k - 1 <= SUBLANES
    prev = jnp.pad(state_conv, ((0, 0), (SUBLANES - (k - 1), 0), (0, 0)))
    h0 = state_h.reshape(n_batch, 1, width)
    rb0 = row0 // t
    cx, cg = col_x // width, col_g // width
    vec = lambda a: a.reshape(1, width)
    vspec = pl.BlockSpec((1, width), lambda b: (0, 0))
    wspec = pl.BlockSpec((n_blocks, LRU_BLOCK, LRU_BLOCK), lambda b: (0, 0, 0))
    sspec = pl.BlockSpec((1, SUBLANES, width), lambda b: (b, 0, 0))
    return pl.pallas_call(
        functools.partial(_lru_sample_kernel, n_blocks=n_blocks),
        out_shape=(
            jax.ShapeDtypeStruct((n_batch * t, width), BF16),
            jax.ShapeDtypeStruct((n_batch, SUBLANES, width), F32),
            jax.ShapeDtypeStruct((n_batch, SUBLANES, width), F32),
        ),
        grid=(n_batch,),
        in_specs=[
            pl.BlockSpec((t, width), lambda b: (rb0 + b, cx)),
            pl.BlockSpec((t, width), lambda b: (rb0 + b, cg)),
            sspec,
            pl.BlockSpec((1, 1, width), lambda b: (b, 0, 0)),
            pl.BlockSpec((k, width), lambda b: (0, 0)),
            vspec, wspec, vspec, wspec, vspec, vspec,
        ],
        out_specs=(pl.BlockSpec((t, width), lambda b: (b, 0)), sspec, sspec),
        compiler_params=_params(("arbitrary",), 32),
        name="lru_sample",
    )(z, z, prev, h0, conv_w, vec(conv_b), w_a, vec(b_a), w_x, vec(b_x), vec(lam))


def _out_proj_kernel(ap_ref, lp_ref, as_ref, ls_ref, w_ref, xp_ref, xs_ref, gp_ref, gs_ref,
                     op_ref, os_ref, wb_ref):
    i = pl.program_id(1)

    @pl.when(i == 0)
    def _():
        wb_ref[...] = w_ref[...].astype(BF16)

    ka = ap_ref.shape[1]

    def tile(a_ref, l_ref, x_ref, g_ref, o_ref):
        acc = (jnp.dot(a_ref[...], wb_ref[:ka, :], preferred_element_type=F32)
               + jnp.dot(l_ref[...], wb_ref[ka:, :], preferred_element_type=F32))
        o_ref[...] = x_ref[...] + _group_scale(acc, g_ref[...])

    _per_row_group(i,
                   lambda: tile(ap_ref, lp_ref, xp_ref, gp_ref, op_ref),
                   lambda: tile(as_ref, ls_ref, xs_ref, gs_ref, os_ref))


def _out_proj(attn_p, lru_p, attn_s, lru_s, w_out, x_p, x_s, table, gate_col, group):
    d, n = w_out.shape
    ka, kl = attn_p.shape[1], lru_p.shape[1]
    assert ka + kl == d
    tn = 512
    rg = _RowGroups(x_p.shape[0], x_s.shape[0])
    gcol = lambda j: gate_col * (n // tn) + j
    sample_table_block = (rg.rows_p // group) // (rg.rows_s // group)
    return pl.pallas_call(
        _out_proj_kernel,
        out_shape=(jax.ShapeDtypeStruct((rg.rows_p, n), F32), jax.ShapeDtypeStruct((rg.rows_s, n), F32)),
        grid=(n // tn, rg.steps),
        in_specs=[
            rg.prompt_spec(ka, lambda j: 0),
            rg.prompt_spec(kl, lambda j: 0),
            rg.sample_spec(ka, lambda j: 0, resident=True),
            rg.sample_spec(kl, lambda j: 0, resident=True),
            pl.BlockSpec((d, tn), lambda j, i: (0, j)),
            rg.prompt_spec(tn, lambda j: j),
            rg.sample_spec(tn, lambda j: j),
            rg.prompt_spec(tn, gcol, rows_per_row=group),
            rg.sample_spec(tn, gcol, rows_per_row=group, row_block=sample_table_block),
        ],
        out_specs=(rg.prompt_spec(tn, lambda j: j, output=True), rg.sample_spec(tn, lambda j: j)),
        scratch_shapes=[pltpu.VMEM((d, tn), BF16)],
        compiler_params=_params(("arbitrary", "arbitrary"), 58),
        name="out_proj",
    )(attn_p, lru_p, attn_s, lru_s, w_out, x_p, x_s, table, table)


def _ffn_up_kernel(xp_ref, xs_ref, wg_ref, wu_ref, op_ref, os_ref, wgb_ref, wub_ref, *,
                   n_real):
    j = pl.program_id(0)
    i = pl.program_id(1)

    @pl.when(i == 0)
    def _():
        wgb_ref[...] = wg_ref[...].astype(BF16)
        wub_ref[...] = wu_ref[...].astype(BF16)

    def tile(x_ref, o_ref):
        @pl.when(j < n_real)
        def _():
            x = x_ref[...]
            g = jnp.dot(x, wgb_ref[...], preferred_element_type=F32)
            u = jnp.dot(x, wub_ref[...], preferred_element_type=F32)
            o_ref[...] = (g * _sigmoid(g) * u).astype(BF16)

        @pl.when(j >= n_real)
        def _():
            o_ref[...] = jnp.zeros_like(o_ref)

    _per_row_group(i, lambda: tile(xp_ref, op_ref), lambda: tile(xs_ref, os_ref))


def _ffn_up(u_p, u_s, w_gate, w_up, ff_pad):
    d, ff = w_gate.shape
    tn = 256
    n_real = ff // tn
    assert ff % tn == 0 and ff_pad % tn == 0
    rg = _RowGroups(u_p.shape[0], u_s.shape[0])
    wspec = pl.BlockSpec((d, tn), lambda j, i: (0, jnp.minimum(j, n_real - 1)))
    return pl.pallas_call(
        functools.partial(_ffn_up_kernel, n_real=n_real),
        out_shape=(jax.ShapeDtypeStruct((rg.rows_p, ff_pad), BF16),
                   jax.ShapeDtypeStruct((rg.rows_s, ff_pad), BF16)),
        grid=(ff_pad // tn, rg.steps),
        in_specs=[rg.prompt_spec(d, lambda j: 0), rg.sample_spec(d, lambda j: 0, resident=True),
                  wspec, wspec],
        out_specs=(rg.prompt_spec(tn, lambda j: j, output=True), rg.sample_spec(tn, lambda j: j)),
        scratch_shapes=[pltpu.VMEM((d, tn), BF16), pltpu.VMEM((d, tn), BF16)],
        compiler_params=_params(("arbitrary", "arbitrary"), 52),
        name="ffn_up",
    )(u_p, u_s, w_gate, w_up)


def _ffn_down_kernel(h_ref, w_ref, x_ref, g_ref, o_ref, *, k_total):
    k = pl.program_id(2)
    tk = w_ref.shape[0]
    row = lax.broadcasted_iota(jnp.int32, w_ref.shape, 0)
    w = jnp.where(row < k_total - k * tk, w_ref[...], 0.0).astype(BF16)

    @pl.when(k == 0)
    def _():
        o_ref[...] = jnp.zeros_like(o_ref)

    tm = o_ref.shape[0]
    rows = min(tm, 512)
    for r0 in range(0, tm, rows):
        o_ref[r0:r0 + rows, :] += jnp.dot(h_ref[r0:r0 + rows, :], w, preferred_element_type=F32)

    @pl.when(k == pl.num_programs(2) - 1)
    def _():
        o_ref[...] = x_ref[...] + _group_scale(o_ref[...], g_ref[...])


def _ffn_down(h, w_down, x1, table, gate_col, group, group0, tm, tn):
    rows, ff_pad = h.shape
    ff, n = w_down.shape
    tk = 1024
    tn = min(tn, n)
    gt = tm // group
    assert rows % tm == 0 and ff_pad % tk == 0 and n % tn == 0 and group0 % gt == 0
    g0 = group0 // gt
    gcol0 = gate_col * (n // tn)
    return pl.pallas_call(
        functools.partial(_ffn_down_kernel, k_total=ff),
        out_shape=jax.ShapeDtypeStruct((rows, n), F32),
        grid=(rows // tm, n // tn, ff_pad // tk),
        in_specs=[
            pl.BlockSpec((tm, tk), lambda i, j, k: (i, k)),
            pl.BlockSpec((tk, tn), lambda i, j, k: (k, j)),
            pl.BlockSpec((tm, tn), lambda i, j, k: (i, j)),
            pl.BlockSpec((gt, tn), lambda i, j, k: (g0 + i, gcol0 + j)),
        ],
        out_specs=pl.BlockSpec((tm, tn), lambda i, j, k: (i, j)),
        compiler_params=_params(("arbitrary", "arbitrary", "arbitrary"), 58),
        name="ffn_down",
    )(h, w_down, x1, table)


def _layer(x_p, x_s, c_prompt, c_sample, cache_k, cache_v, state_conv, state_h, lw, dims):
    (norm_mix_g, norm_ffn_g, w_mod, b_mod, w_in, q_norm_g, k_norm_g, rel_bias,
     conv_w, conv_b, w_rg_a, b_rg_a, w_rg_x, b_rg_x, lru_lambda, w_out,
     w_ffn_gate, w_ffn_up, w_ffn_down) = lw
    n_prompt, seq, n_sample, t_s = dims
    group = t_s
    n_heads = rel_bias.shape[0]
    attn_w = n_heads * HEAD_DIM
    lru_w = w_rg_a.shape[0] * LRU_BLOCK
    rows_p = n_prompt * seq
    rows_s = n_sample * t_s
    groups_p = rows_p // group

    d = x_p.shape[1]
    c_all = _mod_inputs(c_prompt, c_sample)
    b_mod2 = b_mod.reshape(1, -1)
    mod_dims = dict(n_sample=n_sample, n_prompt=n_prompt, groups_per_prompt=seq // group)
    table1 = _mod_table(c_all, w_mod, b_mod2, 2 * d, **mod_dims)
    u_p = _norm_mod(x_p, norm_mix_g, table1, 0, 1, group, 0)
    u_s = _norm_mod(x_s, norm_mix_g, table1, 0, 1, group, groups_p)
    qk_gain = jnp.concatenate([jnp.tile(q_norm_g, n_heads), jnp.tile(k_norm_g, n_heads)]).reshape(1, -1)
    z_p, z_s = _in_proj(u_p, u_s, w_in, qk_gain, attn_w)

    lru_params = (conv_w, conv_b, w_rg_a, b_rg_a, w_rg_x, b_rg_x, lru_lambda)
    col_x, col_g = 3 * attn_w, 3 * attn_w + lru_w
    attn_p, lru_p, conv_p, h_p, table2 = _mix_prompt(
        z_p, rel_bias, lru_params, c_all, w_mod, b_mod2, 2 * d, mod_dims,
        n_prompt, seq, n_heads, col_x, col_g)
    attn_s = _attn_sample(z_s, cache_k, cache_v, rel_bias, 0, n_sample, t_s, n_heads)
    lru_s, conv_s, h_s = _lru_sample(z_s, state_conv, state_h, lru_params, 0, n_sample, t_s,
                                     col_x, col_g)

    x1_p, x1_s = _out_proj(attn_p, lru_p, attn_s, lru_s, w_out, x_p, x_s, table2, 0, group)
    u2_p = _norm_mod(x1_p, norm_ffn_g, table2, 1, 2, group, 0)
    u2_s = _norm_mod(x1_s, norm_ffn_g, table2, 1, 2, group, groups_p)
    ff = w_ffn_gate.shape[1]
    ff_pad = -(-ff // 1024) * 1024
    hid_p, hid_s = _ffn_up(u2_p, u2_s, w_ffn_gate, w_ffn_up, ff_pad)
    y_p = _ffn_down(hid_p, w_ffn_down, x1_p, table2, 3, group, 0, 2048, 1024)
    y_s = _ffn_down(hid_s, w_ffn_down, x1_s, table2, 3, group, groups_p, rows_s, 2048)

    keep = min(N_PREV_CHUNKS * CHUNK, seq)
    kv_p = jnp.stack([z_p[(b + 1) * seq - keep:(b + 1) * seq, attn_w:3 * attn_w] for b in range(n_prompt)])
    kv_p = kv_p.reshape(n_prompt, keep, 2, n_heads, HEAD_DIM)
    kv_s = z_s[:, attn_w:3 * attn_w].reshape(n_sample, t_s, 2, n_heads, HEAD_DIM)
    k_conv = conv_w.shape[0] - 1
    state = dict(
        k_p=kv_p[:, :, 0], v_p=kv_p[:, :, 1], conv_p=conv_p[:, SUBLANES - k_conv:], h_p=h_p[:, SUBLANES - 1],
        k_s=kv_s[:, :, 0], v_s=kv_s[:, :, 1], conv_s=conv_s[:, SUBLANES - k_conv:], h_s=h_s[:, SUBLANES - 1])
    return y_p, y_s, state


def kernel(x_prompt, x_sample, cache_k, cache_v, state_conv, state_h, c_prompt, c_sample, norm_mix_g, norm_ffn_g, w_mod, b_mod, w_in, q_norm_g, k_norm_g, rel_bias, conv_w, conv_b, w_rg_a, b_rg_a, w_rg_x, b_rg_x, lru_lambda, w_out, w_ffn_gate, w_ffn_up, w_ffn_down):
    n_prompt, seq, d = x_prompt.shape
    n_sample, t_s, _ = x_sample.shape
    depth = w_in.shape[0]
    dims = (n_prompt, seq, n_sample, t_s)
    yp = x_prompt.reshape(n_prompt * seq, d)
    ys = x_sample.reshape(n_sample * t_s, d)
    states = []
    for l in range(depth):
        lw = (norm_mix_g[l], norm_ffn_g[l], w_mod[l], b_mod[l], w_in[l], q_norm_g[l], k_norm_g[l],
              rel_bias[l], conv_w[l], conv_b[l], w_rg_a[l], b_rg_a[l], w_rg_x[l], b_rg_x[l],
              lru_lambda[l], w_out[l], w_ffn_gate[l], w_ffn_up[l], w_ffn_down[l])
        yp, ys, st = _layer(yp, ys, c_prompt, c_sample, cache_k[l], cache_v[l], state_conv[l],
                            state_h[l], lw, dims)
        states.append(st)
    stack = lambda name: jnp.stack([s[name] for s in states])
    return (yp.reshape(n_prompt, seq, d), ys.reshape(n_sample, t_s, d),
            stack("k_p"), stack("v_p"), stack("conv_p"), stack("h_p"),
            stack("k_s"), stack("v_s"), stack("conv_s"), stack("h_s"))
```

```python
import functools
import math

import jax
import jax.numpy as jnp
from jax import lax
from jax.experimental import pallas as pl
from jax.experimental.pallas import tpu as pltpu

F32 = jnp.float32
BF16 = jnp.bfloat16

CHUNK = 64
N_PREV_CHUNKS = 8
HEAD_DIM = 128
LRU_BLOCK = 128
LRU_C = 8.0
NEG_INF = -1e30
EPS = 1e-6
LOG2E = math.log2(math.e)

LANES = 128
SUBLANES = 8
MIB = 1024 * 1024


def _params(semantics, vmem_mib):
    return pltpu.CompilerParams(dimension_semantics=semantics, vmem_limit_bytes=vmem_mib * MIB)


def _sigmoid(x):
    return 0.5 * jnp.tanh(0.5 * x) + 0.5


def _gelu_tanh(x):
    c = math.sqrt(2.0 / math.pi)
    return x * (0.5 * (1.0 + jnp.tanh(c * (x + 0.044715 * (x * x * x)))))


def _group_scale(y, g):
    rows, n = y.shape
    groups = g.shape[0]
    return (y.reshape(groups, rows // groups, n) * g[:, None, :]).reshape(rows, n)


PROMPT_TILE_ROWS = 1024


class _RowGroups:
    def __init__(self, rows_p, rows_s, tile=PROMPT_TILE_ROWS):
        assert rows_p % tile == 0
        self.rows_p, self.rows_s, self.tile = rows_p, rows_s, tile
        self.np_tiles = rows_p // tile
        self.steps = self.np_tiles + 1

    def prompt_spec(self, cols, col_block, rows_per_row=1, output=False):
        first = 0 if output else self.np_tiles - 1
        return pl.BlockSpec((self.tile // rows_per_row, cols),
                            lambda j, i: (jnp.where(i == 0, first, i - 1), col_block(j)))

    def sample_spec(self, cols, col_block, rows_per_row=1, row_block=0, resident=False):
        mode = dict(pipeline_mode=pl.Buffered(1)) if resident else {}
        return pl.BlockSpec((self.rows_s // rows_per_row, cols),
                            lambda j, i: (row_block, col_block(j)), **mode)


def _per_row_group(i, prompt_fn, sample_fn):
    pl.when(i == 0)(sample_fn)
    pl.when(i > 0)(prompt_fn)


def _mod_kernel(c_ref, w_ref, b_ref, o_ref, *, n_sample, n_prompt, groups_per_prompt):
    c = c_ref[...]
    s = (c * _sigmoid(c)).astype(BF16)
    m = jnp.dot(s, w_ref[...].astype(BF16), preferred_element_type=F32) + b_ref[...]
    tn = m.shape[1]
    for b in range(n_prompt):
        row = m[n_sample + b:n_sample + b + 1, :]
        o_ref[b * groups_per_prompt:(b + 1) * groups_per_prompt, :] = jnp.broadcast_to(
            row, (groups_per_prompt, tn))
    o_ref[n_prompt * groups_per_prompt:n_prompt * groups_per_prompt + n_sample, :] = m[:n_sample, :]


def _mod_inputs(c_prompt, c_sample):
    rows = c_sample.shape[0] + c_prompt.shape[0]
    rows_pad = -(-rows // SUBLANES) * SUBLANES
    return jnp.concatenate(
        [c_sample, c_prompt, jnp.zeros((rows_pad - rows, c_prompt.shape[1]), c_prompt.dtype)], axis=0)


def _mod_table(c_all, w_mod, b_mod, n_cols, n_sample, n_prompt, groups_per_prompt):
    rows_pad, d = c_all.shape
    n_groups = n_prompt * groups_per_prompt + n_sample
    tn = 512
    return pl.pallas_call(
        functools.partial(_mod_kernel, n_sample=n_sample, n_prompt=n_prompt,
                          groups_per_prompt=groups_per_prompt),
        out_shape=jax.ShapeDtypeStruct((n_groups, n_cols), F32),
        grid=(n_cols // tn,),
        in_specs=[
            pl.BlockSpec((rows_pad, d), lambda j: (0, 0)),
            pl.BlockSpec((d, tn), lambda j: (0, j)),
            pl.BlockSpec((1, tn), lambda j: (0, j)),
        ],
        out_specs=pl.BlockSpec((n_groups, tn), lambda j: (0, j)),
        compiler_params=_params(("arbitrary",), 40),
        name="mod_table",
    )(c_all, w_mod, b_mod)


def _norm_mod_kernel(x_ref, gain_ref, shift_ref, scale_ref, o_ref):
    x = x_ref[...]
    ms = jnp.mean(x * x, axis=-1, keepdims=True)
    y = x * lax.rsqrt(ms + EPS) * gain_ref[...]
    rows, d = y.shape
    groups = scale_ref.shape[0]
    y3 = y.reshape(groups, rows // groups, d)
    u = y3 * (1.0 + scale_ref[...][:, None, :]) + shift_ref[...][:, None, :]
    o_ref[...] = u.reshape(rows, d).astype(BF16)


def _norm_mod(x, gain, table, shift_col, scale_col, group, group0):
    rows, d = x.shape
    tm = 512
    gt = tm // group
    assert rows % tm == 0 and group0 % gt == 0
    g0 = group0 // gt
    return pl.pallas_call(
        _norm_mod_kernel,
        out_shape=jax.ShapeDtypeStruct((rows, d), BF16),
        grid=(rows // tm,),
        in_specs=[
            pl.BlockSpec((tm, d), lambda i: (i, 0)),
            pl.BlockSpec((1, d), lambda i: (0, 0)),
            pl.BlockSpec((gt, d), lambda i: (g0 + i, shift_col)),
            pl.BlockSpec((gt, d), lambda i: (g0 + i, scale_col)),
        ],
        out_specs=pl.BlockSpec((tm, d), lambda i: (i, 0)),
        compiler_params=_params(("arbitrary",), 40),
        name="norm_mod",
    )(x, gain.reshape(1, d), table, table)


def _in_proj_kernel(xp_ref, xs_ref, w_ref, gain_ref, op_ref, os_ref, wb_ref, *,
                    n_norm_tiles):
    j = pl.program_id(0)
    i = pl.program_id(1)

    @pl.when(i == 0)
    def _():
        wb_ref[...] = w_ref[...].astype(BF16)

    def tile(x_ref, o_ref):
        acc = jnp.dot(x_ref[...], wb_ref[...], preferred_element_type=F32)
        tn = acc.shape[1]

        @pl.when(j < n_norm_tiles)
        def _():
            for g in range(tn // HEAD_DIM):
                sl = slice(g * HEAD_DIM, (g + 1) * HEAD_DIM)
                zg = acc[:, sl]
                ms = jnp.mean(zg * zg, axis=-1, keepdims=True)
                o_ref[:, sl] = zg * lax.rsqrt(ms + EPS) * gain_ref[:, sl]

        @pl.when(j >= n_norm_tiles)
        def _():
            o_ref[...] = acc

    _per_row_group(i, lambda: tile(xp_ref, op_ref), lambda: tile(xs_ref, os_ref))


def _in_proj(u_p, u_s, w_in, qk_gain, attn_width):
    d, n = w_in.shape
    tn = 512
    n_norm_tiles = 2 * attn_width // tn
    rg = _RowGroups(u_p.shape[0], u_s.shape[0])
    return pl.pallas_call(
        functools.partial(_in_proj_kernel, n_norm_tiles=n_norm_tiles),
        out_shape=(jax.ShapeDtypeStruct((rg.rows_p, n), F32), jax.ShapeDtypeStruct((rg.rows_s, n), F32)),
        grid=(n // tn, rg.steps),
        in_specs=[
            rg.prompt_spec(d, lambda j: 0),
            rg.sample_spec(d, lambda j: 0, resident=True),
            pl.BlockSpec((d, tn), lambda j, i: (0, j)),
            pl.BlockSpec((1, tn), lambda j, i: (0, jnp.minimum(j, n_norm_tiles - 1))),
        ],
        out_specs=(rg.prompt_spec(tn, lambda j: j, output=True), rg.sample_spec(tn, lambda j: j)),
        scratch_shapes=[pltpu.VMEM((d, tn), BF16)],
        compiler_params=_params(("arbitrary", "arbitrary"), 56),
        name="in_proj",
    )(u_p, u_s, w_in, qk_gain)


def _bias_ramp(rel_bias, top, length):
    n_heads, n_rel = rel_bias.shape
    clip = (n_rel - 1) // 2
    lead = top - clip
    tail = length - lead - n_rel
    assert lead >= 0 and tail >= 0
    rev = rel_bias[:, ::-1]
    ext = jnp.concatenate([
        jnp.broadcast_to(rel_bias[:, n_rel - 1:], (n_heads, lead)),
        rev,
        jnp.broadcast_to(rel_bias[:, :1], (n_heads, tail)),
    ], axis=1)
    return ext.reshape(n_heads, 1, length)


def _lane_reduce(tiles, combine, reduce):
    parts = [t[:, c:c + LANES] for t in tiles for c in range(0, t.shape[1], LANES)]
    return reduce(functools.reduce(combine, parts), axis=-1, keepdims=True)


def _toeplitz(base, rows):
    w = base.shape[1]
    return pltpu.roll(jnp.broadcast_to(base, (rows, w)), w - (rows - 1), 1, stride=1, stride_axis=0)


def _attn_prompt_bias(ext_ref, bias_ref, tq):
    shift = CHUNK.bit_length() - 1
    rowc = jnp.right_shift(lax.broadcasted_iota(jnp.int32, (tq, tq), 0), shift)
    colc = jnp.right_shift(lax.broadcasted_iota(jnp.int32, (tq, tq), 1), shift)
    for h in range(bias_ref.shape[0]):
        for m in range(3):
            off = (2 - m) * tq
            t = _toeplitz(ext_ref[h, :, off:off + 2 * tq], tq)[:, :tq]
            if m == 0:
                t = jnp.where(colc <= rowc, t, NEG_INF)
            if m == 2:
                t = jnp.where(rowc <= colc, t, NEG_INF)
            bias_ref[h, m] = t * LOG2E


def _attn_prompt_head(h, i, q_ref, k_refs, v_refs, bias_ref, o_ref, scale):
    dn = (((1,), (1,)), ((), ()))
    sl = slice(h * HEAD_DIM, (h + 1) * HEAD_DIM)
    q = q_ref[:, sl].astype(BF16)
    scores = []
    for m, k_ref in zip((2, 1, 0), k_refs):
        s = lax.dot_general(q, k_ref[:, sl].astype(BF16), dn, preferred_element_type=F32)
        s = s * (scale * LOG2E) + bias_ref[h, m]
        if m > 0:
            s = s + jnp.where(i >= m, 0.0, NEG_INF)
        scores.append(s)
    mx = _lane_reduce(scores, jnp.maximum, jnp.max)
    probs = [jnp.exp2(s - mx) for s in scores]
    denom = _lane_reduce(probs, jnp.add, jnp.sum)
    out = functools.reduce(jnp.add, [
        jnp.dot(p.astype(BF16), v_ref[:, sl].astype(BF16), preferred_element_type=F32)
        for p, v_ref in zip(probs, v_refs)])
    o_ref[:, sl] = (out / denom).astype(BF16)


def _attn_sample_kernel(q_ref, kn_ref, vn_ref, ck_ref, cv_ref, ext_ref, o_ref, *, scale, n_heads):
    t = q_ref.shape[0]
    r = ck_ref.shape[1] // n_heads
    dn = (((1,), (1,)), ((), ()))
    for h in range(n_heads):
        sl = slice(h * HEAD_DIM, (h + 1) * HEAD_DIM)
        rows_h = pl.ds(h, r, stride=n_heads)
        bias = _toeplitz(ext_ref[h], t)
        q = q_ref[:, sl].astype(BF16)
        s1 = lax.dot_general(q, ck_ref[0, rows_h, :].astype(BF16), dn, preferred_element_type=F32)
        s1 = s1 * scale + bias[:, :r]
        s2 = lax.dot_general(q, kn_ref[:, sl].astype(BF16), dn, preferred_element_type=F32)
        s2 = s2 * scale + bias[:, r:r + t]
        mx = jnp.maximum(jnp.max(s1, axis=-1, keepdims=True), jnp.max(s2, axis=-1, keepdims=True))
        p1 = jnp.exp(s1 - mx)
        p2 = jnp.exp(s2 - mx)
        denom = jnp.sum(p1, axis=-1, keepdims=True) + jnp.sum(p2, axis=-1, keepdims=True)
        out = (jnp.dot(p1.astype(BF16), cv_ref[0, rows_h, :].astype(BF16), preferred_element_type=F32)
               + jnp.dot(p2.astype(BF16), vn_ref[:, sl].astype(BF16), preferred_element_type=F32))
        o_ref[:, sl] = (out / denom).astype(BF16)


def _attn_sample(z, cache_k, cache_v, rel_bias, row0, n_batch, t, n_heads):
    r = cache_k.shape[1]
    width = n_heads * HEAD_DIM
    ck = cache_k.reshape(n_batch, r * n_heads, HEAD_DIM)
    cv = cache_v.reshape(n_batch, r * n_heads, HEAD_DIM)
    top = r + t - 1
    clip = (rel_bias.shape[1] - 1) // 2
    ramp = -(-max(r + 2 * t, top + clip + 1) // LANES) * LANES
    ext = _bias_ramp(rel_bias, top, ramp)
    rb0 = row0 // t
    return pl.pallas_call(
        functools.partial(_attn_sample_kernel, scale=HEAD_DIM ** -0.5, n_heads=n_heads),
        out_shape=jax.ShapeDtypeStruct((n_batch * t, width), BF16),
        grid=(n_batch,),
        in_specs=[
            pl.BlockSpec((t, width), lambda b: (rb0 + b, 0)),
            pl.BlockSpec((t, width), lambda b: (rb0 + b, 1)),
            pl.BlockSpec((t, width), lambda b: (rb0 + b, 2)),
            pl.BlockSpec((1, r * n_heads, HEAD_DIM), lambda b: (b, 0, 0)),
            pl.BlockSpec((1, r * n_heads, HEAD_DIM), lambda b: (b, 0, 0)),
            pl.BlockSpec((n_heads, 1, ramp), lambda b: (0, 0, 0)),
        ],
        out_specs=pl.BlockSpec((t, width), lambda b: (b, 0)),
        compiler_params=_params(("arbitrary",), 40),
        name="attn_sample",
    )(z, z, z, ck, cv, ext)


def _lru_tile(x, prev, h_in, gate_in, cw, cb, wa, ba, wx, bx, lam, scan_scratch=None):
    t = x.shape[0]
    p = prev.shape[0]
    k = cw.shape[0]
    xcat = jnp.concatenate([prev, x], axis=0)
    xc = cb
    for i in range(k):
        lo = p - (k - 1) + i
        xc = xc + xcat[lo:lo + t] * cw[i:i + 1]
    xcb = xc.astype(BF16)
    rg = _sigmoid(jnp.dot(xcb, wa.astype(BF16), preferred_element_type=F32) + ba)
    ig = _sigmoid(jnp.dot(xcb, wx.astype(BF16), preferred_element_type=F32) + bx)
    neg = -lam
    softplus = jnp.maximum(neg, 0.0) + jnp.log1p(jnp.exp(-jnp.abs(neg)))
    log_a = (-LRU_C) * rg * softplus
    a = jnp.exp(log_a)
    var = -jnp.tanh(log_a) * (a * a + 1.0)
    std = jnp.where(var == 0.0, 0.0, var * lax.rsqrt(var))
    b = std * (ig * xc)
    h = _linear_scan(a, b, h_in, scan_scratch)
    return h * _gelu_tanh(gate_in), h


def _doubling_scan(a, b, axis):
    n = a.shape[axis]
    idx = lax.broadcasted_iota(jnp.int32, a.shape, axis)
    step = 1
    while step < n:
        keep = idx >= step
        a_prev = jnp.where(keep, pltpu.roll(a, step, axis), 1.0)
        b_prev = jnp.where(keep, pltpu.roll(b, step, axis), 0.0)
        b = a * b_prev + b
        a = a * a_prev
        step *= 2
    return a, b


def _linear_scan(a, b, h_in, scratch=None):
    t, c = a.shape
    groups = t // SUBLANES
    a3, b3 = _doubling_scan(a.reshape(groups, SUBLANES, c), b.reshape(groups, SUBLANES, c), 1)
    if scratch is not None:
        ab_ref, hs_ref = scratch
        a2, b2 = a3.reshape(t, c), b3.reshape(t, c)
        ab_ref[0] = a2
        ab_ref[1] = b2
        ends = pl.ds(SUBLANES - 1, groups, stride=SUBLANES)
        a_cum, b_cum = _doubling_scan(ab_ref[0, ends, :], ab_ref[1, ends, :], 0)
        h_end = a_cum * h_in + b_cum
        first = lax.broadcasted_iota(jnp.int32, h_end.shape, 0) == 0
        hs_ref[...] = jnp.where(first, h_in, pltpu.roll(h_end, 1, 0))
        h_start = jnp.concatenate(
            [jnp.broadcast_to(hs_ref[g:g + 1, :], (SUBLANES, c)) for g in range(groups)], axis=0)
        return a2 * h_start + b2
    ae = jnp.broadcast_to(a3[:, SUBLANES - 1:, :], a3.shape).reshape(t, c)
    be = jnp.broadcast_to(b3[:, SUBLANES - 1:, :], b3.shape).reshape(t, c)
    step = SUBLANES
    while step < t:
        be = jnp.concatenate([be[:step], ae[step:] * be[:-step] + be[step:]], axis=0)
        ae = jnp.concatenate([ae[:step], ae[step:] * ae[:-step]], axis=0)
        step *= 2
    h_end = ae * h_in + be
    h_start = jnp.concatenate([jnp.broadcast_to(h_in, (SUBLANES, c)), h_end[:t - SUBLANES]], axis=0)
    return a3.reshape(t, c) * h_start + b3.reshape(t, c)


def _lru_prompt_block(n, x_ref, g_ref, cw_ref, cb_ref, wa_ref, ba_ref, wx_ref, bx_ref, lam_ref,
                      o_ref, conv_ref, h_ref, px_ref, ph_ref, ab_ref, hs_ref):
    t = x_ref.shape[0]
    sl = slice(n * LRU_BLOCK, (n + 1) * LRU_BLOCK)
    x = x_ref[:, sl]
    out, h = _lru_tile(x, px_ref[:, sl], ph_ref[SUBLANES - 1:SUBLANES, sl], g_ref[:, sl],
                       cw_ref[:, sl], cb_ref[:, sl], wa_ref[n], ba_ref[:, sl], wx_ref[n],
                       bx_ref[:, sl], lam_ref[:, sl], (ab_ref.at[n], hs_ref.at[n]))
    o_ref[:, sl] = out.astype(BF16)
    px_ref[:, sl] = x[t - SUBLANES:]
    ph_ref[:, sl] = h[t - SUBLANES:]
    conv_ref[0, :, sl] = x[t - SUBLANES:]
    h_ref[0, :, sl] = h[t - SUBLANES:]


def _mix_prompt_kernel(q_ref, ka_ref, kb_ref, kc_ref, va_ref, vb_ref, vc_ref, ext_ref,
                       x_ref, g_ref, cw_ref, cb_ref, wa_ref, ba_ref, wx_ref, bx_ref, lam_ref,
                       c_ref, wm_ref, bm_ref,
                       attn_ref, lru_ref, conv_ref, h_ref, table_ref,
                       bias_ref, px_ref, ph_ref, ab_ref, hs_ref, *, scale, mod_dims):
    i = pl.program_id(2)

    @pl.when(i == 0)
    def _():
        _attn_prompt_bias(ext_ref, bias_ref, q_ref.shape[0])
        px_ref[...] = jnp.zeros_like(px_ref)
        ph_ref[...] = jnp.zeros_like(ph_ref)

    _mod_kernel(c_ref, wm_ref, bm_ref, table_ref, **mod_dims)

    for n in range(q_ref.shape[1] // HEAD_DIM):
        _attn_prompt_head(n, i, q_ref, (ka_ref, kb_ref, kc_ref), (va_ref, vb_ref, vc_ref),
                          bias_ref, attn_ref, scale)
        _lru_prompt_block(n, x_ref, g_ref, cw_ref, cb_ref, wa_ref, ba_ref, wx_ref, bx_ref, lam_ref,
                          lru_ref, conv_ref, h_ref, px_ref, ph_ref, ab_ref, hs_ref)


def _mix_prompt(z, rel_bias, lw, c_all, w_mod, b_mod, mod_col0, mod_dims,
                n_batch, seq, n_heads, col_x, col_g):
    conv_w, conv_b, w_a, b_a, w_x, b_x, lam = lw
    n_blocks = w_a.shape[0]
    tt = 256
    nb = 4
    lanes = nb * HEAD_DIM
    assert HEAD_DIM == LRU_BLOCK and n_heads == n_blocks and n_heads % nb == 0
    assert tt == (N_PREV_CHUNKS * CHUNK) // 2 and seq % tt == 0
    assert col_x % lanes == 0 and col_g % lanes == 0
    nt = seq // tt
    ng = n_heads // nb
    width = n_heads * HEAD_DIM
    ext = _bias_ramp(rel_bias, 3 * tt - 1, 4 * tt)
    cx, cg = col_x // lanes, col_g // lanes
    vec = lambda a: a.reshape(1, width)

    steps = n_batch * ng * nt
    d_model, mod_cols = w_mod.shape
    mtn = (mod_cols - mod_col0) // steps
    assert mtn % LANES == 0 and mtn * steps == mod_cols - mod_col0 and mod_col0 % mtn == 0
    n_groups = mod_dims["n_prompt"] * mod_dims["groups_per_prompt"] + mod_dims["n_sample"]
    step = lambda b, g, t: (b * ng + g) * nt + t

    def rows(back, col0):
        return pl.BlockSpec((tt, lanes),
                            lambda b, g, t: (b * nt + jnp.maximum(t - back, 0), col0 + g))

    vspec = pl.BlockSpec((1, lanes), lambda b, g, t: (0, g))
    wspec = pl.BlockSpec((nb, LRU_BLOCK, LRU_BLOCK), lambda b, g, t: (g, 0, 0))
    sspec = pl.BlockSpec((1, SUBLANES, lanes), lambda b, g, t: (b, 0, g))
    return pl.pallas_call(
        functools.partial(_mix_prompt_kernel, scale=HEAD_DIM ** -0.5, mod_dims=mod_dims),
        out_shape=(
            jax.ShapeDtypeStruct((n_batch * seq, width), BF16),
            jax.ShapeDtypeStruct((n_batch * seq, width), BF16),
            jax.ShapeDtypeStruct((n_batch, SUBLANES, width), F32),
            jax.ShapeDtypeStruct((n_batch, SUBLANES, width), F32),
            jax.ShapeDtypeStruct((n_groups, mod_cols - mod_col0), F32),
        ),
        grid=(n_batch, ng, nt),
        in_specs=[
            rows(0, 0),
            rows(2, ng), rows(1, ng), rows(0, ng),
            rows(2, 2 * ng), rows(1, 2 * ng), rows(0, 2 * ng),
            pl.BlockSpec((nb, 1, 4 * tt), lambda b, g, t: (g, 0, 0)),
            rows(0, cx), rows(0, cg),
            pl.BlockSpec((conv_w.shape[0], lanes), lambda b, g, t: (0, g)),
            vspec, wspec, vspec, wspec, vspec, vspec,
            pl.BlockSpec(c_all.shape, lambda b, g, t: (0, 0)),
            pl.BlockSpec((d_model, mtn), lambda b, g, t: (0, mod_col0 // mtn + step(b, g, t))),
            pl.BlockSpec((1, mtn), lambda b, g, t: (0, mod_col0 // mtn + step(b, g, t))),
        ],
        out_specs=(rows(0, 0), rows(0, 0), sspec, sspec,
                   pl.BlockSpec((n_groups, mtn), lambda b, g, t: (0, step(b, g, t)))),
        scratch_shapes=[pltpu.VMEM((nb, 3, tt, tt), F32),
                        pltpu.VMEM((SUBLANES, lanes), F32), pltpu.VMEM((SUBLANES, lanes), F32),
                        pltpu.VMEM((nb, 2, tt, LRU_BLOCK), F32),
                        pltpu.VMEM((nb, tt // SUBLANES, LRU_BLOCK), F32)],
        compiler_params=_params(("arbitrary", "arbitrary", "arbitrary"), 40),
        name="mix_prompt",
    )(z, z, z, z, z, z, z, ext, z, z, conv_w, vec(conv_b), w_a, vec(b_a), w_x, vec(b_x), vec(lam),
      c_all, w_mod, b_mod)


def _lru_sample_kernel(x_ref, g_ref, prev_ref, h0_ref, cw_ref, cb_ref, wa_ref, ba_ref, wx_ref,
                       bx_ref, lam_ref, o_ref, conv_ref, h_ref, *, n_blocks):
    t = x_ref.shape[0]
    for n in range(n_blocks):
        sl = slice(n * LRU_BLOCK, (n + 1) * LRU_BLOCK)
        x = x_ref[:, sl]
        out, h = _lru_tile(x, prev_ref[0, :, sl], h0_ref[0, :, sl], g_ref[:, sl],
                           cw_ref[:, sl], cb_ref[:, sl], wa_ref[n], ba_ref[:, sl], wx_ref[n],
                           bx_ref[:, sl], lam_ref[:, sl])
        o_ref[:, sl] = out.astype(BF16)
        conv_ref[0, :, sl] = x[t - SUBLANES:]
        h_ref[0, :, sl] = h[t - SUBLANES:]


def _lru_sample(z, state_conv, state_h, lw, row0, n_batch, t, col_x, col_g):
    conv_w, conv_b, w_a, b_a, w_x, b_x, lam = lw
    n_blocks = w_a.shape[0]
    width = n_blocks * LRU_BLOCK
    k = conv_w.shape[0]
    assert t >= SUBLANES and k - 1 <= SUBLANES
    prev = jnp.pad(state_conv, ((0, 0), (SUBLANES - (k - 1), 0), (0, 0)))
    h0 = state_h.reshape(n_batch, 1, width)
    rb0 = row0 // t
    cx, cg = col_x // width, col_g // width
    vec = lambda a: a.reshape(1, width)
    vspec = pl.BlockSpec((1, width), lambda b: (0, 0))
    wspec = pl.BlockSpec((n_blocks, LRU_BLOCK, LRU_BLOCK), lambda b: (0, 0, 0))
    sspec = pl.BlockSpec((1, SUBLANES, width), lambda b: (b, 0, 0))
    return pl.pallas_call(
        functools.partial(_lru_sample_kernel, n_blocks=n_blocks),
        out_shape=(
            jax.ShapeDtypeStruct((n_batch * t, width), BF16),
            jax.ShapeDtypeStruct((n_batch, SUBLANES, width), F32),
            jax.ShapeDtypeStruct((n_batch, SUBLANES, width), F32),
        ),
        grid=(n_batch,),
        in_specs=[
            pl.BlockSpec((t, width), lambda b: (rb0 + b, cx)),
            pl.BlockSpec((t, width), lambda b: (rb0 + b, cg)),
            sspec,
            pl.BlockSpec((1, 1, width), lambda b: (b, 0, 0)),
            pl.BlockSpec((k, width), lambda b: (0, 0)),
            vspec, wspec, vspec, wspec, vspec, vspec,
        ],
        out_specs=(pl.BlockSpec((t, width), lambda b: (b, 0)), sspec, sspec),
        compiler_params=_params(("arbitrary",), 32),
        name="lru_sample",
    )(z, z, prev, h0, conv_w, vec(conv_b), w_a, vec(b_a), w_x, vec(b_x), vec(lam))


def _out_proj_kernel(ap_ref, lp_ref, as_ref, ls_ref, w_ref, xp_ref, xs_ref, gp_ref, gs_ref,
                     op_ref, os_ref, wb_ref):
    i = pl.program_id(1)

    @pl.when(i == 0)
    def _():
        wb_ref[...] = w_ref[...].astype(BF16)

    ka = ap_ref.shape[1]

    def tile(a_ref, l_ref, x_ref, g_ref, o_ref):
        acc = (jnp.dot(a_ref[...], wb_ref[:ka, :], preferred_element_type=F32)
               + jnp.dot(l_ref[...], wb_ref[ka:, :], preferred_element_type=F32))
        o_ref[...] = x_ref[...] + _group_scale(acc, g_ref[...])

    _per_row_group(i,
                   lambda: tile(ap_ref, lp_ref, xp_ref, gp_ref, op_ref),
                   lambda: tile(as_ref, ls_ref, xs_ref, gs_ref, os_ref))


def _out_proj(attn_p, lru_p, attn_s, lru_s, w_out, x_p, x_s, table, gate_col, group):
    d, n = w_out.shape
    ka, kl = attn_p.shape[1], lru_p.shape[1]
    assert ka + kl == d
    tn = 512
    rg = _RowGroups(x_p.shape[0], x_s.shape[0])
    gcol = lambda j: gate_col * (n // tn) + j
    sample_table_block = (rg.rows_p // group) // (rg.rows_s // group)
    return pl.pallas_call(
        _out_proj_kernel,
        out_shape=(jax.ShapeDtypeStruct((rg.rows_p, n), F32), jax.ShapeDtypeStruct((rg.rows_s, n), F32)),
        grid=(n // tn, rg.steps),
        in_specs=[
            rg.prompt_spec(ka, lambda j: 0),
            rg.prompt_spec(kl, lambda j: 0),
            rg.sample_spec(ka, lambda j: 0, resident=True),
            rg.sample_spec(kl, lambda j: 0, resident=True),
            pl.BlockSpec((d, tn), lambda j, i: (0, j)),
            rg.prompt_spec(tn, lambda j: j),
            rg.sample_spec(tn, lambda j: j),
            rg.prompt_spec(tn, gcol, rows_per_row=group),
            rg.sample_spec(tn, gcol, rows_per_row=group, row_block=sample_table_block),
        ],
        out_specs=(rg.prompt_spec(tn, lambda j: j, output=True), rg.sample_spec(tn, lambda j: j)),
        scratch_shapes=[pltpu.VMEM((d, tn), BF16)],
        compiler_params=_params(("arbitrary", "arbitrary"), 58),
        name="out_proj",
    )(attn_p, lru_p, attn_s, lru_s, w_out, x_p, x_s, table, table)


def _ffn_up_kernel(xp_ref, xs_ref, wg_hbm, wu_hbm, op_ref, os_ref, stage_ref, wb_ref, sem, *,
                   n_cols):
    j = pl.program_id(0)
    i = pl.program_id(1)
    tn = wb_ref.shape[2]
    n_full, rem = divmod(n_cols, tn)

    def copies(jj, width):
        col = pl.multiple_of(jj * tn, tn)
        return [pltpu.make_async_copy(src.at[:, pl.ds(col, width)],
                                      stage_ref.at[w, :, pl.ds(0, width)], sem.at[w])
                for w, src in enumerate((wg_hbm, wu_hbm))]

    def start(jj):
        @pl.when(jj < n_full)
        def _():
            for c in copies(jj, tn):
                c.start()

        if rem:
            @pl.when(jj == n_full)
            def _():
                for c in copies(jj, rem):
                    c.start()

    @pl.when((j == 0) & (i == 0))
    def _():
        start(j)

    @pl.when(i == 0)
    def _():
        @pl.when(j < n_full)
        def _():
            for c in copies(j, tn):
                c.wait()
            wb_ref[...] = stage_ref[...].astype(BF16)

        if rem:
            @pl.when(j == n_full)
            def _():
                for c in copies(j, rem):
                    c.wait()
                wb_ref[:, :, :rem] = stage_ref[:, :, :rem].astype(BF16)
                wb_ref[:, :, rem:] = jnp.zeros((2, wb_ref.shape[1], tn - rem), BF16)

        @pl.when(j >= n_full + (1 if rem else 0))
        def _():
            wb_ref[...] = jnp.zeros_like(wb_ref)

    @pl.when((i == 1) & (j + 1 < pl.num_programs(0)))
    def _():
        start(j + 1)

    def tile(x_ref, o_ref):
        x = x_ref[...]
        g = jnp.dot(x, wb_ref[0], preferred_element_type=F32)
        u = jnp.dot(x, wb_ref[1], preferred_element_type=F32)
        o_ref[...] = (g * _sigmoid(g) * u).astype(BF16)

    _per_row_group(i, lambda: tile(xp_ref, op_ref), lambda: tile(xs_ref, os_ref))


def _ffn_up(u_p, u_s, w_gate, w_up, ff_pad):
    d, ff = w_gate.shape
    tn = 512
    assert ff % LANES == 0 and ff_pad % tn == 0
    rg = _RowGroups(u_p.shape[0], u_s.shape[0])
    assert rg.steps >= 2
    hbm = pl.BlockSpec(memory_space=pl.ANY)
    return pl.pallas_call(
        functools.partial(_ffn_up_kernel, n_cols=ff),
        out_shape=(jax.ShapeDtypeStruct((rg.rows_p, ff_pad), BF16),
                   jax.ShapeDtypeStruct((rg.rows_s, ff_pad), BF16)),
        grid=(ff_pad // tn, rg.steps),
        in_specs=[rg.prompt_spec(d, lambda j: 0), rg.sample_spec(d, lambda j: 0, resident=True),
                  hbm, hbm],
        out_specs=(rg.prompt_spec(tn, lambda j: j, output=True), rg.sample_spec(tn, lambda j: j)),
        scratch_shapes=[pltpu.VMEM((2, d, tn), F32), pltpu.VMEM((2, d, tn), BF16),
                        pltpu.SemaphoreType.DMA((2,))],
        compiler_params=_params(("arbitrary", "arbitrary"), 58),
        name="ffn_up",
    )(u_p, u_s, w_gate, w_up)


def _ffn_down_kernel(h_ref, w_ref, x_ref, g_ref, o_ref, *, k_total):
    k = pl.program_id(2)
    tk = w_ref.shape[0]
    row = lax.broadcasted_iota(jnp.int32, w_ref.shape, 0)
    w = jnp.where(row < k_total - k * tk, w_ref[...], 0.0).astype(BF16)

    @pl.when(k == 0)
    def _():
        o_ref[...] = jnp.zeros_like(o_ref)

    tm = o_ref.shape[0]
    rows = min(tm, 512)
    for r0 in range(0, tm, rows):
        o_ref[r0:r0 + rows, :] += jnp.dot(h_ref[r0:r0 + rows, :], w, preferred_element_type=F32)

    @pl.when(k == pl.num_programs(2) - 1)
    def _():
        o_ref[...] = x_ref[...] + _group_scale(o_ref[...], g_ref[...])


def _ffn_down(h, w_down, x1, table, gate_col, group, group0, tm, tn):
    rows, ff_pad = h.shape
    ff, n = w_down.shape
    tk = 1024
    tn = min(tn, n)
    gt = tm // group
    assert rows % tm == 0 and ff_pad % tk == 0 and n % tn == 0 and group0 % gt == 0
    g0 = group0 // gt
    gcol0 = gate_col * (n // tn)
    return pl.pallas_call(
        functools.partial(_ffn_down_kernel, k_total=ff),
        out_shape=jax.ShapeDtypeStruct((rows, n), F32),
        grid=(rows // tm, n // tn, ff_pad // tk),
        in_specs=[
            pl.BlockSpec((tm, tk), lambda i, j, k: (i, k)),
            pl.BlockSpec((tk, tn), lambda i, j, k: (k, j)),
            pl.BlockSpec((tm, tn), lambda i, j, k: (i, j)),
            pl.BlockSpec((gt, tn), lambda i, j, k: (g0 + i, gcol0 + j)),
        ],
        out_specs=pl.BlockSpec((tm, tn), lambda i, j, k: (i, j)),
        compiler_params=_params(("arbitrary", "arbitrary", "arbitrary"), 58),
        name="ffn_down",
    )(h, w_down, x1, table)


def _layer(x_p, x_s, c_prompt, c_sample, cache_k, cache_v, state_conv, state_h, lw, dims):
    (norm_mix_g, norm_ffn_g, w_mod, b_mod, w_in, q_norm_g, k_norm_g, rel_bias,
     conv_w, conv_b, w_rg_a, b_rg_a, w_rg_x, b_rg_x, lru_lambda, w_out,
     w_ffn_gate, w_ffn_up, w_ffn_down) = lw
    n_prompt, seq, n_sample, t_s = dims
    group = t_s
    n_heads = rel_bias.shape[0]
    attn_w = n_heads * HEAD_DIM
    lru_w = w_rg_a.shape[0] * LRU_BLOCK
    rows_p = n_prompt * seq
    rows_s = n_sample * t_s
    groups_p = rows_p // group

    d = x_p.shape[1]
    c_all = _mod_inputs(c_prompt, c_sample)
    b_mod2 = b_mod.reshape(1, -1)
    mod_dims = dict(n_sample=n_sample, n_prompt=n_prompt, groups_per_prompt=seq // group)
    table1 = _mod_table(c_all, w_mod, b_mod2, 2 * d, **mod_dims)
    u_p = _norm_mod(x_p, norm_mix_g, table1, 0, 1, group, 0)
    u_s = _norm_mod(x_s, norm_mix_g, table1, 0, 1, group, groups_p)
    qk_gain = jnp.concatenate([jnp.tile(q_norm_g, n_heads), jnp.tile(k_norm_g, n_heads)]).reshape(1, -1)
    z_p, z_s = _in_proj(u_p, u_s, w_in, qk_gain, attn_w)

    lru_params = (conv_w, conv_b, w_rg_a, b_rg_a, w_rg_x, b_rg_x, lru_lambda)
    col_x, col_g = 3 * attn_w, 3 * attn_w + lru_w
    attn_p, lru_p, conv_p, h_p, table2 = _mix_prompt(
        z_p, rel_bias, lru_params, c_all, w_mod, b_mod2, 2 * d, mod_dims,
        n_prompt, seq, n_heads, col_x, col_g)
    attn_s = _attn_sample(z_s, cache_k, cache_v, rel_bias, 0, n_sample, t_s, n_heads)
    lru_s, conv_s, h_s = _lru_sample(z_s, state_conv, state_h, lru_params, 0, n_sample, t_s,
                                     col_x, col_g)

    x1_p, x1_s = _out_proj(attn_p, lru_p, attn_s, lru_s, w_out, x_p, x_s, table2, 0, group)
    u2_p = _norm_mod(x1_p, norm_ffn_g, table2, 1, 2, group, 0)
    u2_s = _norm_mod(x1_s, norm_ffn_g, table2, 1, 2, group, groups_p)
    ff = w_ffn_gate.shape[1]
    ff_pad = -(-ff // 1024) * 1024
    hid_p, hid_s = _ffn_up(u2_p, u2_s, w_ffn_gate, w_ffn_up, ff_pad)
    y_p = _ffn_down(hid_p, w_ffn_down, x1_p, table2, 3, group, 0, 2048, 1024)
    y_s = _ffn_down(hid_s, w_ffn_down, x1_s, table2, 3, group, groups_p, rows_s, 2048)

    keep = min(N_PREV_CHUNKS * CHUNK, seq)
    kv_p = jnp.stack([z_p[(b + 1) * seq - keep:(b + 1) * seq, attn_w:3 * attn_w] for b in range(n_prompt)])
    kv_p = kv_p.reshape(n_prompt, keep, 2, n_heads, HEAD_DIM)
    kv_s = z_s[:, attn_w:3 * attn_w].reshape(n_sample, t_s, 2, n_heads, HEAD_DIM)
    k_conv = conv_w.shape[0] - 1
    state = dict(
        k_p=kv_p[:, :, 0], v_p=kv_p[:, :, 1], conv_p=conv_p[:, SUBLANES - k_conv:], h_p=h_p[:, SUBLANES - 1],
        k_s=kv_s[:, :, 0], v_s=kv_s[:, :, 1], conv_s=conv_s[:, SUBLANES - k_conv:], h_s=h_s[:, SUBLANES - 1])
    return y_p, y_s, state


def kernel(x_prompt, x_sample, cache_k, cache_v, state_conv, state_h, c_prompt, c_sample, norm_mix_g, norm_ffn_g, w_mod, b_mod, w_in, q_norm_g, k_norm_g, rel_bias, conv_w, conv_b, w_rg_a, b_rg_a, w_rg_x, b_rg_x, lru_lambda, w_out, w_ffn_gate, w_ffn_up, w_ffn_down):
    n_prompt, seq, d = x_prompt.shape
    n_sample, t_s, _ = x_sample.shape
    depth = w_in.shape[0]
    dims = (n_prompt, seq, n_sample, t_s)
    yp = x_prompt.reshape(n_prompt * seq, d)
    ys = x_sample.reshape(n_sample * t_s, d)
    states = []
    for l in range(depth):
        lw = (norm_mix_g[l], norm_ffn_g[l], w_mod[l], b_mod[l], w_in[l], q_norm_g[l], k_norm_g[l],
              rel_bias[l], conv_w[l], conv_b[l], w_rg_a[l], b_rg_a[l], w_rg_x[l], b_rg_x[l],
              lru_lambda[l], w_out[l], w_ffn_gate[l], w_ffn_up[l], w_ffn_down[l])
        yp, ys, st = _layer(yp, ys, c_prompt, c_sample, cache_k[l], cache_v[l], state_conv[l],
                            state_h[l], lw, dims)
        states.append(st)
    stack = lambda name: jnp.stack([s[name] for s in states])
    return (yp.reshape(n_prompt, seq, d), ys.reshape(n_sample, t_s, d),
            stack("k_p"), stack("v_p"), stack("conv_p"), stack("h_p"),
            stack("k_s"), stack("v_s"), stack("conv_s"), stack("h_s"))
```

```python
import functools
import math

import jax
import jax.numpy as jnp
from jax import lax
from jax.experimental import pallas as pl
from jax.experimental.pallas import tpu as pltpu

F32 = jnp.float32
BF16 = jnp.bfloat16

CHUNK = 64
N_PREV_CHUNKS = 8
HEAD_DIM = 128
LRU_BLOCK = 128
LRU_C = 8.0
NEG_INF = -1e30
EPS = 1e-6
LOG2E = math.log2(math.e)

LANES = 128
SUBLANES = 8
MIB = 1024 * 1024


def _params(semantics, vmem_mib):
    return pltpu.CompilerParams(dimension_semantics=semantics, vmem_limit_bytes=vmem_mib * MIB)


def _sigmoid(x):
    return 0.5 * jnp.tanh(0.5 * x) + 0.5


def _gelu_tanh(x):
    c = math.sqrt(2.0 / math.pi)
    return x * (0.5 * (1.0 + jnp.tanh(c * (x + 0.044715 * (x * x * x)))))


def _group_scale(y, g):
    rows, n = y.shape
    groups = g.shape[0]
    return (y.reshape(groups, rows // groups, n) * g[:, None, :]).reshape(rows, n)


PROMPT_TILE_ROWS = 1024


class _RowGroups:
    def __init__(self, rows_p, rows_s, tile=PROMPT_TILE_ROWS):
        assert rows_p % tile == 0
        self.rows_p, self.rows_s, self.tile = rows_p, rows_s, tile
        self.np_tiles = rows_p // tile
        self.steps = self.np_tiles + 1

    def prompt_spec(self, cols, col_block, rows_per_row=1, output=False):
        first = 0 if output else self.np_tiles - 1
        return pl.BlockSpec((self.tile // rows_per_row, cols),
                            lambda j, i: (jnp.where(i == 0, first, i - 1), col_block(j)))

    def sample_spec(self, cols, col_block, rows_per_row=1, row_block=0, resident=False):
        mode = dict(pipeline_mode=pl.Buffered(1)) if resident else {}
        return pl.BlockSpec((self.rows_s // rows_per_row, cols),
                            lambda j, i: (row_block, col_block(j)), **mode)


def _per_row_group(i, prompt_fn, sample_fn):
    pl.when(i == 0)(sample_fn)
    pl.when(i > 0)(prompt_fn)


def _mod_kernel(c_ref, w_ref, b_ref, o_ref, *, n_sample, n_prompt, groups_per_prompt):
    c = c_ref[...]
    s = (c * _sigmoid(c)).astype(BF16)
    m = jnp.dot(s, w_ref[...].astype(BF16), preferred_element_type=F32) + b_ref[...]
    tn = m.shape[1]
    for b in range(n_prompt):
        row = m[n_sample + b:n_sample + b + 1, :]
        o_ref[b * groups_per_prompt:(b + 1) * groups_per_prompt, :] = jnp.broadcast_to(
            row, (groups_per_prompt, tn))
    o_ref[n_prompt * groups_per_prompt:n_prompt * groups_per_prompt + n_sample, :] = m[:n_sample, :]


def _mod_inputs(c_prompt, c_sample):
    rows = c_sample.shape[0] + c_prompt.shape[0]
    rows_pad = -(-rows // SUBLANES) * SUBLANES
    return jnp.concatenate(
        [c_sample, c_prompt, jnp.zeros((rows_pad - rows, c_prompt.shape[1]), c_prompt.dtype)], axis=0)


def _mod_table(c_all, w_mod, b_mod, n_cols, n_sample, n_prompt, groups_per_prompt):
    rows_pad, d = c_all.shape
    n_groups = n_prompt * groups_per_prompt + n_sample
    tn = 512
    return pl.pallas_call(
        functools.partial(_mod_kernel, n_sample=n_sample, n_prompt=n_prompt,
                          groups_per_prompt=groups_per_prompt),
        out_shape=jax.ShapeDtypeStruct((n_groups, n_cols), F32),
        grid=(n_cols // tn,),
        in_specs=[
            pl.BlockSpec((rows_pad, d), lambda j: (0, 0)),
            pl.BlockSpec((d, tn), lambda j: (0, j)),
            pl.BlockSpec((1, tn), lambda j: (0, j)),
        ],
        out_specs=pl.BlockSpec((n_groups, tn), lambda j: (0, j)),
        compiler_params=_params(("arbitrary",), 40),
        name="mod_table",
    )(c_all, w_mod, b_mod)


def _norm_mod_kernel(x_ref, gain_ref, shift_ref, scale_ref, o_ref):
    x = x_ref[...]
    ms = jnp.mean(x * x, axis=-1, keepdims=True)
    y = x * lax.rsqrt(ms + EPS) * gain_ref[...]
    rows, d = y.shape
    groups = scale_ref.shape[0]
    y3 = y.reshape(groups, rows // groups, d)
    u = y3 * (1.0 + scale_ref[...][:, None, :]) + shift_ref[...][:, None, :]
    o_ref[...] = u.reshape(rows, d).astype(BF16)


def _norm_mod(x, gain, table, shift_col, scale_col, group, group0):
    rows, d = x.shape
    tm = 512
    gt = tm // group
    assert rows % tm == 0 and group0 % gt == 0
    g0 = group0 // gt
    return pl.pallas_call(
        _norm_mod_kernel,
        out_shape=jax.ShapeDtypeStruct((rows, d), BF16),
        grid=(rows // tm,),
        in_specs=[
            pl.BlockSpec((tm, d), lambda i: (i, 0)),
            pl.BlockSpec((1, d), lambda i: (0, 0)),
            pl.BlockSpec((gt, d), lambda i: (g0 + i, shift_col)),
            pl.BlockSpec((gt, d), lambda i: (g0 + i, scale_col)),
        ],
        out_specs=pl.BlockSpec((tm, d), lambda i: (i, 0)),
        compiler_params=_params(("arbitrary",), 40),
        name="norm_mod",
    )(x, gain.reshape(1, d), table, table)


def _in_proj_kernel(xp_ref, xs_ref, w_ref, gain_ref, op_ref, os_ref, wb_ref, *,
                    n_norm_tiles):
    j = pl.program_id(0)
    i = pl.program_id(1)

    @pl.when(i == 0)
    def _():
        wb_ref[...] = w_ref[...].astype(BF16)

    def tile(x_ref, o_ref):
        acc = jnp.dot(x_ref[...], wb_ref[...], preferred_element_type=F32)
        tn = acc.shape[1]

        @pl.when(j < n_norm_tiles)
        def _():
            for g in range(tn // HEAD_DIM):
                sl = slice(g * HEAD_DIM, (g + 1) * HEAD_DIM)
                zg = acc[:, sl]
                ms = jnp.mean(zg * zg, axis=-1, keepdims=True)
                o_ref[:, sl] = zg * lax.rsqrt(ms + EPS) * gain_ref[:, sl]

        @pl.when(j >= n_norm_tiles)
        def _():
            o_ref[...] = acc

    _per_row_group(i, lambda: tile(xp_ref, op_ref), lambda: tile(xs_ref, os_ref))


def _in_proj(u_p, u_s, w_in, qk_gain, attn_width):
    d, n = w_in.shape
    tn = 512
    n_norm_tiles = 2 * attn_width // tn
    rg = _RowGroups(u_p.shape[0], u_s.shape[0])
    return pl.pallas_call(
        functools.partial(_in_proj_kernel, n_norm_tiles=n_norm_tiles),
        out_shape=(jax.ShapeDtypeStruct((rg.rows_p, n), F32), jax.ShapeDtypeStruct((rg.rows_s, n), F32)),
        grid=(n // tn, rg.steps),
        in_specs=[
            rg.prompt_spec(d, lambda j: 0),
            rg.sample_spec(d, lambda j: 0, resident=True),
            pl.BlockSpec((d, tn), lambda j, i: (0, j)),
            pl.BlockSpec((1, tn), lambda j, i: (0, jnp.minimum(j, n_norm_tiles - 1))),
        ],
        out_specs=(rg.prompt_spec(tn, lambda j: j, output=True), rg.sample_spec(tn, lambda j: j)),
        scratch_shapes=[pltpu.VMEM((d, tn), BF16)],
        compiler_params=_params(("arbitrary", "arbitrary"), 56),
        name="in_proj",
    )(u_p, u_s, w_in, qk_gain)


def _bias_ramp(rel_bias, top, length):
    n_heads, n_rel = rel_bias.shape
    clip = (n_rel - 1) // 2
    lead = top - clip
    tail = length - lead - n_rel
    assert lead >= 0 and tail >= 0
    rev = rel_bias[:, ::-1]
    ext = jnp.concatenate([
        jnp.broadcast_to(rel_bias[:, n_rel - 1:], (n_heads, lead)),
        rev,
        jnp.broadcast_to(rel_bias[:, :1], (n_heads, tail)),
    ], axis=1)
    return ext.reshape(n_heads, 1, length)


def _lane_reduce(tiles, combine, reduce):
    parts = [t[:, c:c + LANES] for t in tiles for c in range(0, t.shape[1], LANES)]
    return reduce(functools.reduce(combine, parts), axis=-1, keepdims=True)


def _toeplitz(base, rows):
    w = base.shape[1]
    return pltpu.roll(jnp.broadcast_to(base, (rows, w)), w - (rows - 1), 1, stride=1, stride_axis=0)


def _attn_prompt_bias(ext_ref, bias_ref, tq):
    shift = CHUNK.bit_length() - 1
    rowc = jnp.right_shift(lax.broadcasted_iota(jnp.int32, (tq, tq), 0), shift)
    colc = jnp.right_shift(lax.broadcasted_iota(jnp.int32, (tq, tq), 1), shift)
    for h in range(bias_ref.shape[0]):
        for m in range(3):
            off = (2 - m) * tq
            t = _toeplitz(ext_ref[h, :, off:off + 2 * tq], tq)[:, :tq]
            if m == 0:
                t = jnp.where(colc <= rowc, t, NEG_INF)
            if m == 2:
                t = jnp.where(rowc <= colc, t, NEG_INF)
            bias_ref[h, m] = t * LOG2E


def _attn_prompt_head(h, i, q_ref, k_refs, v_refs, bias_ref, o_ref, scale):
    dn = (((1,), (1,)), ((), ()))
    sl = slice(h * HEAD_DIM, (h + 1) * HEAD_DIM)
    q = q_ref[:, sl].astype(BF16)
    scores = []
    for m, k_ref in zip((2, 1, 0), k_refs):
        s = lax.dot_general(q, k_ref[:, sl].astype(BF16), dn, preferred_element_type=F32)
        s = s * (scale * LOG2E) + bias_ref[h, m]
        if m > 0:
            s = s + jnp.where(i >= m, 0.0, NEG_INF)
        scores.append(s)
    mx = _lane_reduce(scores, jnp.maximum, jnp.max)
    probs = [jnp.exp2(s - mx) for s in scores]
    denom = _lane_reduce(probs, jnp.add, jnp.sum)
    out = functools.reduce(jnp.add, [
        jnp.dot(p.astype(BF16), v_ref[:, sl].astype(BF16), preferred_element_type=F32)
        for p, v_ref in zip(probs, v_refs)])
    o_ref[:, sl] = (out / denom).astype(BF16)


def _attn_sample_kernel(q_ref, kn_ref, vn_ref, ck_ref, cv_ref, ext_ref, o_ref, *, scale, n_heads):
    t = q_ref.shape[0]
    r = ck_ref.shape[1] // n_heads
    dn = (((1,), (1,)), ((), ()))
    for h in range(n_heads):
        sl = slice(h * HEAD_DIM, (h + 1) * HEAD_DIM)
        rows_h = pl.ds(h, r, stride=n_heads)
        bias = _toeplitz(ext_ref[h], t)
        q = q_ref[:, sl].astype(BF16)
        s1 = lax.dot_general(q, ck_ref[0, rows_h, :].astype(BF16), dn, preferred_element_type=F32)
        s1 = s1 * scale + bias[:, :r]
        s2 = lax.dot_general(q, kn_ref[:, sl].astype(BF16), dn, preferred_element_type=F32)
        s2 = s2 * scale + bias[:, r:r + t]
        mx = jnp.maximum(jnp.max(s1, axis=-1, keepdims=True), jnp.max(s2, axis=-1, keepdims=True))
        p1 = jnp.exp(s1 - mx)
        p2 = jnp.exp(s2 - mx)
        denom = jnp.sum(p1, axis=-1, keepdims=True) + jnp.sum(p2, axis=-1, keepdims=True)
        out = (jnp.dot(p1.astype(BF16), cv_ref[0, rows_h, :].astype(BF16), preferred_element_type=F32)
               + jnp.dot(p2.astype(BF16), vn_ref[:, sl].astype(BF16), preferred_element_type=F32))
        o_ref[:, sl] = (out / denom).astype(BF16)


def _attn_sample(z, cache_k, cache_v, rel_bias, row0, n_batch, t, n_heads):
    r = cache_k.shape[1]
    width = n_heads * HEAD_DIM
    ck = cache_k.reshape(n_batch, r * n_heads, HEAD_DIM)
    cv = cache_v.reshape(n_batch, r * n_heads, HEAD_DIM)
    top = r + t - 1
    clip = (rel_bias.shape[1] - 1) // 2
    ramp = -(-max(r + 2 * t, top + clip + 1) // LANES) * LANES
    ext = _bias_ramp(rel_bias, top, ramp)
    rb0 = row0 // t
    return pl.pallas_call(
        functools.partial(_attn_sample_kernel, scale=HEAD_DIM ** -0.5, n_heads=n_heads),
        out_shape=jax.ShapeDtypeStruct((n_batch * t, width), BF16),
        grid=(n_batch,),
        in_specs=[
            pl.BlockSpec((t, width), lambda b: (rb0 + b, 0)),
            pl.BlockSpec((t, width), lambda b: (rb0 + b, 1)),
            pl.BlockSpec((t, width), lambda b: (rb0 + b, 2)),
            pl.BlockSpec((1, r * n_heads, HEAD_DIM), lambda b: (b, 0, 0)),
            pl.BlockSpec((1, r * n_heads, HEAD_DIM), lambda b: (b, 0, 0)),
            pl.BlockSpec((n_heads, 1, ramp), lambda b: (0, 0, 0)),
        ],
        out_specs=pl.BlockSpec((t, width), lambda b: (b, 0)),
        compiler_params=_params(("arbitrary",), 40),
        name="attn_sample",
    )(z, z, z, ck, cv, ext)


def _lru_tile(x, prev, h_in, gate_in, cw, cb, wa, ba, wx, bx, lam, scan_scratch=None):
    t = x.shape[0]
    p = prev.shape[0]
    k = cw.shape[0]
    xcat = jnp.concatenate([prev, x], axis=0)
    xc = cb
    for i in range(k):
        lo = p - (k - 1) + i
        xc = xc + xcat[lo:lo + t] * cw[i:i + 1]
    xcb = xc.astype(BF16)
    rg = _sigmoid(jnp.dot(xcb, wa.astype(BF16), preferred_element_type=F32) + ba)
    ig = _sigmoid(jnp.dot(xcb, wx.astype(BF16), preferred_element_type=F32) + bx)
    neg = -lam
    softplus = jnp.maximum(neg, 0.0) + jnp.log1p(jnp.exp(-jnp.abs(neg)))
    log_a = (-LRU_C) * rg * softplus
    a = jnp.exp(log_a)
    var = -jnp.tanh(log_a) * (a * a + 1.0)
    std = jnp.where(var == 0.0, 0.0, var * lax.rsqrt(var))
    b = std * (ig * xc)
    h = _linear_scan(a, b, h_in, scan_scratch)
    return h * _gelu_tanh(gate_in), h


def _doubling_scan(a, b, axis):
    n = a.shape[axis]
    idx = lax.broadcasted_iota(jnp.int32, a.shape, axis)
    step = 1
    while step < n:
        keep = idx >= step
        a_prev = jnp.where(keep, pltpu.roll(a, step, axis), 1.0)
        b_prev = jnp.where(keep, pltpu.roll(b, step, axis), 0.0)
        b = a * b_prev + b
        a = a * a_prev
        step *= 2
    return a, b


def _linear_scan(a, b, h_in, scratch=None):
    t, c = a.shape
    groups = t // SUBLANES
    a3, b3 = _doubling_scan(a.reshape(groups, SUBLANES, c), b.reshape(groups, SUBLANES, c), 1)
    if scratch is not None:
        ab_ref, hs_ref = scratch
        a2, b2 = a3.reshape(t, c), b3.reshape(t, c)
        ab_ref[0] = a2
        ab_ref[1] = b2
        ends = pl.ds(SUBLANES - 1, groups, stride=SUBLANES)
        a_cum, b_cum = _doubling_scan(ab_ref[0, ends, :], ab_ref[1, ends, :], 0)
        h_end = a_cum * h_in + b_cum
        first = lax.broadcasted_iota(jnp.int32, h_end.shape, 0) == 0
        hs_ref[...] = jnp.where(first, h_in, pltpu.roll(h_end, 1, 0))
        h_start = jnp.concatenate(
            [jnp.broadcast_to(hs_ref[g:g + 1, :], (SUBLANES, c)) for g in range(groups)], axis=0)
        return a2 * h_start + b2
    ae = jnp.broadcast_to(a3[:, SUBLANES - 1:, :], a3.shape).reshape(t, c)
    be = jnp.broadcast_to(b3[:, SUBLANES - 1:, :], b3.shape).reshape(t, c)
    step = SUBLANES
    while step < t:
        be = jnp.concatenate([be[:step], ae[step:] * be[:-step] + be[step:]], axis=0)
        ae = jnp.concatenate([ae[:step], ae[step:] * ae[:-step]], axis=0)
        step *= 2
    h_end = ae * h_in + be
    h_start = jnp.concatenate([jnp.broadcast_to(h_in, (SUBLANES, c)), h_end[:t - SUBLANES]], axis=0)
    return a3.reshape(t, c) * h_start + b3.reshape(t, c)


def _lru_prompt_block(n, x_ref, g_ref, cw_ref, cb_ref, wa_ref, ba_ref, wx_ref, bx_ref, lam_ref,
                      o_ref, conv_ref, h_ref, px_ref, ph_ref, ab_ref, hs_ref):
    t = x_ref.shape[0]
    sl = slice(n * LRU_BLOCK, (n + 1) * LRU_BLOCK)
    x = x_ref[:, sl]
    out, h = _lru_tile(x, px_ref[:, sl], ph_ref[SUBLANES - 1:SUBLANES, sl], g_ref[:, sl],
                       cw_ref[:, sl], cb_ref[:, sl], wa_ref[n], ba_ref[:, sl], wx_ref[n],
                       bx_ref[:, sl], lam_ref[:, sl], (ab_ref.at[n], hs_ref.at[n]))
    o_ref[:, sl] = out.astype(BF16)
    px_ref[:, sl] = x[t - SUBLANES:]
    ph_ref[:, sl] = h[t - SUBLANES:]
    conv_ref[0, :, sl] = x[t - SUBLANES:]
    h_ref[0, :, sl] = h[t - SUBLANES:]


def _mix_prompt_kernel(q_ref, ka_ref, kb_ref, kc_ref, va_ref, vb_ref, vc_ref, ext_ref,
                       x_ref, g_ref, cw_ref, cb_ref, wa_ref, ba_ref, wx_ref, bx_ref, lam_ref,
                       c_ref, wm_ref, bm_ref,
                       attn_ref, lru_ref, conv_ref, h_ref, table_ref,
                       bias_ref, px_ref, ph_ref, ab_ref, hs_ref, *, scale, mod_dims):
    i = pl.program_id(2)

    @pl.when(i == 0)
    def _():
        _attn_prompt_bias(ext_ref, bias_ref, q_ref.shape[0])
        px_ref[...] = jnp.zeros_like(px_ref)
        ph_ref[...] = jnp.zeros_like(ph_ref)

    _mod_kernel(c_ref, wm_ref, bm_ref, table_ref, **mod_dims)

    for n in range(q_ref.shape[1] // HEAD_DIM):
        _attn_prompt_head(n, i, q_ref, (ka_ref, kb_ref, kc_ref), (va_ref, vb_ref, vc_ref),
                          bias_ref, attn_ref, scale)
        _lru_prompt_block(n, x_ref, g_ref, cw_ref, cb_ref, wa_ref, ba_ref, wx_ref, bx_ref, lam_ref,
                          lru_ref, conv_ref, h_ref, px_ref, ph_ref, ab_ref, hs_ref)


def _mix_prompt(z, rel_bias, lw, c_all, w_mod, b_mod, mod_col0, mod_dims,
                n_batch, seq, n_heads, col_x, col_g):
    conv_w, conv_b, w_a, b_a, w_x, b_x, lam = lw
    n_blocks = w_a.shape[0]
    tt = 256
    nb = 4
    lanes = nb * HEAD_DIM
    assert HEAD_DIM == LRU_BLOCK and n_heads == n_blocks and n_heads % nb == 0
    assert tt == (N_PREV_CHUNKS * CHUNK) // 2 and seq % tt == 0
    assert col_x % lanes == 0 and col_g % lanes == 0
    nt = seq // tt
    ng = n_heads // nb
    width = n_heads * HEAD_DIM
    ext = _bias_ramp(rel_bias, 3 * tt - 1, 4 * tt)
    cx, cg = col_x // lanes, col_g // lanes
    vec = lambda a: a.reshape(1, width)

    steps = n_batch * ng * nt
    d_model, mod_cols = w_mod.shape
    mtn = (mod_cols - mod_col0) // steps
    assert mtn % LANES == 0 and mtn * steps == mod_cols - mod_col0 and mod_col0 % mtn == 0
    n_groups = mod_dims["n_prompt"] * mod_dims["groups_per_prompt"] + mod_dims["n_sample"]
    step = lambda b, g, t: (b * ng + g) * nt + t

    def rows(back, col0):
        return pl.BlockSpec((tt, lanes),
                            lambda b, g, t: (b * nt + jnp.maximum(t - back, 0), col0 + g))

    vspec = pl.BlockSpec((1, lanes), lambda b, g, t: (0, g))
    wspec = pl.BlockSpec((nb, LRU_BLOCK, LRU_BLOCK), lambda b, g, t: (g, 0, 0))
    sspec = pl.BlockSpec((1, SUBLANES, lanes), lambda b, g, t: (b, 0, g))
    return pl.pallas_call(
        functools.partial(_mix_prompt_kernel, scale=HEAD_DIM ** -0.5, mod_dims=mod_dims),
        out_shape=(
            jax.ShapeDtypeStruct((n_batch * seq, width), BF16),
            jax.ShapeDtypeStruct((n_batch * seq, width), BF16),
            jax.ShapeDtypeStruct((n_batch, SUBLANES, width), F32),
            jax.ShapeDtypeStruct((n_batch, SUBLANES, width), F32),
            jax.ShapeDtypeStruct((n_groups, mod_cols - mod_col0), F32),
        ),
        grid=(n_batch, ng, nt),
        in_specs=[
            rows(0, 0),
            rows(2, ng), rows(1, ng), rows(0, ng),
            rows(2, 2 * ng), rows(1, 2 * ng), rows(0, 2 * ng),
            pl.BlockSpec((nb, 1, 4 * tt), lambda b, g, t: (g, 0, 0)),
            rows(0, cx), rows(0, cg),
            pl.BlockSpec((conv_w.shape[0], lanes), lambda b, g, t: (0, g)),
            vspec, wspec, vspec, wspec, vspec, vspec,
            pl.BlockSpec(c_all.shape, lambda b, g, t: (0, 0)),
            pl.BlockSpec((d_model, mtn), lambda b, g, t: (0, mod_col0 // mtn + step(b, g, t))),
            pl.BlockSpec((1, mtn), lambda b, g, t: (0, mod_col0 // mtn + step(b, g, t))),
        ],
        out_specs=(rows(0, 0), rows(0, 0), sspec, sspec,
                   pl.BlockSpec((n_groups, mtn), lambda b, g, t: (0, step(b, g, t)))),
        scratch_shapes=[pltpu.VMEM((nb, 3, tt, tt), F32),
                        pltpu.VMEM((SUBLANES, lanes), F32), pltpu.VMEM((SUBLANES, lanes), F32),
                        pltpu.VMEM((nb, 2, tt, LRU_BLOCK), F32),
                        pltpu.VMEM((nb, tt // SUBLANES, LRU_BLOCK), F32)],
        compiler_params=_params(("arbitrary", "arbitrary", "arbitrary"), 40),
        name="mix_prompt",
    )(z, z, z, z, z, z, z, ext, z, z, conv_w, vec(conv_b), w_a, vec(b_a), w_x, vec(b_x), vec(lam),
      c_all, w_mod, b_mod)


def _lru_sample_kernel(x_ref, g_ref, prev_ref, h0_ref, cw_ref, cb_ref, wa_ref, ba_ref, wx_ref,
                       bx_ref, lam_ref, o_ref, conv_ref, h_ref, *, n_blocks):
    t = x_ref.shape[0]
    for n in range(n_blocks):
        sl = slice(n * LRU_BLOCK, (n + 1) * LRU_BLOCK)
        x = x_ref[:, sl]
        out, h = _lru_tile(x, prev_ref[0, :, sl], h0_ref[0, :, sl], g_ref[:, sl],
                           cw_ref[:, sl], cb_ref[:, sl], wa_ref[n], ba_ref[:, sl], wx_ref[n],
                           bx_ref[:, sl], lam_ref[:, sl])
        o_ref[:, sl] = out.astype(BF16)
        conv_ref[0, :, sl] = x[t - SUBLANES:]
        h_ref[0, :, sl] = h[t - SUBLANES:]


def _lru_sample(z, state_conv, state_h, lw, row0, n_batch, t, col_x, col_g):
    conv_w, conv_b, w_a, b_a, w_x, b_x, lam = lw
    n_blocks = w_a.shape[0]
    width = n_blocks * LRU_BLOCK
    k = conv_w.shape[0]
    assert t >= SUBLANES and k - 1 <= SUBLANES
    prev = jnp.pad(state_conv, ((0, 0), (SUBLANES - (k - 1), 0), (0, 0)))
    h0 = state_h.reshape(n_batch, 1, width)
    rb0 = row0 // t
    cx, cg = col_x // width, col_g // width
    vec = lambda a: a.reshape(1, width)
    vspec = pl.BlockSpec((1, width), lambda b: (0, 0))
    wspec = pl.BlockSpec((n_blocks, LRU_BLOCK, LRU_BLOCK), lambda b: (0, 0, 0))
    sspec = pl.BlockSpec((1, SUBLANES, width), lambda b: (b, 0, 0))
    return pl.pallas_call(
        functools.partial(_lru_sample_kernel, n_blocks=n_blocks),
        out_shape=(
            jax.ShapeDtypeStruct((n_batch * t, width), BF16),
            jax.ShapeDtypeStruct((n_batch, SUBLANES, width), F32),
            jax.ShapeDtypeStruct((n_batch, SUBLANES, width), F32),
        ),
        grid=(n_batch,),
        in_specs=[
            pl.BlockSpec((t, width), lambda b: (rb0 + b, cx)),
            pl.BlockSpec((t, width), lambda b: (rb0 + b, cg)),
            sspec,
            pl.BlockSpec((1, 1, width), lambda b: (b, 0, 0)),
            pl.BlockSpec((k, width), lambda b: (0, 0)),
            vspec, wspec, vspec, wspec, vspec, vspec,
        ],
        out_specs=(pl.BlockSpec((t, width), lambda b: (b, 0)), sspec, sspec),
        compiler_params=_params(("arbitrary",), 32),
        name="lru_sample",
    )(z, z, prev, h0, conv_w, vec(conv_b), w_a, vec(b_a), w_x, vec(b_x), vec(lam))


def _out_proj_kernel(ap_ref, lp_ref, as_ref, ls_ref, w_ref, xp_ref, xs_ref, gp_ref, gs_ref,
                     op_ref, os_ref, wb_ref):
    i = pl.program_id(1)

    @pl.when(i == 0)
    def _():
        wb_ref[...] = w_ref[...].astype(BF16)

    ka = ap_ref.shape[1]

    def tile(a_ref, l_ref, x_ref, g_ref, o_ref):
        acc = (jnp.dot(a_ref[...], wb_ref[:ka, :], preferred_element_type=F32)
               + jnp.dot(l_ref[...], wb_ref[ka:, :], preferred_element_type=F32))
        o_ref[...] = x_ref[...] + _group_scale(acc, g_ref[...])

    _per_row_group(i,
                   lambda: tile(ap_ref, lp_ref, xp_ref, gp_ref, op_ref),
                   lambda: tile(as_ref, ls_ref, xs_ref, gs_ref, os_ref))


def _out_proj(attn_p, lru_p, attn_s, lru_s, w_out, x_p, x_s, table, gate_col, group):
    d, n = w_out.shape
    ka, kl = attn_p.shape[1], lru_p.shape[1]
    assert ka + kl == d
    tn = 512
    rg = _RowGroups(x_p.shape[0], x_s.shape[0])
    gcol = lambda j: gate_col * (n // tn) + j
    sample_table_block = (rg.rows_p // group) // (rg.rows_s // group)
    return pl.pallas_call(
        _out_proj_kernel,
        out_shape=(jax.ShapeDtypeStruct((rg.rows_p, n), F32), jax.ShapeDtypeStruct((rg.rows_s, n), F32)),
        grid=(n // tn, rg.steps),
        in_specs=[
            rg.prompt_spec(ka, lambda j: 0),
            rg.prompt_spec(kl, lambda j: 0),
            rg.sample_spec(ka, lambda j: 0, resident=True),
            rg.sample_spec(kl, lambda j: 0, resident=True),
            pl.BlockSpec((d, tn), lambda j, i: (0, j)),
            rg.prompt_spec(tn, lambda j: j),
            rg.sample_spec(tn, lambda j: j),
            rg.prompt_spec(tn, gcol, rows_per_row=group),
            rg.sample_spec(tn, gcol, rows_per_row=group, row_block=sample_table_block),
        ],
        out_specs=(rg.prompt_spec(tn, lambda j: j, output=True), rg.sample_spec(tn, lambda j: j)),
        scratch_shapes=[pltpu.VMEM((d, tn), BF16)],
        compiler_params=_params(("arbitrary", "arbitrary"), 58),
        name="out_proj",
    )(attn_p, lru_p, attn_s, lru_s, w_out, x_p, x_s, table, table)


def _ffn_up_kernel(xp_ref, xs_ref, wg_hbm, wu_hbm, wd_ref, op_ref, os_ref, wdb_ref,
                   stage_ref, wb_ref, sem, *, n_cols, n_wd_chunks):
    j = pl.program_id(0)
    i = pl.program_id(1)
    tn = wb_ref.shape[2]
    n_full, rem = divmod(n_cols, tn)

    rc = wd_ref.shape[0]
    chunk = j * pl.num_programs(1) + i

    @pl.when(chunk < n_wd_chunks)
    def _():
        row = lax.broadcasted_iota(jnp.int32, wd_ref.shape, 0)
        wdb_ref[...] = jnp.where(chunk * rc + row < n_cols, wd_ref[...], 0.0).astype(BF16)

    def copies(jj, width):
        col = pl.multiple_of(jj * tn, tn)
        return [pltpu.make_async_copy(src.at[:, pl.ds(col, width)],
                                      stage_ref.at[w, :, pl.ds(0, width)], sem.at[w])
                for w, src in enumerate((wg_hbm, wu_hbm))]

    def start(jj):
        @pl.when(jj < n_full)
        def _():
            for c in copies(jj, tn):
                c.start()

        if rem:
            @pl.when(jj == n_full)
            def _():
                for c in copies(jj, rem):
                    c.start()

    @pl.when((j == 0) & (i == 0))
    def _():
        start(j)

    @pl.when(i == 0)
    def _():
        @pl.when(j < n_full)
        def _():
            for c in copies(j, tn):
                c.wait()
            wb_ref[...] = stage_ref[...].astype(BF16)

        if rem:
            @pl.when(j == n_full)
            def _():
                for c in copies(j, rem):
                    c.wait()
                wb_ref[:, :, :rem] = stage_ref[:, :, :rem].astype(BF16)
                wb_ref[:, :, rem:] = jnp.zeros((2, wb_ref.shape[1], tn - rem), BF16)

        @pl.when(j >= n_full + (1 if rem else 0))
        def _():
            wb_ref[...] = jnp.zeros_like(wb_ref)

    @pl.when((i == 1) & (j + 1 < pl.num_programs(0)))
    def _():
        start(j + 1)

    def tile(x_ref, o_ref):
        x = x_ref[...]
        g = jnp.dot(x, wb_ref[0], preferred_element_type=F32)
        u = jnp.dot(x, wb_ref[1], preferred_element_type=F32)
        o_ref[...] = (g * _sigmoid(g) * u).astype(BF16)

    _per_row_group(i, lambda: tile(xp_ref, op_ref), lambda: tile(xs_ref, os_ref))


def _ffn_up(u_p, u_s, w_gate, w_up, w_down, ff_pad):
    d, ff = w_gate.shape
    n_out = w_down.shape[1]
    tn = 512
    assert ff % LANES == 0 and ff_pad % tn == 0
    rg = _RowGroups(u_p.shape[0], u_s.shape[0])
    assert rg.steps >= 2
    steps = (ff_pad // tn) * rg.steps
    rc = next(r for r in range(16, ff_pad + 1, 16)
              if ff_pad % r == 0 and ff % r == 0 and ff_pad // r <= steps)
    n_wd_chunks, n_real_chunks = ff_pad // rc, ff // rc
    chunk_of = lambda j, i: j * rg.steps + i
    hbm = pl.BlockSpec(memory_space=pl.ANY)
    return pl.pallas_call(
        functools.partial(_ffn_up_kernel, n_cols=ff, n_wd_chunks=n_wd_chunks),
        out_shape=(jax.ShapeDtypeStruct((rg.rows_p, ff_pad), BF16),
                   jax.ShapeDtypeStruct((rg.rows_s, ff_pad), BF16),
                   jax.ShapeDtypeStruct((ff_pad, n_out), BF16)),
        grid=(ff_pad // tn, rg.steps),
        in_specs=[rg.prompt_spec(d, lambda j: 0), rg.sample_spec(d, lambda j: 0, resident=True),
                  hbm, hbm,
                  pl.BlockSpec((rc, n_out),
                               lambda j, i: (jnp.minimum(chunk_of(j, i), n_real_chunks - 1), 0))],
        out_specs=(rg.prompt_spec(tn, lambda j: j, output=True), rg.sample_spec(tn, lambda j: j),
                   pl.BlockSpec((rc, n_out),
                                lambda j, i: (jnp.minimum(chunk_of(j, i), n_wd_chunks - 1), 0))),
        scratch_shapes=[pltpu.VMEM((2, d, tn), F32), pltpu.VMEM((2, d, tn), BF16),
                        pltpu.SemaphoreType.DMA((2,))],
        compiler_params=_params(("arbitrary", "arbitrary"), 62),
        name="ffn_up",
    )(u_p, u_s, w_gate, w_up, w_down)


def _ffn_down_kernel(h_ref, w_ref, x_ref, g_ref, o_ref):
    k = pl.program_id(2)

    @pl.when(k == 0)
    def _():
        o_ref[...] = jnp.zeros_like(o_ref)

    tm = o_ref.shape[0]
    rows = min(tm, 512)
    for r0 in range(0, tm, rows):
        o_ref[r0:r0 + rows, :] += jnp.dot(h_ref[r0:r0 + rows, :], w_ref[...],
                                          preferred_element_type=F32)

    @pl.when(k == pl.num_programs(2) - 1)
    def _():
        o_ref[...] = x_ref[...] + _group_scale(o_ref[...], g_ref[...])


def _ffn_down(h, w_down, x1, table, gate_col, group, group0, tm, tn):
    rows, ff_pad = h.shape
    n = w_down.shape[1]
    tk = 1024
    tn = min(tn, n)
    gt = tm // group
    assert w_down.shape[0] == ff_pad and w_down.dtype == BF16
    assert rows % tm == 0 and ff_pad % tk == 0 and n % tn == 0 and group0 % gt == 0
    g0 = group0 // gt
    gcol0 = gate_col * (n // tn)
    return pl.pallas_call(
        _ffn_down_kernel,
        out_shape=jax.ShapeDtypeStruct((rows, n), F32),
        grid=(rows // tm, n // tn, ff_pad // tk),
        in_specs=[
            pl.BlockSpec((tm, tk), lambda i, j, k: (i, k)),
            pl.BlockSpec((tk, tn), lambda i, j, k: (k, j)),
            pl.BlockSpec((tm, tn), lambda i, j, k: (i, j)),
            pl.BlockSpec((gt, tn), lambda i, j, k: (g0 + i, gcol0 + j)),
        ],
        out_specs=pl.BlockSpec((tm, tn), lambda i, j, k: (i, j)),
        compiler_params=_params(("arbitrary", "arbitrary", "arbitrary"), 58),
        name="ffn_down",
    )(h, w_down, x1, table)


def _layer(x_p, x_s, c_prompt, c_sample, cache_k, cache_v, state_conv, state_h, lw, dims):
    (norm_mix_g, norm_ffn_g, w_mod, b_mod, w_in, q_norm_g, k_norm_g, rel_bias,
     conv_w, conv_b, w_rg_a, b_rg_a, w_rg_x, b_rg_x, lru_lambda, w_out,
     w_ffn_gate, w_ffn_up, w_ffn_down) = lw
    n_prompt, seq, n_sample, t_s = dims
    group = t_s
    n_heads = rel_bias.shape[0]
    attn_w = n_heads * HEAD_DIM
    lru_w = w_rg_a.shape[0] * LRU_BLOCK
    rows_p = n_prompt * seq
    rows_s = n_sample * t_s
    groups_p = rows_p // group

    d = x_p.shape[1]
    c_all = _mod_inputs(c_prompt, c_sample)
    b_mod2 = b_mod.reshape(1, -1)
    mod_dims = dict(n_sample=n_sample, n_prompt=n_prompt, groups_per_prompt=seq // group)
    table1 = _mod_table(c_all, w_mod, b_mod2, 2 * d, **mod_dims)
    u_p = _norm_mod(x_p, norm_mix_g, table1, 0, 1, group, 0)
    u_s = _norm_mod(x_s, norm_mix_g, table1, 0, 1, group, groups_p)
    qk_gain = jnp.concatenate([jnp.tile(q_norm_g, n_heads), jnp.tile(k_norm_g, n_heads)]).reshape(1, -1)
    z_p, z_s = _in_proj(u_p, u_s, w_in, qk_gain, attn_w)

    lru_params = (conv_w, conv_b, w_rg_a, b_rg_a, w_rg_x, b_rg_x, lru_lambda)
    col_x, col_g = 3 * attn_w, 3 * attn_w + lru_w
    attn_p, lru_p, conv_p, h_p, table2 = _mix_prompt(
        z_p, rel_bias, lru_params, c_all, w_mod, b_mod2, 2 * d, mod_dims,
        n_prompt, seq, n_heads, col_x, col_g)
    attn_s = _attn_sample(z_s, cache_k, cache_v, rel_bias, 0, n_sample, t_s, n_heads)
    lru_s, conv_s, h_s = _lru_sample(z_s, state_conv, state_h, lru_params, 0, n_sample, t_s,
                                     col_x, col_g)

    x1_p, x1_s = _out_proj(attn_p, lru_p, attn_s, lru_s, w_out, x_p, x_s, table2, 0, group)
    u2_p = _norm_mod(x1_p, norm_ffn_g, table2, 1, 2, group, 0)
    u2_s = _norm_mod(x1_s, norm_ffn_g, table2, 1, 2, group, groups_p)
    ff = w_ffn_gate.shape[1]
    ff_pad = -(-ff // 1024) * 1024
    hid_p, hid_s, w_down_b = _ffn_up(u2_p, u2_s, w_ffn_gate, w_ffn_up, w_ffn_down, ff_pad)
    y_p = _ffn_down(hid_p, w_down_b, x1_p, table2, 3, group, 0, 2048, 1024)
    y_s = _ffn_down(hid_s, w_down_b, x1_s, table2, 3, group, groups_p, rows_s, 2048)

    keep = min(N_PREV_CHUNKS * CHUNK, seq)
    kv_p = jnp.stack([z_p[(b + 1) * seq - keep:(b + 1) * seq, attn_w:3 * attn_w] for b in range(n_prompt)])
    kv_p = kv_p.reshape(n_prompt, keep, 2, n_heads, HEAD_DIM)
    kv_s = z_s[:, attn_w:3 * attn_w].reshape(n_sample, t_s, 2, n_heads, HEAD_DIM)
    k_conv = conv_w.shape[0] - 1
    state = dict(
        k_p=kv_p[:, :, 0], v_p=kv_p[:, :, 1], conv_p=conv_p[:, SUBLANES - k_conv:], h_p=h_p[:, SUBLANES - 1],
        k_s=kv_s[:, :, 0], v_s=kv_s[:, :, 1], conv_s=conv_s[:, SUBLANES - k_conv:], h_s=h_s[:, SUBLANES - 1])
    return y_p, y_s, state


def kernel(x_prompt, x_sample, cache_k, cache_v, state_conv, state_h, c_prompt, c_sample, norm_mix_g, norm_ffn_g, w_mod, b_mod, w_in, q_norm_g, k_norm_g, rel_bias, conv_w, conv_b, w_rg_a, b_rg_a, w_rg_x, b_rg_x, lru_lambda, w_out, w_ffn_gate, w_ffn_up, w_ffn_down):
    n_prompt, seq, d = x_prompt.shape
    n_sample, t_s, _ = x_sample.shape
    depth = w_in.shape[0]
    dims = (n_prompt, seq, n_sample, t_s)
    yp = x_prompt.reshape(n_prompt * seq, d)
    ys = x_sample.reshape(n_sample * t_s, d)
    states = []
    for l in range(depth):
        lw = (norm_mix_g[l], norm_ffn_g[l], w_mod[l], b_mod[l], w_in[l], q_norm_g[l], k_norm_g[l],
              rel_bias[l], conv_w[l], conv_b[l], w_rg_a[l], b_rg_a[l], w_rg_x[l], b_rg_x[l],
              lru_lambda[l], w_out[l], w_ffn_gate[l], w_ffn_up[l], w_ffn_down[l])
        yp, ys, st = _layer(yp, ys, c_prompt, c_sample, cache_k[l], cache_v[l], state_conv[l],
                            state_h[l], lw, dims)
        states.append(st)
    stack = lambda name: jnp.stack([s[name] for s in states])
    return (yp.reshape(n_prompt, seq, d), ys.reshape(n_sample, t_s, d),
            stack("k_p"), stack("v_p"), stack("conv_p"), stack("h_p"),
            stack("k_s"), stack("v_s"), stack("conv_s"), stack("h_s"))
```

```python
import functools
import math

import jax
import jax.numpy as jnp
from jax import lax
from jax.experimental import pallas as pl
from jax.experimental.pallas import tpu as pltpu

F32 = jnp.float32
BF16 = jnp.bfloat16

CHUNK = 64
N_PREV_CHUNKS = 8
HEAD_DIM = 128
LRU_BLOCK = 128
LRU_C = 8.0
NEG_INF = -1e30
EPS = 1e-6
LOG2E = math.log2(math.e)

LANES = 128
SUBLANES = 8
MIB = 1024 * 1024


def _params(semantics, vmem_mib):
    return pltpu.CompilerParams(dimension_semantics=semantics, vmem_limit_bytes=vmem_mib * MIB)


def _sigmoid(x):
    return 0.5 * jnp.tanh(0.5 * x) + 0.5


def _gelu_tanh(x):
    c = math.sqrt(2.0 / math.pi)
    return x * (0.5 * (1.0 + jnp.tanh(c * (x + 0.044715 * (x * x * x)))))


def _group_scale(y, g):
    rows, n = y.shape
    groups = g.shape[0]
    return (y.reshape(groups, rows // groups, n) * g[:, None, :]).reshape(rows, n)


PROMPT_TILE_ROWS = 1024


class _RowGroups:
    def __init__(self, rows_p, rows_s, tile=PROMPT_TILE_ROWS):
        assert rows_p % tile == 0
        self.rows_p, self.rows_s, self.tile = rows_p, rows_s, tile
        self.np_tiles = rows_p // tile
        self.steps = self.np_tiles + 1

    def prompt_spec(self, cols, col_block, rows_per_row=1, output=False):
        first = 0 if output else self.np_tiles - 1
        return pl.BlockSpec((self.tile // rows_per_row, cols),
                            lambda j, i: (jnp.where(i == 0, first, i - 1), col_block(j)))

    def sample_spec(self, cols, col_block, rows_per_row=1, row_block=0, resident=False):
        mode = dict(pipeline_mode=pl.Buffered(1)) if resident else {}
        return pl.BlockSpec((self.rows_s // rows_per_row, cols),
                            lambda j, i: (row_block, col_block(j)), **mode)


def _per_row_group(i, prompt_fn, sample_fn):
    pl.when(i == 0)(sample_fn)
    pl.when(i > 0)(prompt_fn)


def _mod_kernel(c_ref, w_ref, b_ref, o_ref, *, n_sample, n_prompt, groups_per_prompt):
    c = c_ref[...]
    s = (c * _sigmoid(c)).astype(BF16)
    m = jnp.dot(s, w_ref[...].astype(BF16), preferred_element_type=F32) + b_ref[...]
    tn = m.shape[1]
    for b in range(n_prompt):
        row = m[n_sample + b:n_sample + b + 1, :]
        o_ref[b * groups_per_prompt:(b + 1) * groups_per_prompt, :] = jnp.broadcast_to(
            row, (groups_per_prompt, tn))
    o_ref[n_prompt * groups_per_prompt:n_prompt * groups_per_prompt + n_sample, :] = m[:n_sample, :]


def _mod_inputs(c_prompt, c_sample):
    rows = c_sample.shape[0] + c_prompt.shape[0]
    rows_pad = -(-rows // SUBLANES) * SUBLANES
    return jnp.concatenate(
        [c_sample, c_prompt, jnp.zeros((rows_pad - rows, c_prompt.shape[1]), c_prompt.dtype)], axis=0)


def _mod_table(c_all, w_mod, b_mod, n_cols, n_sample, n_prompt, groups_per_prompt):
    rows_pad, d = c_all.shape
    n_groups = n_prompt * groups_per_prompt + n_sample
    tn = 512
    return pl.pallas_call(
        functools.partial(_mod_kernel, n_sample=n_sample, n_prompt=n_prompt,
                          groups_per_prompt=groups_per_prompt),
        out_shape=jax.ShapeDtypeStruct((n_groups, n_cols), F32),
        grid=(n_cols // tn,),
        in_specs=[
            pl.BlockSpec((rows_pad, d), lambda j: (0, 0)),
            pl.BlockSpec((d, tn), lambda j: (0, j)),
            pl.BlockSpec((1, tn), lambda j: (0, j)),
        ],
        out_specs=pl.BlockSpec((n_groups, tn), lambda j: (0, j)),
        compiler_params=_params(("arbitrary",), 40),
        name="mod_table",
    )(c_all, w_mod, b_mod)


def _norm_mod_kernel(x_ref, gain_ref, shift_ref, scale_ref, o_ref):
    x = x_ref[...]
    ms = jnp.mean(x * x, axis=-1, keepdims=True)
    y = x * lax.rsqrt(ms + EPS) * gain_ref[...]
    rows, d = y.shape
    groups = scale_ref.shape[0]
    y3 = y.reshape(groups, rows // groups, d)
    u = y3 * (1.0 + scale_ref[...][:, None, :]) + shift_ref[...][:, None, :]
    o_ref[...] = u.reshape(rows, d).astype(BF16)


def _norm_mod(x, gain, table, shift_col, scale_col, group, group0):
    rows, d = x.shape
    tm = 512
    gt = tm // group
    assert rows % tm == 0 and group0 % gt == 0
    g0 = group0 // gt
    return pl.pallas_call(
        _norm_mod_kernel,
        out_shape=jax.ShapeDtypeStruct((rows, d), BF16),
        grid=(rows // tm,),
        in_specs=[
            pl.BlockSpec((tm, d), lambda i: (i, 0)),
            pl.BlockSpec((1, d), lambda i: (0, 0)),
            pl.BlockSpec((gt, d), lambda i: (g0 + i, shift_col)),
            pl.BlockSpec((gt, d), lambda i: (g0 + i, scale_col)),
        ],
        out_specs=pl.BlockSpec((tm, d), lambda i: (i, 0)),
        compiler_params=_params(("arbitrary",), 40),
        name="norm_mod",
    )(x, gain.reshape(1, d), table, table)


def _row_chunk(rows, granule, max_chunks):
    return next(r for r in range(granule, rows + 1, granule)
                if rows % r == 0 and rows // r <= max_chunks)


def _in_proj_kernel(xp_ref, xs_ref, w_ref, gain_ref, wo_ref, op_ref, os_ref, wob_ref, wb_ref, *,
                    n_norm_tiles, n_wo_chunks):
    j = pl.program_id(0)
    i = pl.program_id(1)

    @pl.when(i == 0)
    def _():
        wb_ref[...] = w_ref[...].astype(BF16)

    @pl.when(j * pl.num_programs(1) + i < n_wo_chunks)
    def _():
        wob_ref[...] = wo_ref[...].astype(BF16)

    def tile(x_ref, o_ref):
        rows, tn = o_ref.shape

        @pl.when(j < n_norm_tiles)
        def _():
            half = rows // 2
            accs = [jnp.dot(x_ref[r0:r0 + half, :], wb_ref[...], preferred_element_type=F32)
                    for r0 in (0, half)]
            for r0, acc in zip((0, half), accs):
                for g in range(tn // HEAD_DIM):
                    sl = slice(g * HEAD_DIM, (g + 1) * HEAD_DIM)
                    zg = acc[:, sl]
                    ms = jnp.mean(zg * zg, axis=-1, keepdims=True)
                    o_ref[r0:r0 + half, sl] = zg * lax.rsqrt(ms + EPS) * gain_ref[:, sl]

        @pl.when(j >= n_norm_tiles)
        def _():
            o_ref[...] = jnp.dot(x_ref[...], wb_ref[...], preferred_element_type=F32)

    _per_row_group(i, lambda: tile(xp_ref, op_ref), lambda: tile(xs_ref, os_ref))


def _in_proj(u_p, u_s, w_in, qk_gain, attn_width, w_out):
    d, n = w_in.shape
    tn = 512
    n_norm_tiles = 2 * attn_width // tn
    rg = _RowGroups(u_p.shape[0], u_s.shape[0])
    ko, no = w_out.shape
    rc = _row_chunk(ko, 16, (n // tn) * rg.steps)
    n_wo_chunks = ko // rc
    wo_spec = pl.BlockSpec((rc, no), lambda j, i: (jnp.minimum(j * rg.steps + i, n_wo_chunks - 1), 0))
    return pl.pallas_call(
        functools.partial(_in_proj_kernel, n_norm_tiles=n_norm_tiles, n_wo_chunks=n_wo_chunks),
        out_shape=(jax.ShapeDtypeStruct((rg.rows_p, n), F32), jax.ShapeDtypeStruct((rg.rows_s, n), F32),
                   jax.ShapeDtypeStruct((ko, no), BF16)),
        grid=(n // tn, rg.steps),
        in_specs=[
            rg.prompt_spec(d, lambda j: 0),
            rg.sample_spec(d, lambda j: 0, resident=True),
            pl.BlockSpec((d, tn), lambda j, i: (0, j)),
            pl.BlockSpec((1, tn), lambda j, i: (0, jnp.minimum(j, n_norm_tiles - 1))),
            wo_spec,
        ],
        out_specs=(rg.prompt_spec(tn, lambda j: j, output=True), rg.sample_spec(tn, lambda j: j),
                   wo_spec),
        scratch_shapes=[pltpu.VMEM((d, tn), BF16)],
        compiler_params=_params(("arbitrary", "arbitrary"), 58),
        name="in_proj",
    )(u_p, u_s, w_in, qk_gain, w_out)


def _bias_ramp(rel_bias, top, length):
    n_heads, n_rel = rel_bias.shape
    clip = (n_rel - 1) // 2
    lead = top - clip
    tail = length - lead - n_rel
    assert lead >= 0 and tail >= 0
    rev = rel_bias[:, ::-1]
    ext = jnp.concatenate([
        jnp.broadcast_to(rel_bias[:, n_rel - 1:], (n_heads, lead)),
        rev,
        jnp.broadcast_to(rel_bias[:, :1], (n_heads, tail)),
    ], axis=1)
    return ext.reshape(n_heads, 1, length)


def _lane_reduce(tiles, combine, reduce):
    parts = [t[:, c:c + LANES] for t in tiles for c in range(0, t.shape[1], LANES)]
    return reduce(functools.reduce(combine, parts), axis=-1, keepdims=True)


def _toeplitz(base, rows):
    w = base.shape[1]
    return pltpu.roll(jnp.broadcast_to(base, (rows, w)), w - (rows - 1), 1, stride=1, stride_axis=0)


def _attn_prompt_bias(ext_ref, bias_ref, tq):
    shift = CHUNK.bit_length() - 1
    rowc = jnp.right_shift(lax.broadcasted_iota(jnp.int32, (tq, tq), 0), shift)
    colc = jnp.right_shift(lax.broadcasted_iota(jnp.int32, (tq, tq), 1), shift)
    for h in range(bias_ref.shape[0]):
        for m in range(3):
            off = (2 - m) * tq
            t = _toeplitz(ext_ref[h, :, off:off + 2 * tq], tq)[:, :tq]
            if m == 0:
                t = jnp.where(colc <= rowc, t, NEG_INF)
            if m == 2:
                t = jnp.where(rowc <= colc, t, NEG_INF)
            bias_ref[h, m] = t * LOG2E


def _attn_prompt_head(h, i, q_ref, k_refs, v_refs, bias_ref, o_ref, scale):
    dn = (((1,), (1,)), ((), ()))
    sl = slice(h * HEAD_DIM, (h + 1) * HEAD_DIM)
    q = q_ref[:, sl].astype(BF16)
    scores = []
    for m, k_ref in zip((2, 1, 0), k_refs):
        s = lax.dot_general(q, k_ref[:, sl].astype(BF16), dn, preferred_element_type=F32)
        s = s * (scale * LOG2E) + bias_ref[h, m]
        if m > 0:
            s = s + jnp.where(i >= m, 0.0, NEG_INF)
        scores.append(s)
    mx = _lane_reduce(scores, jnp.maximum, jnp.max)
    probs = [jnp.exp2(s - mx) for s in scores]
    denom = _lane_reduce(probs, jnp.add, jnp.sum)
    out = functools.reduce(jnp.add, [
        jnp.dot(p.astype(BF16), v_ref[:, sl].astype(BF16), preferred_element_type=F32)
        for p, v_ref in zip(probs, v_refs)])
    o_ref[:, sl] = (out / denom).astype(BF16)


def _attn_sample_kernel(q_ref, kn_ref, vn_ref, ck_ref, cv_ref, ext_ref, o_ref, *, scale, n_heads):
    t = q_ref.shape[0]
    r = ck_ref.shape[1] // n_heads
    dn = (((1,), (1,)), ((), ()))
    for h in range(n_heads):
        sl = slice(h * HEAD_DIM, (h + 1) * HEAD_DIM)
        rows_h = pl.ds(h, r, stride=n_heads)
        bias = _toeplitz(ext_ref[h], t)
        q = q_ref[:, sl].astype(BF16)
        s1 = lax.dot_general(q, ck_ref[0, rows_h, :].astype(BF16), dn, preferred_element_type=F32)
        s1 = s1 * scale + bias[:, :r]
        s2 = lax.dot_general(q, kn_ref[:, sl].astype(BF16), dn, preferred_element_type=F32)
        s2 = s2 * scale + bias[:, r:r + t]
        mx = jnp.maximum(jnp.max(s1, axis=-1, keepdims=True), jnp.max(s2, axis=-1, keepdims=True))
        p1 = jnp.exp(s1 - mx)
        p2 = jnp.exp(s2 - mx)
        denom = jnp.sum(p1, axis=-1, keepdims=True) + jnp.sum(p2, axis=-1, keepdims=True)
        out = (jnp.dot(p1.astype(BF16), cv_ref[0, rows_h, :].astype(BF16), preferred_element_type=F32)
               + jnp.dot(p2.astype(BF16), vn_ref[:, sl].astype(BF16), preferred_element_type=F32))
        o_ref[:, sl] = (out / denom).astype(BF16)


def _attn_sample(z, cache_k, cache_v, rel_bias, row0, n_batch, t, n_heads):
    r = cache_k.shape[1]
    width = n_heads * HEAD_DIM
    ck = cache_k.reshape(n_batch, r * n_heads, HEAD_DIM)
    cv = cache_v.reshape(n_batch, r * n_heads, HEAD_DIM)
    top = r + t - 1
    clip = (rel_bias.shape[1] - 1) // 2
    ramp = -(-max(r + 2 * t, top + clip + 1) // LANES) * LANES
    ext = _bias_ramp(rel_bias, top, ramp)
    rb0 = row0 // t
    return pl.pallas_call(
        functools.partial(_attn_sample_kernel, scale=HEAD_DIM ** -0.5, n_heads=n_heads),
        out_shape=jax.ShapeDtypeStruct((n_batch * t, width), BF16),
        grid=(n_batch,),
        in_specs=[
            pl.BlockSpec((t, width), lambda b: (rb0 + b, 0)),
            pl.BlockSpec((t, width), lambda b: (rb0 + b, 1)),
            pl.BlockSpec((t, width), lambda b: (rb0 + b, 2)),
            pl.BlockSpec((1, r * n_heads, HEAD_DIM), lambda b: (b, 0, 0)),
            pl.BlockSpec((1, r * n_heads, HEAD_DIM), lambda b: (b, 0, 0)),
            pl.BlockSpec((n_heads, 1, ramp), lambda b: (0, 0, 0)),
        ],
        out_specs=pl.BlockSpec((t, width), lambda b: (b, 0)),
        compiler_params=_params(("arbitrary",), 40),
        name="attn_sample",
    )(z, z, z, ck, cv, ext)


def _lru_tile(x, prev, h_in, gate_in, cw, cb, wa, ba, wx, bx, lam, scan_scratch=None):
    t = x.shape[0]
    p = prev.shape[0]
    k = cw.shape[0]
    xcat = jnp.concatenate([prev, x], axis=0)
    xc = cb
    for i in range(k):
        lo = p - (k - 1) + i
        xc = xc + xcat[lo:lo + t] * cw[i:i + 1]
    xcb = xc.astype(BF16)
    rg = _sigmoid(jnp.dot(xcb, wa.astype(BF16), preferred_element_type=F32) + ba)
    ig = _sigmoid(jnp.dot(xcb, wx.astype(BF16), preferred_element_type=F32) + bx)
    neg = -lam
    softplus = jnp.maximum(neg, 0.0) + jnp.log1p(jnp.exp(-jnp.abs(neg)))
    log_a = (-LRU_C) * rg * softplus
    a = jnp.exp(log_a)
    var = -jnp.tanh(log_a) * (a * a + 1.0)
    std = jnp.where(var == 0.0, 0.0, var * lax.rsqrt(var))
    b = std * (ig * xc)
    h = _linear_scan(a, b, h_in, scan_scratch)
    return h * _gelu_tanh(gate_in), h


def _doubling_scan(a, b, axis):
    n = a.shape[axis]
    idx = lax.broadcasted_iota(jnp.int32, a.shape, axis)
    step = 1
    while step < n:
        keep = idx >= step
        a_prev = jnp.where(keep, pltpu.roll(a, step, axis), 1.0)
        b_prev = jnp.where(keep, pltpu.roll(b, step, axis), 0.0)
        b = a * b_prev + b
        a = a * a_prev
        step *= 2
    return a, b


def _linear_scan(a, b, h_in, scratch=None):
    t, c = a.shape
    groups = t // SUBLANES
    a3, b3 = _doubling_scan(a.reshape(groups, SUBLANES, c), b.reshape(groups, SUBLANES, c), 1)
    if scratch is not None:
        ab_ref, hs_ref = scratch
        a2, b2 = a3.reshape(t, c), b3.reshape(t, c)
        ab_ref[0] = a2
        ab_ref[1] = b2
        ends = pl.ds(SUBLANES - 1, groups, stride=SUBLANES)
        a_cum, b_cum = _doubling_scan(ab_ref[0, ends, :], ab_ref[1, ends, :], 0)
        h_end = a_cum * h_in + b_cum
        first = lax.broadcasted_iota(jnp.int32, h_end.shape, 0) == 0
        hs_ref[...] = jnp.where(first, h_in, pltpu.roll(h_end, 1, 0))
        h_start = jnp.concatenate(
            [jnp.broadcast_to(hs_ref[g:g + 1, :], (SUBLANES, c)) for g in range(groups)], axis=0)
        return a2 * h_start + b2
    ae = jnp.broadcast_to(a3[:, SUBLANES - 1:, :], a3.shape).reshape(t, c)
    be = jnp.broadcast_to(b3[:, SUBLANES - 1:, :], b3.shape).reshape(t, c)
    step = SUBLANES
    while step < t:
        be = jnp.concatenate([be[:step], ae[step:] * be[:-step] + be[step:]], axis=0)
        ae = jnp.concatenate([ae[:step], ae[step:] * ae[:-step]], axis=0)
        step *= 2
    h_end = ae * h_in + be
    h_start = jnp.concatenate([jnp.broadcast_to(h_in, (SUBLANES, c)), h_end[:t - SUBLANES]], axis=0)
    return a3.reshape(t, c) * h_start + b3.reshape(t, c)


def _lru_prompt_block(n, x_ref, g_ref, cw_ref, cb_ref, wa_ref, ba_ref, wx_ref, bx_ref, lam_ref,
                      o_ref, conv_ref, h_ref, px_ref, ph_ref, ab_ref, hs_ref):
    t = x_ref.shape[0]
    sl = slice(n * LRU_BLOCK, (n + 1) * LRU_BLOCK)
    x = x_ref[:, sl]
    out, h = _lru_tile(x, px_ref[:, sl], ph_ref[SUBLANES - 1:SUBLANES, sl], g_ref[:, sl],
                       cw_ref[:, sl], cb_ref[:, sl], wa_ref[n], ba_ref[:, sl], wx_ref[n],
                       bx_ref[:, sl], lam_ref[:, sl], (ab_ref.at[n], hs_ref.at[n]))
    o_ref[:, sl] = out.astype(BF16)
    px_ref[:, sl] = x[t - SUBLANES:]
    ph_ref[:, sl] = h[t - SUBLANES:]
    conv_ref[0, :, sl] = x[t - SUBLANES:]
    h_ref[0, :, sl] = h[t - SUBLANES:]


def _mix_prompt_kernel(q_ref, ka_ref, kb_ref, kc_ref, va_ref, vb_ref, vc_ref, ext_ref,
                       x_ref, g_ref, cw_ref, cb_ref, wa_ref, ba_ref, wx_ref, bx_ref, lam_ref,
                       c_ref, wm_ref, bm_ref,
                       attn_ref, lru_ref, conv_ref, h_ref, table_ref,
                       bias_ref, px_ref, ph_ref, ab_ref, hs_ref, *, scale, mod_dims):
    i = pl.program_id(2)

    @pl.when(i == 0)
    def _():
        _attn_prompt_bias(ext_ref, bias_ref, q_ref.shape[0])
        px_ref[...] = jnp.zeros_like(px_ref)
        ph_ref[...] = jnp.zeros_like(ph_ref)

    _mod_kernel(c_ref, wm_ref, bm_ref, table_ref, **mod_dims)

    for n in range(q_ref.shape[1] // HEAD_DIM):
        _attn_prompt_head(n, i, q_ref, (ka_ref, kb_ref, kc_ref), (va_ref, vb_ref, vc_ref),
                          bias_ref, attn_ref, scale)
        _lru_prompt_block(n, x_ref, g_ref, cw_ref, cb_ref, wa_ref, ba_ref, wx_ref, bx_ref, lam_ref,
                          lru_ref, conv_ref, h_ref, px_ref, ph_ref, ab_ref, hs_ref)


def _mix_prompt(z, rel_bias, lw, c_all, w_mod, b_mod, mod_col0, mod_dims,
                n_batch, seq, n_heads, col_x, col_g):
    conv_w, conv_b, w_a, b_a, w_x, b_x, lam = lw
    n_blocks = w_a.shape[0]
    tt = 256
    nb = 4
    lanes = nb * HEAD_DIM
    assert HEAD_DIM == LRU_BLOCK and n_heads == n_blocks and n_heads % nb == 0
    assert tt == (N_PREV_CHUNKS * CHUNK) // 2 and seq % tt == 0
    assert col_x % lanes == 0 and col_g % lanes == 0
    nt = seq // tt
    ng = n_heads // nb
    width = n_heads * HEAD_DIM
    ext = _bias_ramp(rel_bias, 3 * tt - 1, 4 * tt)
    cx, cg = col_x // lanes, col_g // lanes
    vec = lambda a: a.reshape(1, width)

    steps = n_batch * ng * nt
    d_model, mod_cols = w_mod.shape
    mtn = (mod_cols - mod_col0) // steps
    assert mtn % LANES == 0 and mtn * steps == mod_cols - mod_col0 and mod_col0 % mtn == 0
    n_groups = mod_dims["n_prompt"] * mod_dims["groups_per_prompt"] + mod_dims["n_sample"]
    step = lambda b, g, t: (b * ng + g) * nt + t

    def rows(back, col0):
        return pl.BlockSpec((tt, lanes),
                            lambda b, g, t: (b * nt + jnp.maximum(t - back, 0), col0 + g))

    vspec = pl.BlockSpec((1, lanes), lambda b, g, t: (0, g))
    wspec = pl.BlockSpec((nb, LRU_BLOCK, LRU_BLOCK), lambda b, g, t: (g, 0, 0))
    sspec = pl.BlockSpec((1, SUBLANES, lanes), lambda b, g, t: (b, 0, g))
    return pl.pallas_call(
        functools.partial(_mix_prompt_kernel, scale=HEAD_DIM ** -0.5, mod_dims=mod_dims),
        out_shape=(
            jax.ShapeDtypeStruct((n_batch * seq, width), BF16),
            jax.ShapeDtypeStruct((n_batch * seq, width), BF16),
            jax.ShapeDtypeStruct((n_batch, SUBLANES, width), F32),
            jax.ShapeDtypeStruct((n_batch, SUBLANES, width), F32),
            jax.ShapeDtypeStruct((n_groups, mod_cols - mod_col0), F32),
        ),
        grid=(n_batch, ng, nt),
        in_specs=[
            rows(0, 0),
            rows(2, ng), rows(1, ng), rows(0, ng),
            rows(2, 2 * ng), rows(1, 2 * ng), rows(0, 2 * ng),
            pl.BlockSpec((nb, 1, 4 * tt), lambda b, g, t: (g, 0, 0)),
            rows(0, cx), rows(0, cg),
            pl.BlockSpec((conv_w.shape[0], lanes), lambda b, g, t: (0, g)),
            vspec, wspec, vspec, wspec, vspec, vspec,
            pl.BlockSpec(c_all.shape, lambda b, g, t: (0, 0)),
            pl.BlockSpec((d_model, mtn), lambda b, g, t: (0, mod_col0 // mtn + step(b, g, t))),
            pl.BlockSpec((1, mtn), lambda b, g, t: (0, mod_col0 // mtn + step(b, g, t))),
        ],
        out_specs=(rows(0, 0), rows(0, 0), sspec, sspec,
                   pl.BlockSpec((n_groups, mtn), lambda b, g, t: (0, step(b, g, t)))),
        scratch_shapes=[pltpu.VMEM((nb, 3, tt, tt), F32),
                        pltpu.VMEM((SUBLANES, lanes), F32), pltpu.VMEM((SUBLANES, lanes), F32),
                        pltpu.VMEM((nb, 2, tt, LRU_BLOCK), F32),
                        pltpu.VMEM((nb, tt // SUBLANES, LRU_BLOCK), F32)],
        compiler_params=_params(("arbitrary", "arbitrary", "arbitrary"), 40),
        name="mix_prompt",
    )(z, z, z, z, z, z, z, ext, z, z, conv_w, vec(conv_b), w_a, vec(b_a), w_x, vec(b_x), vec(lam),
      c_all, w_mod, b_mod)


def _lru_sample_kernel(x_ref, g_ref, prev_ref, h0_ref, cw_ref, cb_ref, wa_ref, ba_ref, wx_ref,
                       bx_ref, lam_ref, o_ref, conv_ref, h_ref, *, n_blocks):
    t = x_ref.shape[0]
    for n in range(n_blocks):
        sl = slice(n * LRU_BLOCK, (n + 1) * LRU_BLOCK)
        x = x_ref[:, sl]
        out, h = _lru_tile(x, prev_ref[0, :, sl], h0_ref[0, :, sl], g_ref[:, sl],
                           cw_ref[:, sl], cb_ref[:, sl], wa_ref[n], ba_ref[:, sl], wx_ref[n],
                           bx_ref[:, sl], lam_ref[:, sl])
        o_ref[:, sl] = out.astype(BF16)
        conv_ref[0, :, sl] = x[t - SUBLANES:]
        h_ref[0, :, sl] = h[t - SUBLANES:]


def _lru_sample(z, state_conv, state_h, lw, row0, n_batch, t, col_x, col_g):
    conv_w, conv_b, w_a, b_a, w_x, b_x, lam = lw
    n_blocks = w_a.shape[0]
    width = n_blocks * LRU_BLOCK
    k = conv_w.shape[0]
    assert t >= SUBLANES and k - 1 <= SUBLANES
    prev = jnp.pad(state_conv, ((0, 0), (SUBLANES - (k - 1), 0), (0, 0)))
    h0 = state_h.reshape(n_batch, 1, width)
    rb0 = row0 // t
    cx, cg = col_x // width, col_g // width
    vec = lambda a: a.reshape(1, width)
    vspec = pl.BlockSpec((1, width), lambda b: (0, 0))
    wspec = pl.BlockSpec((n_blocks, LRU_BLOCK, LRU_BLOCK), lambda b: (0, 0, 0))
    sspec = pl.BlockSpec((1, SUBLANES, width), lambda b: (b, 0, 0))
    return pl.pallas_call(
        functools.partial(_lru_sample_kernel, n_blocks=n_blocks),
        out_shape=(
            jax.ShapeDtypeStruct((n_batch * t, width), BF16),
            jax.ShapeDtypeStruct((n_batch, SUBLANES, width), F32),
            jax.ShapeDtypeStruct((n_batch, SUBLANES, width), F32),
        ),
        grid=(n_batch,),
        in_specs=[
            pl.BlockSpec((t, width), lambda b: (rb0 + b, cx)),
            pl.BlockSpec((t, width), lambda b: (rb0 + b, cg)),
            sspec,
            pl.BlockSpec((1, 1, width), lambda b: (b, 0, 0)),
            pl.BlockSpec((k, width), lambda b: (0, 0)),
            vspec, wspec, vspec, wspec, vspec, vspec,
        ],
        out_specs=(pl.BlockSpec((t, width), lambda b: (b, 0)), sspec, sspec),
        compiler_params=_params(("arbitrary",), 32),
        name="lru_sample",
    )(z, z, prev, h0, conv_w, vec(conv_b), w_a, vec(b_a), w_x, vec(b_x), vec(lam))


def _out_proj_kernel(ap_ref, lp_ref, as_ref, ls_ref, w_ref, xp_ref, xs_ref, gp_ref, gs_ref,
                     op_ref, os_ref):
    i = pl.program_id(1)
    ka = ap_ref.shape[1]

    def tile(a_ref, l_ref, x_ref, g_ref, o_ref):
        acc = (jnp.dot(a_ref[...], w_ref[:ka, :], preferred_element_type=F32)
               + jnp.dot(l_ref[...], w_ref[ka:, :], preferred_element_type=F32))
        o_ref[...] = x_ref[...] + _group_scale(acc, g_ref[...])

    _per_row_group(i,
                   lambda: tile(ap_ref, lp_ref, xp_ref, gp_ref, op_ref),
                   lambda: tile(as_ref, ls_ref, xs_ref, gs_ref, os_ref))


def _out_proj(attn_p, lru_p, attn_s, lru_s, w_out, x_p, x_s, table, gate_col, group):
    d, n = w_out.shape
    ka, kl = attn_p.shape[1], lru_p.shape[1]
    assert ka + kl == d and w_out.dtype == BF16
    tn = min(1024, n)
    rg = _RowGroups(x_p.shape[0], x_s.shape[0], tile=512)
    gcol = lambda j: gate_col * (n // tn) + j
    sample_table_block = (rg.rows_p // group) // (rg.rows_s // group)
    return pl.pallas_call(
        _out_proj_kernel,
        out_shape=(jax.ShapeDtypeStruct((rg.rows_p, n), F32), jax.ShapeDtypeStruct((rg.rows_s, n), F32)),
        grid=(n // tn, rg.steps),
        in_specs=[
            rg.prompt_spec(ka, lambda j: 0),
            rg.prompt_spec(kl, lambda j: 0),
            rg.sample_spec(ka, lambda j: 0, resident=True),
            rg.sample_spec(kl, lambda j: 0, resident=True),
            pl.BlockSpec((d, tn), lambda j, i: (0, j)),
            rg.prompt_spec(tn, lambda j: j),
            rg.sample_spec(tn, lambda j: j),
            rg.prompt_spec(tn, gcol, rows_per_row=group),
            rg.sample_spec(tn, gcol, rows_per_row=group, row_block=sample_table_block),
        ],
        out_specs=(rg.prompt_spec(tn, lambda j: j, output=True), rg.sample_spec(tn, lambda j: j)),
        compiler_params=_params(("arbitrary", "arbitrary"), 56),
        name="out_proj",
    )(attn_p, lru_p, attn_s, lru_s, w_out, x_p, x_s, table, table)


def _ffn_up_kernel(xp_ref, xs_ref, wg_hbm, wu_hbm, wd_ref, op_ref, os_ref, wdb_ref,
                   stage_ref, wb_ref, sem, *, n_cols, n_wd_chunks):
    j = pl.program_id(0)
    i = pl.program_id(1)
    tn = wb_ref.shape[2]
    n_full, rem = divmod(n_cols, tn)

    rc = wd_ref.shape[0]
    chunk = j * pl.num_programs(1) + i

    @pl.when(chunk < n_wd_chunks)
    def _():
        row = lax.broadcasted_iota(jnp.int32, wd_ref.shape, 0)
        wdb_ref[...] = jnp.where(chunk * rc + row < n_cols, wd_ref[...], 0.0).astype(BF16)

    def copies(jj, width):
        col = pl.multiple_of(jj * tn, tn)
        return [pltpu.make_async_copy(src.at[:, pl.ds(col, width)],
                                      stage_ref.at[w, :, pl.ds(0, width)], sem.at[w])
                for w, src in enumerate((wg_hbm, wu_hbm))]

    def start(jj):
        @pl.when(jj < n_full)
        def _():
            for c in copies(jj, tn):
                c.start()

        if rem:
            @pl.when(jj == n_full)
            def _():
                for c in copies(jj, rem):
                    c.start()

    @pl.when((j == 0) & (i == 0))
    def _():
        start(j)

    @pl.when(i == 0)
    def _():
        @pl.when(j < n_full)
        def _():
            for c in copies(j, tn):
                c.wait()
            wb_ref[...] = stage_ref[...].astype(BF16)

        if rem:
            @pl.when(j == n_full)
            def _():
                for c in copies(j, rem):
                    c.wait()
                wb_ref[:, :, :rem] = stage_ref[:, :, :rem].astype(BF16)
                wb_ref[:, :, rem:] = jnp.zeros((2, wb_ref.shape[1], tn - rem), BF16)

        @pl.when(j >= n_full + (1 if rem else 0))
        def _():
            wb_ref[...] = jnp.zeros_like(wb_ref)

    @pl.when((i == 1) & (j + 1 < pl.num_programs(0)))
    def _():
        start(j + 1)

    def tile(x_ref, o_ref):
        x = x_ref[...]
        g = jnp.dot(x, wb_ref[0], preferred_element_type=F32)
        u = jnp.dot(x, wb_ref[1], preferred_element_type=F32)
        o_ref[...] = (g * _sigmoid(g) * u).astype(BF16)

    _per_row_group(i, lambda: tile(xp_ref, op_ref), lambda: tile(xs_ref, os_ref))


def _ffn_up(u_p, u_s, w_gate, w_up, w_down, ff_pad):
    d, ff = w_gate.shape
    n_out = w_down.shape[1]
    tn = 512
    assert ff % LANES == 0 and ff_pad % tn == 0
    rg = _RowGroups(u_p.shape[0], u_s.shape[0])
    assert rg.steps >= 2
    steps = (ff_pad // tn) * rg.steps
    rc = next(r for r in range(16, ff_pad + 1, 16)
              if ff_pad % r == 0 and ff % r == 0 and ff_pad // r <= steps)
    n_wd_chunks, n_real_chunks = ff_pad // rc, ff // rc
    chunk_of = lambda j, i: j * rg.steps + i
    hbm = pl.BlockSpec(memory_space=pl.ANY)
    return pl.pallas_call(
        functools.partial(_ffn_up_kernel, n_cols=ff, n_wd_chunks=n_wd_chunks),
        out_shape=(jax.ShapeDtypeStruct((rg.rows_p, ff_pad), BF16),
                   jax.ShapeDtypeStruct((rg.rows_s, ff_pad), BF16),
                   jax.ShapeDtypeStruct((ff_pad, n_out), BF16)),
        grid=(ff_pad // tn, rg.steps),
        in_specs=[rg.prompt_spec(d, lambda j: 0), rg.sample_spec(d, lambda j: 0, resident=True),
                  hbm, hbm,
                  pl.BlockSpec((rc, n_out),
                               lambda j, i: (jnp.minimum(chunk_of(j, i), n_real_chunks - 1), 0))],
        out_specs=(rg.prompt_spec(tn, lambda j: j, output=True), rg.sample_spec(tn, lambda j: j),
                   pl.BlockSpec((rc, n_out),
                                lambda j, i: (jnp.minimum(chunk_of(j, i), n_wd_chunks - 1), 0))),
        scratch_shapes=[pltpu.VMEM((2, d, tn), F32), pltpu.VMEM((2, d, tn), BF16),
                        pltpu.SemaphoreType.DMA((2,))],
        compiler_params=_params(("arbitrary", "arbitrary"), 62),
        name="ffn_up",
    )(u_p, u_s, w_gate, w_up, w_down)


def _ffn_down_kernel(h_ref, w_ref, x_ref, g_ref, o_ref):
    k = pl.program_id(2)

    @pl.when(k == 0)
    def _():
        o_ref[...] = jnp.zeros_like(o_ref)

    tm = o_ref.shape[0]
    rows = min(tm, 512)
    for r0 in range(0, tm, rows):
        o_ref[r0:r0 + rows, :] += jnp.dot(h_ref[r0:r0 + rows, :], w_ref[...],
                                          preferred_element_type=F32)

    @pl.when(k == pl.num_programs(2) - 1)
    def _():
        o_ref[...] = x_ref[...] + _group_scale(o_ref[...], g_ref[...])


def _ffn_down(h, w_down, x1, table, gate_col, group, group0, tm, tn):
    rows, ff_pad = h.shape
    n = w_down.shape[1]
    tk = 1024
    tn = min(tn, n)
    gt = tm // group
    assert w_down.shape[0] == ff_pad and w_down.dtype == BF16
    assert rows % tm == 0 and ff_pad % tk == 0 and n % tn == 0 and group0 % gt == 0
    g0 = group0 // gt
    gcol0 = gate_col * (n // tn)
    return pl.pallas_call(
        _ffn_down_kernel,
        out_shape=jax.ShapeDtypeStruct((rows, n), F32),
        grid=(rows // tm, n // tn, ff_pad // tk),
        in_specs=[
            pl.BlockSpec((tm, tk), lambda i, j, k: (i, k)),
            pl.BlockSpec((tk, tn), lambda i, j, k: (k, j)),
            pl.BlockSpec((tm, tn), lambda i, j, k: (i, j)),
            pl.BlockSpec((gt, tn), lambda i, j, k: (g0 + i, gcol0 + j)),
        ],
        out_specs=pl.BlockSpec((tm, tn), lambda i, j, k: (i, j)),
        compiler_params=_params(("arbitrary", "arbitrary", "arbitrary"), 58),
        name="ffn_down",
    )(h, w_down, x1, table)


def _layer(x_p, x_s, c_prompt, c_sample, cache_k, cache_v, state_conv, state_h, lw, dims):
    (norm_mix_g, norm_ffn_g, w_mod, b_mod, w_in, q_norm_g, k_norm_g, rel_bias,
     conv_w, conv_b, w_rg_a, b_rg_a, w_rg_x, b_rg_x, lru_lambda, w_out,
     w_ffn_gate, w_ffn_up, w_ffn_down) = lw
    n_prompt, seq, n_sample, t_s = dims
    group = t_s
    n_heads = rel_bias.shape[0]
    attn_w = n_heads * HEAD_DIM
    lru_w = w_rg_a.shape[0] * LRU_BLOCK
    rows_p = n_prompt * seq
    rows_s = n_sample * t_s
    groups_p = rows_p // group

    d = x_p.shape[1]
    c_all = _mod_inputs(c_prompt, c_sample)
    b_mod2 = b_mod.reshape(1, -1)
    mod_dims = dict(n_sample=n_sample, n_prompt=n_prompt, groups_per_prompt=seq // group)
    table1 = _mod_table(c_all, w_mod, b_mod2, 2 * d, **mod_dims)
    u_p = _norm_mod(x_p, norm_mix_g, table1, 0, 1, group, 0)
    u_s = _norm_mod(x_s, norm_mix_g, table1, 0, 1, group, groups_p)
    qk_gain = jnp.concatenate([jnp.tile(q_norm_g, n_heads), jnp.tile(k_norm_g, n_heads)]).reshape(1, -1)
    z_p, z_s, w_out_b = _in_proj(u_p, u_s, w_in, qk_gain, attn_w, w_out)

    lru_params = (conv_w, conv_b, w_rg_a, b_rg_a, w_rg_x, b_rg_x, lru_lambda)
    col_x, col_g = 3 * attn_w, 3 * attn_w + lru_w
    attn_p, lru_p, conv_p, h_p, table2 = _mix_prompt(
        z_p, rel_bias, lru_params, c_all, w_mod, b_mod2, 2 * d, mod_dims,
        n_prompt, seq, n_heads, col_x, col_g)
    attn_s = _attn_sample(z_s, cache_k, cache_v, rel_bias, 0, n_sample, t_s, n_heads)
    lru_s, conv_s, h_s = _lru_sample(z_s, state_conv, state_h, lru_params, 0, n_sample, t_s,
                                     col_x, col_g)

    x1_p, x1_s = _out_proj(attn_p, lru_p, attn_s, lru_s, w_out_b, x_p, x_s, table2, 0, group)
    u2_p = _norm_mod(x1_p, norm_ffn_g, table2, 1, 2, group, 0)
    u2_s = _norm_mod(x1_s, norm_ffn_g, table2, 1, 2, group, groups_p)
    ff = w_ffn_gate.shape[1]
    ff_pad = -(-ff // 1024) * 1024
    hid_p, hid_s, w_down_b = _ffn_up(u2_p, u2_s, w_ffn_gate, w_ffn_up, w_ffn_down, ff_pad)
    y_p = _ffn_down(hid_p, w_down_b, x1_p, table2, 3, group, 0, 2048, 1024)
    y_s = _ffn_down(hid_s, w_down_b, x1_s, table2, 3, group, groups_p, rows_s, 2048)

    keep = min(N_PREV_CHUNKS * CHUNK, seq)
    kv_p = z_p.reshape(n_prompt, seq, -1)[:, seq - keep:, attn_w:3 * attn_w]
    kv_p = kv_p.reshape(n_prompt, keep, 2, n_heads, HEAD_DIM)
    kv_s = z_s[:, attn_w:3 * attn_w].reshape(n_sample, t_s, 2, n_heads, HEAD_DIM)
    k_conv = conv_w.shape[0] - 1
    state = dict(
        k_p=kv_p[:, :, 0], v_p=kv_p[:, :, 1], conv_p=conv_p[:, SUBLANES - k_conv:], h_p=h_p[:, SUBLANES - 1],
        k_s=kv_s[:, :, 0], v_s=kv_s[:, :, 1], conv_s=conv_s[:, SUBLANES - k_conv:], h_s=h_s[:, SUBLANES - 1])
    return y_p, y_s, state


def kernel(x_prompt, x_sample, cache_k, cache_v, state_conv, state_h, c_prompt, c_sample, norm_mix_g, norm_ffn_g, w_mod, b_mod, w_in, q_norm_g, k_norm_g, rel_bias, conv_w, conv_b, w_rg_a, b_rg_a, w_rg_x, b_rg_x, lru_lambda, w_out, w_ffn_gate, w_ffn_up, w_ffn_down):
    n_prompt, seq, d = x_prompt.shape
    n_sample, t_s, _ = x_sample.shape
    depth = w_in.shape[0]
    dims = (n_prompt, seq, n_sample, t_s)
    yp = x_prompt.reshape(n_prompt * seq, d)
    ys = x_sample.reshape(n_sample * t_s, d)
    states = []
    for l in range(depth):
        lw = (norm_mix_g[l], norm_ffn_g[l], w_mod[l], b_mod[l], w_in[l], q_norm_g[l], k_norm_g[l],
              rel_bias[l], conv_w[l], conv_b[l], w_rg_a[l], b_rg_a[l], w_rg_x[l], b_rg_x[l],
              lru_lambda[l], w_out[l], w_ffn_gate[l], w_ffn_up[l], w_ffn_down[l])
        yp, ys, st = _layer(yp, ys, c_prompt, c_sample, cache_k[l], cache_v[l], state_conv[l],
                            state_h[l], lw, dims)
        states.append(st)
    stack = lambda name: jnp.stack([s[name] for s in states])
    return (yp.reshape(n_prompt, seq, d), ys.reshape(n_sample, t_s, d),
            stack("k_p"), stack("v_p"), stack("conv_p"), stack("h_p"),
            stack("k_s"), stack("v_s"), stack("conv_s"), stack("h_s"))
```

```python
import functools
import math

import jax
import jax.numpy as jnp
from jax import lax
from jax.experimental import pallas as pl
from jax.experimental.pallas import tpu as pltpu

F32 = jnp.float32
BF16 = jnp.bfloat16

CHUNK = 64
N_PREV_CHUNKS = 8
HEAD_DIM = 128
LRU_BLOCK = 128
LRU_C = 8.0
NEG_INF = -1e30
EPS = 1e-6
LOG2E = math.log2(math.e)

LANES = 128
SUBLANES = 8
MIB = 1024 * 1024


def _params(semantics, vmem_mib):
    return pltpu.CompilerParams(dimension_semantics=semantics, vmem_limit_bytes=vmem_mib * MIB)


def _sigmoid(x):
    return 0.5 * jnp.tanh(0.5 * x) + 0.5


def _gelu_tanh(x):
    c = math.sqrt(2.0 / math.pi)
    return x * (0.5 * (1.0 + jnp.tanh(c * (x + 0.044715 * (x * x * x)))))


def _group_scale(y, g):
    rows, n = y.shape
    groups = g.shape[0]
    return (y.reshape(groups, rows // groups, n) * g[:, None, :]).reshape(rows, n)


PROMPT_TILE_ROWS = 1024


class _RowGroups:
    def __init__(self, rows_p, rows_s, tile=PROMPT_TILE_ROWS):
        assert rows_p % tile == 0
        self.rows_p, self.rows_s, self.tile = rows_p, rows_s, tile
        self.np_tiles = rows_p // tile
        self.steps = self.np_tiles + 1

    def prompt_spec(self, cols, col_block, rows_per_row=1, output=False):
        first = 0 if output else self.np_tiles - 1
        return pl.BlockSpec((self.tile // rows_per_row, cols),
                            lambda j, i: (jnp.where(i == 0, first, i - 1), col_block(j)))

    def sample_spec(self, cols, col_block, rows_per_row=1, row_block=0, resident=False):
        mode = dict(pipeline_mode=pl.Buffered(1)) if resident else {}
        return pl.BlockSpec((self.rows_s // rows_per_row, cols),
                            lambda j, i: (row_block, col_block(j)), **mode)


def _per_row_group(i, prompt_fn, sample_fn):
    pl.when(i == 0)(sample_fn)
    pl.when(i > 0)(prompt_fn)


def _mod_kernel(c_ref, w_ref, b_ref, o_ref, *, n_sample, n_prompt, groups_per_prompt):
    c = c_ref[...]
    s = (c * _sigmoid(c)).astype(BF16)
    m = jnp.dot(s, w_ref[...].astype(BF16), preferred_element_type=F32) + b_ref[...]
    tn = m.shape[1]
    for b in range(n_prompt):
        row = m[n_sample + b:n_sample + b + 1, :]
        o_ref[b * groups_per_prompt:(b + 1) * groups_per_prompt, :] = jnp.broadcast_to(
            row, (groups_per_prompt, tn))
    o_ref[n_prompt * groups_per_prompt:n_prompt * groups_per_prompt + n_sample, :] = m[:n_sample, :]


def _mod_inputs(c_prompt, c_sample):
    rows = c_sample.shape[0] + c_prompt.shape[0]
    rows_pad = -(-rows // SUBLANES) * SUBLANES
    return jnp.concatenate(
        [c_sample, c_prompt, jnp.zeros((rows_pad - rows, c_prompt.shape[1]), c_prompt.dtype)], axis=0)


def _mod_table(c_all, w_mod, b_mod, n_cols, n_sample, n_prompt, groups_per_prompt):
    rows_pad, d = c_all.shape
    n_groups = n_prompt * groups_per_prompt + n_sample
    tn = 512
    return pl.pallas_call(
        functools.partial(_mod_kernel, n_sample=n_sample, n_prompt=n_prompt,
                          groups_per_prompt=groups_per_prompt),
        out_shape=jax.ShapeDtypeStruct((n_groups, n_cols), F32),
        grid=(n_cols // tn,),
        in_specs=[
            pl.BlockSpec((rows_pad, d), lambda j: (0, 0)),
            pl.BlockSpec((d, tn), lambda j: (0, j)),
            pl.BlockSpec((1, tn), lambda j: (0, j)),
        ],
        out_specs=pl.BlockSpec((n_groups, tn), lambda j: (0, j)),
        compiler_params=_params(("arbitrary",), 40),
        name="mod_table",
    )(c_all, w_mod, b_mod)


def _norm_mod_kernel(x_ref, gain_ref, shift_ref, scale_ref, o_ref):
    x = x_ref[...]
    ms = jnp.mean(x * x, axis=-1, keepdims=True)
    y = x * lax.rsqrt(ms + EPS) * gain_ref[...]
    rows, d = y.shape
    groups = scale_ref.shape[0]
    y3 = y.reshape(groups, rows // groups, d)
    u = y3 * (1.0 + scale_ref[...][:, None, :]) + shift_ref[...][:, None, :]
    o_ref[...] = u.reshape(rows, d).astype(BF16)


def _norm_mod(x, gain, table, shift_col, scale_col, group, group0):
    rows, d = x.shape
    tm = 512
    gt = tm // group
    assert rows % tm == 0 and group0 % gt == 0
    g0 = group0 // gt
    return pl.pallas_call(
        _norm_mod_kernel,
        out_shape=jax.ShapeDtypeStruct((rows, d), BF16),
        grid=(rows // tm,),
        in_specs=[
            pl.BlockSpec((tm, d), lambda i: (i, 0)),
            pl.BlockSpec((1, d), lambda i: (0, 0)),
            pl.BlockSpec((gt, d), lambda i: (g0 + i, shift_col)),
            pl.BlockSpec((gt, d), lambda i: (g0 + i, scale_col)),
        ],
        out_specs=pl.BlockSpec((tm, d), lambda i: (i, 0)),
        compiler_params=_params(("arbitrary",), 40),
        name="norm_mod",
    )(x, gain.reshape(1, d), table, table)


def _row_chunk(rows, granule, max_chunks):
    return next(r for r in range(granule, rows + 1, granule)
                if rows % r == 0 and rows // r <= max_chunks)


def _in_proj_kernel(xp_ref, xs_ref, w_ref, gain_ref, wo_ref, op_ref, os_ref, wob_ref, wb_ref, *,
                    n_norm_tiles, n_wo_chunks):
    j = pl.program_id(0)
    i = pl.program_id(1)

    @pl.when(i == 0)
    def _():
        wb_ref[...] = w_ref[...].astype(BF16)

    @pl.when(j * pl.num_programs(1) + i < n_wo_chunks)
    def _():
        wob_ref[...] = wo_ref[...].astype(BF16)

    def tile(x_ref, o_ref):
        rows, tn = o_ref.shape

        @pl.when(j < n_norm_tiles)
        def _():
            half = rows // 2
            accs = [jnp.dot(x_ref[r0:r0 + half, :], wb_ref[...], preferred_element_type=F32)
                    for r0 in (0, half)]
            for r0, acc in zip((0, half), accs):
                for g in range(tn // HEAD_DIM):
                    sl = slice(g * HEAD_DIM, (g + 1) * HEAD_DIM)
                    zg = acc[:, sl]
                    ms = jnp.mean(zg * zg, axis=-1, keepdims=True)
                    o_ref[r0:r0 + half, sl] = zg * lax.rsqrt(ms + EPS) * gain_ref[:, sl]

        @pl.when(j >= n_norm_tiles)
        def _():
            o_ref[...] = jnp.dot(x_ref[...], wb_ref[...], preferred_element_type=F32)

    _per_row_group(i, lambda: tile(xp_ref, op_ref), lambda: tile(xs_ref, os_ref))


def _in_proj(u_p, u_s, w_in, qk_gain, attn_width, w_out):
    d, n = w_in.shape
    tn = 512
    n_norm_tiles = 2 * attn_width // tn
    rg = _RowGroups(u_p.shape[0], u_s.shape[0])
    ko, no = w_out.shape
    rc = _row_chunk(ko, 16, (n // tn) * rg.steps)
    n_wo_chunks = ko // rc
    wo_spec = pl.BlockSpec((rc, no), lambda j, i: (jnp.minimum(j * rg.steps + i, n_wo_chunks - 1), 0))
    return pl.pallas_call(
        functools.partial(_in_proj_kernel, n_norm_tiles=n_norm_tiles, n_wo_chunks=n_wo_chunks),
        out_shape=(jax.ShapeDtypeStruct((rg.rows_p, n), F32), jax.ShapeDtypeStruct((rg.rows_s, n), F32),
                   jax.ShapeDtypeStruct((ko, no), BF16)),
        grid=(n // tn, rg.steps),
        in_specs=[
            rg.prompt_spec(d, lambda j: 0),
            rg.sample_spec(d, lambda j: 0, resident=True),
            pl.BlockSpec((d, tn), lambda j, i: (0, j)),
            pl.BlockSpec((1, tn), lambda j, i: (0, jnp.minimum(j, n_norm_tiles - 1))),
            wo_spec,
        ],
        out_specs=(rg.prompt_spec(tn, lambda j: j, output=True), rg.sample_spec(tn, lambda j: j),
                   wo_spec),
        scratch_shapes=[pltpu.VMEM((d, tn), BF16)],
        compiler_params=_params(("arbitrary", "arbitrary"), 58),
        name="in_proj",
    )(u_p, u_s, w_in, qk_gain, w_out)


def _bias_ramp(rel_bias, top, length):
    n_heads, n_rel = rel_bias.shape
    clip = (n_rel - 1) // 2
    lead = top - clip
    tail = length - lead - n_rel
    assert lead >= 0 and tail >= 0
    rev = rel_bias[:, ::-1]
    ext = jnp.concatenate([
        jnp.broadcast_to(rel_bias[:, n_rel - 1:], (n_heads, lead)),
        rev,
        jnp.broadcast_to(rel_bias[:, :1], (n_heads, tail)),
    ], axis=1)
    return ext.reshape(n_heads, 1, length)


def _lane_reduce(tiles, combine, reduce):
    parts = [t[:, c:c + LANES] for t in tiles for c in range(0, t.shape[1], LANES)]
    return reduce(functools.reduce(combine, parts), axis=-1, keepdims=True)


def _toeplitz(base, rows):
    w = base.shape[1]
    return pltpu.roll(jnp.broadcast_to(base, (rows, w)), w - (rows - 1), 1, stride=1, stride_axis=0)


def _attn_prompt_bias(ext_ref, bias_ref, tq):
    shift = CHUNK.bit_length() - 1
    rowc = jnp.right_shift(lax.broadcasted_iota(jnp.int32, (tq, tq), 0), shift)
    colc = jnp.right_shift(lax.broadcasted_iota(jnp.int32, (tq, tq), 1), shift)
    for h in range(bias_ref.shape[0]):
        for m in range(3):
            off = (2 - m) * tq
            t = _toeplitz(ext_ref[h, :, off:off + 2 * tq], tq)[:, :tq]
            if m == 0:
                t = jnp.where(colc <= rowc, t, NEG_INF)
            if m == 2:
                t = jnp.where(rowc <= colc, t, NEG_INF)
            bias_ref[h, m] = t * LOG2E


def _attn_prompt_head(h, i, q_ref, k_refs, v_refs, bias_ref, o_ref, scale):
    dn = (((1,), (1,)), ((), ()))
    sl = slice(h * HEAD_DIM, (h + 1) * HEAD_DIM)
    q = q_ref[:, sl].astype(BF16)
    scores = []
    for m, k_ref in zip((2, 1, 0), k_refs):
        s = lax.dot_general(q, k_ref[:, sl].astype(BF16), dn, preferred_element_type=F32)
        s = s * (scale * LOG2E) + bias_ref[h, m]
        if m > 0:
            s = s + jnp.where(i >= m, 0.0, NEG_INF)
        scores.append(s)
    mx = _lane_reduce(scores, jnp.maximum, jnp.max)
    probs = [jnp.exp2(s - mx) for s in scores]
    denom = _lane_reduce(probs, jnp.add, jnp.sum)
    out = functools.reduce(jnp.add, [
        jnp.dot(p.astype(BF16), v_ref[:, sl].astype(BF16), preferred_element_type=F32)
        for p, v_ref in zip(probs, v_refs)])
    o_ref[:, sl] = (out / denom).astype(BF16)


def _attn_sample_kernel(q_ref, kn_ref, vn_ref, ck_ref, cv_ref, ext_ref, o_ref, *, scale, n_heads):
    t = q_ref.shape[0]
    r = ck_ref.shape[1] // n_heads
    dn = (((1,), (1,)), ((), ()))
    for h in range(n_heads):
        sl = slice(h * HEAD_DIM, (h + 1) * HEAD_DIM)
        rows_h = pl.ds(h, r, stride=n_heads)
        bias = _toeplitz(ext_ref[h], t)
        q = q_ref[:, sl].astype(BF16)
        s1 = lax.dot_general(q, ck_ref[0, rows_h, :].astype(BF16), dn, preferred_element_type=F32)
        s1 = s1 * scale + bias[:, :r]
        s2 = lax.dot_general(q, kn_ref[:, sl].astype(BF16), dn, preferred_element_type=F32)
        s2 = s2 * scale + bias[:, r:r + t]
        mx = jnp.maximum(jnp.max(s1, axis=-1, keepdims=True), jnp.max(s2, axis=-1, keepdims=True))
        p1 = jnp.exp(s1 - mx)
        p2 = jnp.exp(s2 - mx)
        denom = jnp.sum(p1, axis=-1, keepdims=True) + jnp.sum(p2, axis=-1, keepdims=True)
        out = (jnp.dot(p1.astype(BF16), cv_ref[0, rows_h, :].astype(BF16), preferred_element_type=F32)
               + jnp.dot(p2.astype(BF16), vn_ref[:, sl].astype(BF16), preferred_element_type=F32))
        o_ref[:, sl] = (out / denom).astype(BF16)


def _attn_sample(z, cache_k, cache_v, rel_bias, row0, n_batch, t, n_heads):
    r = cache_k.shape[1]
    width = n_heads * HEAD_DIM
    ck = cache_k.reshape(n_batch, r * n_heads, HEAD_DIM)
    cv = cache_v.reshape(n_batch, r * n_heads, HEAD_DIM)
    top = r + t - 1
    clip = (rel_bias.shape[1] - 1) // 2
    ramp = -(-max(r + 2 * t, top + clip + 1) // LANES) * LANES
    ext = _bias_ramp(rel_bias, top, ramp)
    rb0 = row0 // t
    return pl.pallas_call(
        functools.partial(_attn_sample_kernel, scale=HEAD_DIM ** -0.5, n_heads=n_heads),
        out_shape=jax.ShapeDtypeStruct((n_batch * t, width), BF16),
        grid=(n_batch,),
        in_specs=[
            pl.BlockSpec((t, width), lambda b: (rb0 + b, 0)),
            pl.BlockSpec((t, width), lambda b: (rb0 + b, 1)),
            pl.BlockSpec((t, width), lambda b: (rb0 + b, 2)),
            pl.BlockSpec((1, r * n_heads, HEAD_DIM), lambda b: (b, 0, 0)),
            pl.BlockSpec((1, r * n_heads, HEAD_DIM), lambda b: (b, 0, 0)),
            pl.BlockSpec((n_heads, 1, ramp), lambda b: (0, 0, 0)),
        ],
        out_specs=pl.BlockSpec((t, width), lambda b: (b, 0)),
        compiler_params=_params(("arbitrary",), 40),
        name="attn_sample",
    )(z, z, z, ck, cv, ext)


def _lru_tile(x, prev, h_in, gate_in, cw, cb, wa, ba, wx, bx, lam, scan_scratch=None):
    t = x.shape[0]
    p = prev.shape[0]
    k = cw.shape[0]
    xcat = jnp.concatenate([prev, x], axis=0)
    xc = cb
    for i in range(k):
        lo = p - (k - 1) + i
        xc = xc + xcat[lo:lo + t] * cw[i:i + 1]
    xcb = xc.astype(BF16)
    rg = _sigmoid(jnp.dot(xcb, wa.astype(BF16), preferred_element_type=F32) + ba)
    ig = _sigmoid(jnp.dot(xcb, wx.astype(BF16), preferred_element_type=F32) + bx)
    neg = -lam
    softplus = jnp.maximum(neg, 0.0) + jnp.log1p(jnp.exp(-jnp.abs(neg)))
    log_a = (-LRU_C) * rg * softplus
    a = jnp.exp(log_a)
    var = -jnp.tanh(log_a) * (a * a + 1.0)
    std = jnp.where(var == 0.0, 0.0, var * lax.rsqrt(var))
    b = std * (ig * xc)
    h = _linear_scan(a, b, h_in, scan_scratch)
    return h * _gelu_tanh(gate_in), h


def _doubling_scan(a, b, axis):
    n = a.shape[axis]
    idx = lax.broadcasted_iota(jnp.int32, a.shape, axis)
    step = 1
    while step < n:
        keep = idx >= step
        a_prev = jnp.where(keep, pltpu.roll(a, step, axis), 1.0)
        b_prev = jnp.where(keep, pltpu.roll(b, step, axis), 0.0)
        b = a * b_prev + b
        a = a * a_prev
        step *= 2
    return a, b


def _linear_scan(a, b, h_in, scratch=None):
    t, c = a.shape
    groups = t // SUBLANES
    a3, b3 = _doubling_scan(a.reshape(groups, SUBLANES, c), b.reshape(groups, SUBLANES, c), 1)
    if scratch is not None:
        ab_ref, hs_ref = scratch
        a2, b2 = a3.reshape(t, c), b3.reshape(t, c)
        ab_ref[0] = a2
        ab_ref[1] = b2
        ends = pl.ds(SUBLANES - 1, groups, stride=SUBLANES)
        a_cum, b_cum = _doubling_scan(ab_ref[0, ends, :], ab_ref[1, ends, :], 0)
        h_end = a_cum * h_in + b_cum
        first = lax.broadcasted_iota(jnp.int32, h_end.shape, 0) == 0
        hs_ref[...] = jnp.where(first, h_in, pltpu.roll(h_end, 1, 0))
        h_start = jnp.concatenate(
            [jnp.broadcast_to(hs_ref[g:g + 1, :], (SUBLANES, c)) for g in range(groups)], axis=0)
        return a2 * h_start + b2
    ae = jnp.broadcast_to(a3[:, SUBLANES - 1:, :], a3.shape).reshape(t, c)
    be = jnp.broadcast_to(b3[:, SUBLANES - 1:, :], b3.shape).reshape(t, c)
    step = SUBLANES
    while step < t:
        be = jnp.concatenate([be[:step], ae[step:] * be[:-step] + be[step:]], axis=0)
        ae = jnp.concatenate([ae[:step], ae[step:] * ae[:-step]], axis=0)
        step *= 2
    h_end = ae * h_in + be
    h_start = jnp.concatenate([jnp.broadcast_to(h_in, (SUBLANES, c)), h_end[:t - SUBLANES]], axis=0)
    return a3.reshape(t, c) * h_start + b3.reshape(t, c)


def _lru_prompt_block(n, x_ref, g_ref, cw_ref, cb_ref, wa_ref, ba_ref, wx_ref, bx_ref, lam_ref,
                      o_ref, conv_ref, h_ref, px_ref, ph_ref, ab_ref, hs_ref):
    t = x_ref.shape[0]
    sl = slice(n * LRU_BLOCK, (n + 1) * LRU_BLOCK)
    x = x_ref[:, sl]
    out, h = _lru_tile(x, px_ref[:, sl], ph_ref[SUBLANES - 1:SUBLANES, sl], g_ref[:, sl],
                       cw_ref[:, sl], cb_ref[:, sl], wa_ref[n], ba_ref[:, sl], wx_ref[n],
                       bx_ref[:, sl], lam_ref[:, sl], (ab_ref.at[n], hs_ref.at[n]))
    o_ref[:, sl] = out.astype(BF16)
    px_ref[:, sl] = x[t - SUBLANES:]
    ph_ref[:, sl] = h[t - SUBLANES:]
    conv_ref[0, :, sl] = x[t - SUBLANES:]
    h_ref[0, :, sl] = h[t - SUBLANES:]


def _mix_prompt_kernel(q_ref, ka_ref, kb_ref, kc_ref, va_ref, vb_ref, vc_ref, ext_ref,
                       x_ref, g_ref, cw_ref, cb_ref, wa_ref, ba_ref, wx_ref, bx_ref, lam_ref,
                       c_ref, wm_ref, bm_ref,
                       attn_ref, lru_ref, conv_ref, h_ref, table_ref,
                       bias_ref, px_ref, ph_ref, ab_ref, hs_ref, *, scale, mod_dims):
    i = pl.program_id(2)

    @pl.when(i == 0)
    def _():
        _attn_prompt_bias(ext_ref, bias_ref, q_ref.shape[0])
        px_ref[...] = jnp.zeros_like(px_ref)
        ph_ref[...] = jnp.zeros_like(ph_ref)

    _mod_kernel(c_ref, wm_ref, bm_ref, table_ref, **mod_dims)

    for n in range(q_ref.shape[1] // HEAD_DIM):
        _attn_prompt_head(n, i, q_ref, (ka_ref, kb_ref, kc_ref), (va_ref, vb_ref, vc_ref),
                          bias_ref, attn_ref, scale)
        _lru_prompt_block(n, x_ref, g_ref, cw_ref, cb_ref, wa_ref, ba_ref, wx_ref, bx_ref, lam_ref,
                          lru_ref, conv_ref, h_ref, px_ref, ph_ref, ab_ref, hs_ref)


def _mix_prompt(z, rel_bias, lw, c_all, w_mod, b_mod, mod_col0, mod_dims,
                n_batch, seq, n_heads, col_x, col_g):
    conv_w, conv_b, w_a, b_a, w_x, b_x, lam = lw
    n_blocks = w_a.shape[0]
    tt = 256
    nb = 4
    lanes = nb * HEAD_DIM
    assert HEAD_DIM == LRU_BLOCK and n_heads == n_blocks and n_heads % nb == 0
    assert tt == (N_PREV_CHUNKS * CHUNK) // 2 and seq % tt == 0
    assert col_x % lanes == 0 and col_g % lanes == 0
    nt = seq // tt
    ng = n_heads // nb
    width = n_heads * HEAD_DIM
    ext = _bias_ramp(rel_bias, 3 * tt - 1, 4 * tt)
    cx, cg = col_x // lanes, col_g // lanes
    vec = lambda a: a.reshape(1, width)

    steps = n_batch * ng * nt
    d_model, mod_cols = w_mod.shape
    mtn = (mod_cols - mod_col0) // steps
    assert mtn % LANES == 0 and mtn * steps == mod_cols - mod_col0 and mod_col0 % mtn == 0
    n_groups = mod_dims["n_prompt"] * mod_dims["groups_per_prompt"] + mod_dims["n_sample"]
    step = lambda b, g, t: (b * ng + g) * nt + t

    def rows(back, col0):
        return pl.BlockSpec((tt, lanes),
                            lambda b, g, t: (b * nt + jnp.maximum(t - back, 0), col0 + g))

    vspec = pl.BlockSpec((1, lanes), lambda b, g, t: (0, g))
    wspec = pl.BlockSpec((nb, LRU_BLOCK, LRU_BLOCK), lambda b, g, t: (g, 0, 0))
    sspec = pl.BlockSpec((1, SUBLANES, lanes), lambda b, g, t: (b, 0, g))
    return pl.pallas_call(
        functools.partial(_mix_prompt_kernel, scale=HEAD_DIM ** -0.5, mod_dims=mod_dims),
        out_shape=(
            jax.ShapeDtypeStruct((n_batch * seq, width), BF16),
            jax.ShapeDtypeStruct((n_batch * seq, width), BF16),
            jax.ShapeDtypeStruct((n_batch, SUBLANES, width), F32),
            jax.ShapeDtypeStruct((n_batch, SUBLANES, width), F32),
            jax.ShapeDtypeStruct((n_groups, mod_cols - mod_col0), F32),
        ),
        grid=(n_batch, ng, nt),
        in_specs=[
            rows(0, 0),
            rows(2, ng), rows(1, ng), rows(0, ng),
            rows(2, 2 * ng), rows(1, 2 * ng), rows(0, 2 * ng),
            pl.BlockSpec((nb, 1, 4 * tt), lambda b, g, t: (g, 0, 0)),
            rows(0, cx), rows(0, cg),
            pl.BlockSpec((conv_w.shape[0], lanes), lambda b, g, t: (0, g)),
            vspec, wspec, vspec, wspec, vspec, vspec,
            pl.BlockSpec(c_all.shape, lambda b, g, t: (0, 0)),
            pl.BlockSpec((d_model, mtn), lambda b, g, t: (0, mod_col0 // mtn + step(b, g, t))),
            pl.BlockSpec((1, mtn), lambda b, g, t: (0, mod_col0 // mtn + step(b, g, t))),
        ],
        out_specs=(rows(0, 0), rows(0, 0), sspec, sspec,
                   pl.BlockSpec((n_groups, mtn), lambda b, g, t: (0, step(b, g, t)))),
        scratch_shapes=[pltpu.VMEM((nb, 3, tt, tt), F32),
                        pltpu.VMEM((SUBLANES, lanes), F32), pltpu.VMEM((SUBLANES, lanes), F32),
                        pltpu.VMEM((nb, 2, tt, LRU_BLOCK), F32),
                        pltpu.VMEM((nb, tt // SUBLANES, LRU_BLOCK), F32)],
        compiler_params=_params(("arbitrary", "arbitrary", "arbitrary"), 40),
        name="mix_prompt",
    )(z, z, z, z, z, z, z, ext, z, z, conv_w, vec(conv_b), w_a, vec(b_a), w_x, vec(b_x), vec(lam),
      c_all, w_mod, b_mod)


def _lru_sample_kernel(x_ref, g_ref, prev_ref, h0_ref, cw_ref, cb_ref, wa_ref, ba_ref, wx_ref,
                       bx_ref, lam_ref, o_ref, conv_ref, h_ref, *, n_blocks):
    t = x_ref.shape[0]
    for n in range(n_blocks):
        sl = slice(n * LRU_BLOCK, (n + 1) * LRU_BLOCK)
        x = x_ref[:, sl]
        out, h = _lru_tile(x, prev_ref[0, :, sl], h0_ref[0, :, sl], g_ref[:, sl],
                           cw_ref[:, sl], cb_ref[:, sl], wa_ref[n], ba_ref[:, sl], wx_ref[n],
                           bx_ref[:, sl], lam_ref[:, sl])
        o_ref[:, sl] = out.astype(BF16)
        conv_ref[0, :, sl] = x[t - SUBLANES:]
        h_ref[0, :, sl] = h[t - SUBLANES:]


def _lru_sample(z, state_conv, state_h, lw, row0, n_batch, t, col_x, col_g):
    conv_w, conv_b, w_a, b_a, w_x, b_x, lam = lw
    n_blocks = w_a.shape[0]
    width = n_blocks * LRU_BLOCK
    k = conv_w.shape[0]
    assert t >= SUBLANES and k - 1 <= SUBLANES
    prev = jnp.pad(state_conv, ((0, 0), (SUBLANES - (k - 1), 0), (0, 0)))
    h0 = state_h.reshape(n_batch, 1, width)
    rb0 = row0 // t
    cx, cg = col_x // width, col_g // width
    vec = lambda a: a.reshape(1, width)
    vspec = pl.BlockSpec((1, width), lambda b: (0, 0))
    wspec = pl.BlockSpec((n_blocks, LRU_BLOCK, LRU_BLOCK), lambda b: (0, 0, 0))
    sspec = pl.BlockSpec((1, SUBLANES, width), lambda b: (b, 0, 0))
    return pl.pallas_call(
        functools.partial(_lru_sample_kernel, n_blocks=n_blocks),
        out_shape=(
            jax.ShapeDtypeStruct((n_batch * t, width), BF16),
            jax.ShapeDtypeStruct((n_batch, SUBLANES, width), F32),
            jax.ShapeDtypeStruct((n_batch, SUBLANES, width), F32),
        ),
        grid=(n_batch,),
        in_specs=[
            pl.BlockSpec((t, width), lambda b: (rb0 + b, cx)),
            pl.BlockSpec((t, width), lambda b: (rb0 + b, cg)),
            sspec,
            pl.BlockSpec((1, 1, width), lambda b: (b, 0, 0)),
            pl.BlockSpec((k, width), lambda b: (0, 0)),
            vspec, wspec, vspec, wspec, vspec, vspec,
        ],
        out_specs=(pl.BlockSpec((t, width), lambda b: (b, 0)), sspec, sspec),
        compiler_params=_params(("arbitrary",), 32),
        name="lru_sample",
    )(z, z, prev, h0, conv_w, vec(conv_b), w_a, vec(b_a), w_x, vec(b_x), vec(lam))


def _out_proj_kernel(ap_ref, lp_ref, as_ref, ls_ref, w_ref, xp_ref, xs_ref, gp_ref, gs_ref,
                     op_ref, os_ref):
    i = pl.program_id(1)
    ka = ap_ref.shape[1]

    def tile(a_ref, l_ref, x_ref, g_ref, o_ref):
        acc = (jnp.dot(a_ref[...], w_ref[:ka, :], preferred_element_type=F32)
               + jnp.dot(l_ref[...], w_ref[ka:, :], preferred_element_type=F32))
        o_ref[...] = x_ref[...] + _group_scale(acc, g_ref[...])

    _per_row_group(i,
                   lambda: tile(ap_ref, lp_ref, xp_ref, gp_ref, op_ref),
                   lambda: tile(as_ref, ls_ref, xs_ref, gs_ref, os_ref))


def _out_proj(attn_p, lru_p, attn_s, lru_s, w_out, x_p, x_s, table, gate_col, group):
    d, n = w_out.shape
    ka, kl = attn_p.shape[1], lru_p.shape[1]
    assert ka + kl == d and w_out.dtype == BF16
    tn = min(1024, n)
    rg = _RowGroups(x_p.shape[0], x_s.shape[0], tile=512)
    gcol = lambda j: gate_col * (n // tn) + j
    sample_table_block = (rg.rows_p // group) // (rg.rows_s // group)
    return pl.pallas_call(
        _out_proj_kernel,
        out_shape=(jax.ShapeDtypeStruct((rg.rows_p, n), F32), jax.ShapeDtypeStruct((rg.rows_s, n), F32)),
        grid=(n // tn, rg.steps),
        in_specs=[
            rg.prompt_spec(ka, lambda j: 0),
            rg.prompt_spec(kl, lambda j: 0),
            rg.sample_spec(ka, lambda j: 0, resident=True),
            rg.sample_spec(kl, lambda j: 0, resident=True),
            pl.BlockSpec((d, tn), lambda j, i: (0, j)),
            rg.prompt_spec(tn, lambda j: j),
            rg.sample_spec(tn, lambda j: j),
            rg.prompt_spec(tn, gcol, rows_per_row=group),
            rg.sample_spec(tn, gcol, rows_per_row=group, row_block=sample_table_block),
        ],
        out_specs=(rg.prompt_spec(tn, lambda j: j, output=True), rg.sample_spec(tn, lambda j: j)),
        compiler_params=_params(("arbitrary", "arbitrary"), 56),
        name="out_proj",
    )(attn_p, lru_p, attn_s, lru_s, w_out, x_p, x_s, table, table)


def _ffn_up_kernel(xp_ref, xs_ref, wg_hbm, wu_hbm, wd_ref, op_ref, os_ref, wdb_ref,
                   stage_ref, wb_ref, sem, *, n_cols, n_wd_chunks):
    j = pl.program_id(0)
    i = pl.program_id(1)
    tn = wb_ref.shape[2]
    n_full, rem = divmod(n_cols, tn)

    rc = wd_ref.shape[0]
    chunk = j * pl.num_programs(1) + i

    @pl.when(chunk < n_wd_chunks)
    def _():
        row = lax.broadcasted_iota(jnp.int32, wd_ref.shape, 0)
        wdb_ref[...] = jnp.where(chunk * rc + row < n_cols, wd_ref[...], 0.0).astype(BF16)

    def copies(jj, width):
        col = pl.multiple_of(jj * tn, tn)
        return [pltpu.make_async_copy(src.at[:, pl.ds(col, width)],
                                      stage_ref.at[w, :, pl.ds(0, width)], sem.at[w])
                for w, src in enumerate((wg_hbm, wu_hbm))]

    def start(jj):
        @pl.when(jj < n_full)
        def _():
            for c in copies(jj, tn):
                c.start()

        if rem:
            @pl.when(jj == n_full)
            def _():
                for c in copies(jj, rem):
                    c.start()

    @pl.when((j == 0) & (i == 0))
    def _():
        start(j)

    @pl.when(i == 0)
    def _():
        @pl.when(j < n_full)
        def _():
            for c in copies(j, tn):
                c.wait()
            wb_ref[...] = stage_ref[...].astype(BF16)

        if rem:
            @pl.when(j == n_full)
            def _():
                for c in copies(j, rem):
                    c.wait()
                wb_ref[:, :, :rem] = stage_ref[:, :, :rem].astype(BF16)
                wb_ref[:, :, rem:] = jnp.zeros((2, wb_ref.shape[1], tn - rem), BF16)

        @pl.when(j >= n_full + (1 if rem else 0))
        def _():
            wb_ref[...] = jnp.zeros_like(wb_ref)

    @pl.when((i == 1) & (j + 1 < pl.num_programs(0)))
    def _():
        start(j + 1)

    def tile(x_ref, o_ref):
        def hidden(width):
            x = x_ref[...]
            g = jnp.dot(x, wb_ref[0, :, :width], preferred_element_type=F32)
            u = jnp.dot(x, wb_ref[1, :, :width], preferred_element_type=F32)
            o_ref[:, :width] = (g * _sigmoid(g) * u).astype(BF16)

        if rem:
            pl.when(j != n_full)(lambda: hidden(tn))

            @pl.when(j == n_full)
            def _():
                hidden(rem)
                o_ref[:, rem:] = jnp.zeros((o_ref.shape[0], tn - rem), BF16)
        else:
            hidden(tn)

    _per_row_group(i, lambda: tile(xp_ref, op_ref), lambda: tile(xs_ref, os_ref))


def _ffn_up(u_p, u_s, w_gate, w_up, w_down, ff_pad):
    d, ff = w_gate.shape
    n_out = w_down.shape[1]
    tn = 512
    assert ff % LANES == 0 and ff_pad % tn == 0
    rg = _RowGroups(u_p.shape[0], u_s.shape[0])
    assert rg.steps >= 2
    steps = (ff_pad // tn) * rg.steps
    rc = next(r for r in range(16, ff_pad + 1, 16)
              if ff_pad % r == 0 and ff % r == 0 and ff_pad // r <= steps)
    n_wd_chunks, n_real_chunks = ff_pad // rc, ff // rc
    chunk_of = lambda j, i: j * rg.steps + i
    hbm = pl.BlockSpec(memory_space=pl.ANY)
    return pl.pallas_call(
        functools.partial(_ffn_up_kernel, n_cols=ff, n_wd_chunks=n_wd_chunks),
        out_shape=(jax.ShapeDtypeStruct((rg.rows_p, ff_pad), BF16),
                   jax.ShapeDtypeStruct((rg.rows_s, ff_pad), BF16),
                   jax.ShapeDtypeStruct((ff_pad, n_out), BF16)),
        grid=(ff_pad // tn, rg.steps),
        in_specs=[rg.prompt_spec(d, lambda j: 0), rg.sample_spec(d, lambda j: 0, resident=True),
                  hbm, hbm,
                  pl.BlockSpec((rc, n_out),
                               lambda j, i: (jnp.minimum(chunk_of(j, i), n_real_chunks - 1), 0))],
        out_specs=(rg.prompt_spec(tn, lambda j: j, output=True), rg.sample_spec(tn, lambda j: j),
                   pl.BlockSpec((rc, n_out),
                                lambda j, i: (jnp.minimum(chunk_of(j, i), n_wd_chunks - 1), 0))),
        scratch_shapes=[pltpu.VMEM((2, d, tn), F32), pltpu.VMEM((2, d, tn), BF16),
                        pltpu.SemaphoreType.DMA((2,))],
        compiler_params=_params(("arbitrary", "arbitrary"), 62),
        name="ffn_up",
    )(u_p, u_s, w_gate, w_up, w_down)


def _ffn_down_kernel(h_ref, w_ref, x_ref, g_ref, o_ref):
    k = pl.program_id(2)
    last = pl.num_programs(2) - 1
    tm = o_ref.shape[0]
    rows = min(tm, 512)
    group = tm // g_ref.shape[0]

    def chunks():
        for r0 in range(0, tm, rows):
            sl = slice(r0, r0 + rows)
            yield sl, slice(r0 // group, (r0 + rows) // group), jnp.dot(
                h_ref[sl, :], w_ref[...], preferred_element_type=F32)

    @pl.when(k == 0)
    def _():
        for sl, _, part in chunks():
            o_ref[sl, :] = part

    @pl.when((k > 0) & (k < last))
    def _():
        for sl, _, part in chunks():
            o_ref[sl, :] += part

    @pl.when(k == last)
    def _():
        for sl, gsl, part in chunks():
            o_ref[sl, :] = x_ref[sl, :] + _group_scale(o_ref[sl, :] + part, g_ref[gsl, :])


def _ffn_down(h, w_down, x1, table, gate_col, group, group0, tm, tn):
    rows, ff_pad = h.shape
    n = w_down.shape[1]
    tk = 1024
    tn = min(tn, n)
    gt = tm // group
    assert w_down.shape[0] == ff_pad and w_down.dtype == BF16 and ff_pad // tk >= 2
    assert rows % tm == 0 and ff_pad % tk == 0 and n % tn == 0 and group0 % gt == 0
    g0 = group0 // gt
    gcol0 = gate_col * (n // tn)
    return pl.pallas_call(
        _ffn_down_kernel,
        out_shape=jax.ShapeDtypeStruct((rows, n), F32),
        grid=(rows // tm, n // tn, ff_pad // tk),
        in_specs=[
            pl.BlockSpec((tm, tk), lambda i, j, k: (i, k)),
            pl.BlockSpec((tk, tn), lambda i, j, k: (k, j)),
            pl.BlockSpec((tm, tn), lambda i, j, k: (i, j)),
            pl.BlockSpec((gt, tn), lambda i, j, k: (g0 + i, gcol0 + j)),
        ],
        out_specs=pl.BlockSpec((tm, tn), lambda i, j, k: (i, j)),
        compiler_params=_params(("arbitrary", "arbitrary", "arbitrary"), 58),
        name="ffn_down",
    )(h, w_down, x1, table)


def _layer(x_p, x_s, c_prompt, c_sample, cache_k, cache_v, state_conv, state_h, lw, dims):
    (norm_mix_g, norm_ffn_g, w_mod, b_mod, w_in, q_norm_g, k_norm_g, rel_bias,
     conv_w, conv_b, w_rg_a, b_rg_a, w_rg_x, b_rg_x, lru_lambda, w_out,
     w_ffn_gate, w_ffn_up, w_ffn_down) = lw
    n_prompt, seq, n_sample, t_s = dims
    group = t_s
    n_heads = rel_bias.shape[0]
    attn_w = n_heads * HEAD_DIM
    lru_w = w_rg_a.shape[0] * LRU_BLOCK
    rows_p = n_prompt * seq
    rows_s = n_sample * t_s
    groups_p = rows_p // group

    d = x_p.shape[1]
    c_all = _mod_inputs(c_prompt, c_sample)
    b_mod2 = b_mod.reshape(1, -1)
    mod_dims = dict(n_sample=n_sample, n_prompt=n_prompt, groups_per_prompt=seq // group)
    table1 = _mod_table(c_all, w_mod, b_mod2, 2 * d, **mod_dims)
    u_p = _norm_mod(x_p, norm_mix_g, table1, 0, 1, group, 0)
    u_s = _norm_mod(x_s, norm_mix_g, table1, 0, 1, group, groups_p)
    qk_gain = jnp.concatenate([jnp.tile(q_norm_g, n_heads), jnp.tile(k_norm_g, n_heads)]).reshape(1, -1)
    z_p, z_s, w_out_b = _in_proj(u_p, u_s, w_in, qk_gain, attn_w, w_out)

    lru_params = (conv_w, conv_b, w_rg_a, b_rg_a, w_rg_x, b_rg_x, lru_lambda)
    col_x, col_g = 3 * attn_w, 3 * attn_w + lru_w
    attn_p, lru_p, conv_p, h_p, table2 = _mix_prompt(
        z_p, rel_bias, lru_params, c_all, w_mod, b_mod2, 2 * d, mod_dims,
        n_prompt, seq, n_heads, col_x, col_g)
    attn_s = _attn_sample(z_s, cache_k, cache_v, rel_bias, 0, n_sample, t_s, n_heads)
    lru_s, conv_s, h_s = _lru_sample(z_s, state_conv, state_h, lru_params, 0, n_sample, t_s,
                                     col_x, col_g)

    x1_p, x1_s = _out_proj(attn_p, lru_p, attn_s, lru_s, w_out_b, x_p, x_s, table2, 0, group)
    u2_p = _norm_mod(x1_p, norm_ffn_g, table2, 1, 2, group, 0)
    u2_s = _norm_mod(x1_s, norm_ffn_g, table2, 1, 2, group, groups_p)
    ff = w_ffn_gate.shape[1]
    ff_pad = -(-ff // 1024) * 1024
    hid_p, hid_s, w_down_b = _ffn_up(u2_p, u2_s, w_ffn_gate, w_ffn_up, w_ffn_down, ff_pad)
    y_p = _ffn_down(hid_p, w_down_b, x1_p, table2, 3, group, 0, 2048, 1024)
    y_s = _ffn_down(hid_s, w_down_b, x1_s, table2, 3, group, groups_p, rows_s, 2048)

    keep = min(N_PREV_CHUNKS * CHUNK, seq)
    kv_p = z_p.reshape(n_prompt, seq, -1)[:, seq - keep:, attn_w:3 * attn_w]
    kv_p = kv_p.reshape(n_prompt, keep, 2, n_heads, HEAD_DIM)
    kv_s = z_s[:, attn_w:3 * attn_w].reshape(n_sample, t_s, 2, n_heads, HEAD_DIM)
    k_conv = conv_w.shape[0] - 1
    state = dict(
        k_p=kv_p[:, :, 0], v_p=kv_p[:, :, 1], conv_p=conv_p[:, SUBLANES - k_conv:], h_p=h_p[:, SUBLANES - 1],
        k_s=kv_s[:, :, 0], v_s=kv_s[:, :, 1], conv_s=conv_s[:, SUBLANES - k_conv:], h_s=h_s[:, SUBLANES - 1])
    return y_p, y_s, state


def kernel(x_prompt, x_sample, cache_k, cache_v, state_conv, state_h, c_prompt, c_sample, norm_mix_g, norm_ffn_g, w_mod, b_mod, w_in, q_norm_g, k_norm_g, rel_bias, conv_w, conv_b, w_rg_a, b_rg_a, w_rg_x, b_rg_x, lru_lambda, w_out, w_ffn_gate, w_ffn_up, w_ffn_down):
    n_prompt, seq, d = x_prompt.shape
    n_sample, t_s, _ = x_sample.shape
    depth = w_in.shape[0]
    dims = (n_prompt, seq, n_sample, t_s)
    yp = x_prompt.reshape(n_prompt * seq, d)
    ys = x_sample.reshape(n_sample * t_s, d)
    states = []
    for l in range(depth):
        lw = (norm_mix_g[l], norm_ffn_g[l], w_mod[l], b_mod[l], w_in[l], q_norm_g[l], k_norm_g[l],
              rel_bias[l], conv_w[l], conv_b[l], w_rg_a[l], b_rg_a[l], w_rg_x[l], b_rg_x[l],
              lru_lambda[l], w_out[l], w_ffn_gate[l], w_ffn_up[l], w_ffn_down[l])
        yp, ys, st = _layer(yp, ys, c_prompt, c_sample, cache_k[l], cache_v[l], state_conv[l],
                            state_h[l], lw, dims)
        states.append(st)
    stack = lambda name: jnp.stack([s[name] for s in states])
    return (yp.reshape(n_prompt, seq, d), ys.reshape(n_sample, t_s, d),
            stack("k_p"), stack("v_p"), stack("conv_p"), stack("h_p"),
            stack("k_s"), stack("v_s"), stack("conv_s"), stack("h_s"))
```

```python
import functools
import math

import jax
import jax.numpy as jnp
from jax import lax
from jax.experimental import pallas as pl
from jax.experimental.pallas import tpu as pltpu

F32 = jnp.float32
BF16 = jnp.bfloat16

CHUNK = 64
N_PREV_CHUNKS = 8
HEAD_DIM = 128
LRU_BLOCK = 128
LRU_C = 8.0
NEG_INF = -1e30
EPS = 1e-6
LOG2E = math.log2(math.e)

LANES = 128
SUBLANES = 8
MIB = 1024 * 1024


def _params(semantics, vmem_mib):
    return pltpu.CompilerParams(dimension_semantics=semantics, vmem_limit_bytes=vmem_mib * MIB)


def _sigmoid(x):
    return 0.5 * jnp.tanh(0.5 * x) + 0.5


def _gelu_tanh(x):
    c = math.sqrt(2.0 / math.pi)
    return x * (0.5 * (1.0 + jnp.tanh(c * (x + 0.044715 * (x * x * x)))))


def _group_scale(y, g):
    rows, n = y.shape
    groups = g.shape[0]
    return (y.reshape(groups, rows // groups, n) * g[:, None, :]).reshape(rows, n)


PROMPT_TILE_ROWS = 1024


class _RowGroups:
    def __init__(self, rows_p, rows_s, tile=PROMPT_TILE_ROWS):
        assert rows_p % tile == 0
        self.rows_p, self.rows_s, self.tile = rows_p, rows_s, tile
        self.np_tiles = rows_p // tile
        self.steps = self.np_tiles + 1

    def prompt_spec(self, cols, col_block, rows_per_row=1, output=False):
        first = 0 if output else self.np_tiles - 1
        return pl.BlockSpec((self.tile // rows_per_row, cols),
                            lambda j, i: (jnp.where(i == 0, first, i - 1), col_block(j)))

    def sample_spec(self, cols, col_block, rows_per_row=1, row_block=0, resident=False):
        mode = dict(pipeline_mode=pl.Buffered(1)) if resident else {}
        return pl.BlockSpec((self.rows_s // rows_per_row, cols),
                            lambda j, i: (row_block, col_block(j)), **mode)


def _per_row_group(i, prompt_fn, sample_fn):
    pl.when(i == 0)(sample_fn)
    pl.when(i > 0)(prompt_fn)


def _mod_kernel(c_ref, w_ref, b_ref, o_ref, *, n_sample, n_prompt, groups_per_prompt):
    c = c_ref[...]
    s = (c * _sigmoid(c)).astype(BF16)
    m = jnp.dot(s, w_ref[...].astype(BF16), preferred_element_type=F32) + b_ref[...]
    tn = m.shape[1]
    for b in range(n_prompt):
        row = m[n_sample + b:n_sample + b + 1, :]
        o_ref[b * groups_per_prompt:(b + 1) * groups_per_prompt, :] = jnp.broadcast_to(
            row, (groups_per_prompt, tn))
    o_ref[n_prompt * groups_per_prompt:n_prompt * groups_per_prompt + n_sample, :] = m[:n_sample, :]


def _mod_inputs(c_prompt, c_sample):
    rows = c_sample.shape[0] + c_prompt.shape[0]
    rows_pad = -(-rows // SUBLANES) * SUBLANES
    return jnp.concatenate(
        [c_sample, c_prompt, jnp.zeros((rows_pad - rows, c_prompt.shape[1]), c_prompt.dtype)], axis=0)


def _mod_table(c_all, w_mod, b_mod, n_cols, n_sample, n_prompt, groups_per_prompt):
    rows_pad, d = c_all.shape
    n_groups = n_prompt * groups_per_prompt + n_sample
    tn = 512
    return pl.pallas_call(
        functools.partial(_mod_kernel, n_sample=n_sample, n_prompt=n_prompt,
                          groups_per_prompt=groups_per_prompt),
        out_shape=jax.ShapeDtypeStruct((n_groups, n_cols), F32),
        grid=(n_cols // tn,),
        in_specs=[
            pl.BlockSpec((rows_pad, d), lambda j: (0, 0)),
            pl.BlockSpec((d, tn), lambda j: (0, j)),
            pl.BlockSpec((1, tn), lambda j: (0, j)),
        ],
        out_specs=pl.BlockSpec((n_groups, tn), lambda j: (0, j)),
        compiler_params=_params(("arbitrary",), 40),
        name="mod_table",
    )(c_all, w_mod, b_mod)


def _norm_mod_kernel(x_ref, gain_ref, shift_ref, scale_ref, o_ref):
    x = x_ref[...]
    ms = jnp.mean(x * x, axis=-1, keepdims=True)
    y = x * lax.rsqrt(ms + EPS) * gain_ref[...]
    rows, d = y.shape
    groups = scale_ref.shape[0]
    y3 = y.reshape(groups, rows // groups, d)
    u = y3 * (1.0 + scale_ref[...][:, None, :]) + shift_ref[...][:, None, :]
    o_ref[...] = u.reshape(rows, d).astype(BF16)


def _norm_mod(x, gain, table, shift_col, scale_col, group, group0):
    rows, d = x.shape
    tm = 512
    gt = tm // group
    assert rows % tm == 0 and group0 % gt == 0
    g0 = group0 // gt
    return pl.pallas_call(
        _norm_mod_kernel,
        out_shape=jax.ShapeDtypeStruct((rows, d), BF16),
        grid=(rows // tm,),
        in_specs=[
            pl.BlockSpec((tm, d), lambda i: (i, 0)),
            pl.BlockSpec((1, d), lambda i: (0, 0)),
            pl.BlockSpec((gt, d), lambda i: (g0 + i, shift_col)),
            pl.BlockSpec((gt, d), lambda i: (g0 + i, scale_col)),
        ],
        out_specs=pl.BlockSpec((tm, d), lambda i: (i, 0)),
        compiler_params=_params(("arbitrary",), 40),
        name="norm_mod",
    )(x, gain.reshape(1, d), table, table)


def _row_chunk(rows, granule, max_chunks):
    return next(r for r in range(granule, rows + 1, granule)
                if rows % r == 0 and rows // r <= max_chunks)


def _in_proj_kernel(xp_ref, xs_ref, w_hbm, gain_ref, wo_ref, op_ref, os_ref, wob_ref,
                    stage_ref, wb_ref, sem, *, n_norm_tiles, n_wo_chunks):
    j = pl.program_id(0)
    i = pl.program_id(1)
    tn = wb_ref.shape[1]

    def copy(jj):
        return pltpu.make_async_copy(w_hbm.at[:, pl.ds(pl.multiple_of(jj * tn, tn), tn)],
                                     stage_ref, sem.at[0])

    @pl.when((j == 0) & (i == 0))
    def _():
        copy(j).start()

    @pl.when(i == 0)
    def _():
        copy(j).wait()
        wb_ref[...] = stage_ref[...].astype(BF16)

    @pl.when((i == 1) & (j + 1 < pl.num_programs(0)))
    def _():
        copy(j + 1).start()

    @pl.when(j * pl.num_programs(1) + i < n_wo_chunks)
    def _():
        wob_ref[...] = wo_ref[...].astype(BF16)

    def tile(x_ref, o_ref):
        rows, tn = o_ref.shape

        @pl.when(j < n_norm_tiles)
        def _():
            half = rows // 2
            accs = [jnp.dot(x_ref[r0:r0 + half, :], wb_ref[...], preferred_element_type=F32)
                    for r0 in (0, half)]
            for r0, acc in zip((0, half), accs):
                for g in range(tn // HEAD_DIM):
                    sl = slice(g * HEAD_DIM, (g + 1) * HEAD_DIM)
                    zg = acc[:, sl]
                    ms = jnp.mean(zg * zg, axis=-1, keepdims=True)
                    o_ref[r0:r0 + half, sl] = zg * lax.rsqrt(ms + EPS) * gain_ref[:, sl]

        @pl.when(j >= n_norm_tiles)
        def _():
            o_ref[...] = jnp.dot(x_ref[...], wb_ref[...], preferred_element_type=F32)

    _per_row_group(i, lambda: tile(xp_ref, op_ref), lambda: tile(xs_ref, os_ref))


def _in_proj(u_p, u_s, w_in, qk_gain, attn_width, w_out):
    d, n = w_in.shape
    tn = 1024
    assert n % tn == 0 and (2 * attn_width) % tn == 0
    n_norm_tiles = 2 * attn_width // tn
    rg = _RowGroups(u_p.shape[0], u_s.shape[0], tile=512)
    assert rg.steps >= 2
    ko, no = w_out.shape
    rc = _row_chunk(ko, 16, (n // tn) * rg.steps)
    n_wo_chunks = ko // rc
    wo_spec = pl.BlockSpec((rc, no), lambda j, i: (jnp.minimum(j * rg.steps + i, n_wo_chunks - 1), 0))
    return pl.pallas_call(
        functools.partial(_in_proj_kernel, n_norm_tiles=n_norm_tiles, n_wo_chunks=n_wo_chunks),
        out_shape=(jax.ShapeDtypeStruct((rg.rows_p, n), F32), jax.ShapeDtypeStruct((rg.rows_s, n), F32),
                   jax.ShapeDtypeStruct((ko, no), BF16)),
        grid=(n // tn, rg.steps),
        in_specs=[
            rg.prompt_spec(d, lambda j: 0),
            rg.sample_spec(d, lambda j: 0, resident=True),
            pl.BlockSpec(memory_space=pl.ANY),
            pl.BlockSpec((1, tn), lambda j, i: (0, jnp.minimum(j, n_norm_tiles - 1))),
            wo_spec,
        ],
        out_specs=(rg.prompt_spec(tn, lambda j: j, output=True), rg.sample_spec(tn, lambda j: j),
                   wo_spec),
        scratch_shapes=[pltpu.VMEM((d, tn), F32), pltpu.VMEM((d, tn), BF16),
                        pltpu.SemaphoreType.DMA((1,))],
        compiler_params=_params(("arbitrary", "arbitrary"), 58),
        name="in_proj",
    )(u_p, u_s, w_in, qk_gain, w_out)


def _bias_ramp(rel_bias, top, length):
    n_heads, n_rel = rel_bias.shape
    clip = (n_rel - 1) // 2
    lead = top - clip
    tail = length - lead - n_rel
    assert lead >= 0 and tail >= 0
    rev = rel_bias[:, ::-1]
    ext = jnp.concatenate([
        jnp.broadcast_to(rel_bias[:, n_rel - 1:], (n_heads, lead)),
        rev,
        jnp.broadcast_to(rel_bias[:, :1], (n_heads, tail)),
    ], axis=1)
    return ext.reshape(n_heads, 1, length)


def _lane_reduce(tiles, combine, reduce):
    parts = [t[:, c:c + LANES] for t in tiles for c in range(0, t.shape[1], LANES)]
    return reduce(functools.reduce(combine, parts), axis=-1, keepdims=True)


def _toeplitz(base, rows):
    w = base.shape[1]
    return pltpu.roll(jnp.broadcast_to(base, (rows, w)), w - (rows - 1), 1, stride=1, stride_axis=0)


def _attn_prompt_bias(ext_ref, bias_ref, tq):
    shift = CHUNK.bit_length() - 1
    rowc = jnp.right_shift(lax.broadcasted_iota(jnp.int32, (tq, tq), 0), shift)
    colc = jnp.right_shift(lax.broadcasted_iota(jnp.int32, (tq, tq), 1), shift)
    for h in range(bias_ref.shape[0]):
        for m in range(3):
            off = (2 - m) * tq
            t = _toeplitz(ext_ref[h, :, off:off + 2 * tq], tq)[:, :tq]
            if m == 0:
                t = jnp.where(colc <= rowc, t, NEG_INF)
            if m == 2:
                t = jnp.where(rowc <= colc, t, NEG_INF)
            bias_ref[h, m] = t * LOG2E


def _attn_prompt_head(h, i, q_ref, k_refs, v_refs, bias_ref, o_ref, scale):
    dn = (((1,), (1,)), ((), ()))
    sl = slice(h * HEAD_DIM, (h + 1) * HEAD_DIM)
    q = q_ref[:, sl].astype(BF16)
    scores = []
    for m, k_ref in zip((2, 1, 0), k_refs):
        s = lax.dot_general(q, k_ref[:, sl].astype(BF16), dn, preferred_element_type=F32)
        s = s * (scale * LOG2E) + bias_ref[h, m]
        if m > 0:
            s = s + jnp.where(i >= m, 0.0, NEG_INF)
        scores.append(s)
    mx = _lane_reduce(scores, jnp.maximum, jnp.max)
    probs = [jnp.exp2(s - mx) for s in scores]
    denom = _lane_reduce(probs, jnp.add, jnp.sum)
    out = functools.reduce(jnp.add, [
        jnp.dot(p.astype(BF16), v_ref[:, sl].astype(BF16), preferred_element_type=F32)
        for p, v_ref in zip(probs, v_refs)])
    o_ref[:, sl] = (out / denom).astype(BF16)


def _attn_sample_kernel(q_ref, kn_ref, vn_ref, ck_ref, cv_ref, ext_ref, o_ref, *, scale, n_heads):
    t = q_ref.shape[0]
    r = ck_ref.shape[1] // n_heads
    dn = (((1,), (1,)), ((), ()))
    for h in range(n_heads):
        sl = slice(h * HEAD_DIM, (h + 1) * HEAD_DIM)
        rows_h = pl.ds(h, r, stride=n_heads)
        bias = _toeplitz(ext_ref[h], t)
        q = q_ref[:, sl].astype(BF16)
        s1 = lax.dot_general(q, ck_ref[0, rows_h, :].astype(BF16), dn, preferred_element_type=F32)
        s1 = s1 * scale + bias[:, :r]
        s2 = lax.dot_general(q, kn_ref[:, sl].astype(BF16), dn, preferred_element_type=F32)
        s2 = s2 * scale + bias[:, r:r + t]
        mx = jnp.maximum(jnp.max(s1, axis=-1, keepdims=True), jnp.max(s2, axis=-1, keepdims=True))
        p1 = jnp.exp(s1 - mx)
        p2 = jnp.exp(s2 - mx)
        denom = jnp.sum(p1, axis=-1, keepdims=True) + jnp.sum(p2, axis=-1, keepdims=True)
        out = (jnp.dot(p1.astype(BF16), cv_ref[0, rows_h, :].astype(BF16), preferred_element_type=F32)
               + jnp.dot(p2.astype(BF16), vn_ref[:, sl].astype(BF16), preferred_element_type=F32))
        o_ref[:, sl] = (out / denom).astype(BF16)


def _attn_sample(z, cache_k, cache_v, rel_bias, row0, n_batch, t, n_heads):
    r = cache_k.shape[1]
    width = n_heads * HEAD_DIM
    ck = cache_k.reshape(n_batch, r * n_heads, HEAD_DIM)
    cv = cache_v.reshape(n_batch, r * n_heads, HEAD_DIM)
    top = r + t - 1
    clip = (rel_bias.shape[1] - 1) // 2
    ramp = -(-max(r + 2 * t, top + clip + 1) // LANES) * LANES
    ext = _bias_ramp(rel_bias, top, ramp)
    rb0 = row0 // t
    return pl.pallas_call(
        functools.partial(_attn_sample_kernel, scale=HEAD_DIM ** -0.5, n_heads=n_heads),
        out_shape=jax.ShapeDtypeStruct((n_batch * t, width), BF16),
        grid=(n_batch,),
        in_specs=[
            pl.BlockSpec((t, width), lambda b: (rb0 + b, 0)),
            pl.BlockSpec((t, width), lambda b: (rb0 + b, 1)),
            pl.BlockSpec((t, width), lambda b: (rb0 + b, 2)),
            pl.BlockSpec((1, r * n_heads, HEAD_DIM), lambda b: (b, 0, 0)),
            pl.BlockSpec((1, r * n_heads, HEAD_DIM), lambda b: (b, 0, 0)),
            pl.BlockSpec((n_heads, 1, ramp), lambda b: (0, 0, 0)),
        ],
        out_specs=pl.BlockSpec((t, width), lambda b: (b, 0)),
        compiler_params=_params(("arbitrary",), 40),
        name="attn_sample",
    )(z, z, z, ck, cv, ext)


def _lru_tile(x, prev, h_in, gate_in, cw, cb, wa, ba, wx, bx, lam, scan_scratch=None):
    t = x.shape[0]
    p = prev.shape[0]
    k = cw.shape[0]
    xcat = jnp.concatenate([prev, x], axis=0)
    xc = cb
    for i in range(k):
        lo = p - (k - 1) + i
        xc = xc + xcat[lo:lo + t] * cw[i:i + 1]
    xcb = xc.astype(BF16)
    rg = _sigmoid(jnp.dot(xcb, wa.astype(BF16), preferred_element_type=F32) + ba)
    ig = _sigmoid(jnp.dot(xcb, wx.astype(BF16), preferred_element_type=F32) + bx)
    neg = -lam
    softplus = jnp.maximum(neg, 0.0) + jnp.log1p(jnp.exp(-jnp.abs(neg)))
    log_a = (-LRU_C) * rg * softplus
    a = jnp.exp(log_a)
    var = -jnp.tanh(log_a) * (a * a + 1.0)
    std = jnp.where(var == 0.0, 0.0, var * lax.rsqrt(var))
    b = std * (ig * xc)
    h = _linear_scan(a, b, h_in, scan_scratch)
    return h * _gelu_tanh(gate_in), h


def _doubling_scan(a, b, axis):
    n = a.shape[axis]
    idx = lax.broadcasted_iota(jnp.int32, a.shape, axis)
    step = 1
    while step < n:
        keep = idx >= step
        a_prev = jnp.where(keep, pltpu.roll(a, step, axis), 1.0)
        b_prev = jnp.where(keep, pltpu.roll(b, step, axis), 0.0)
        b = a * b_prev + b
        a = a * a_prev
        step *= 2
    return a, b


def _linear_scan(a, b, h_in, scratch=None):
    t, c = a.shape
    groups = t // SUBLANES
    a3, b3 = _doubling_scan(a.reshape(groups, SUBLANES, c), b.reshape(groups, SUBLANES, c), 1)
    if scratch is not None:
        ab_ref, hs_ref = scratch
        a2, b2 = a3.reshape(t, c), b3.reshape(t, c)
        ab_ref[0] = a2
        ab_ref[1] = b2
        ends = pl.ds(SUBLANES - 1, groups, stride=SUBLANES)
        a_cum, b_cum = _doubling_scan(ab_ref[0, ends, :], ab_ref[1, ends, :], 0)
        h_end = a_cum * h_in + b_cum
        first = lax.broadcasted_iota(jnp.int32, h_end.shape, 0) == 0
        hs_ref[...] = jnp.where(first, h_in, pltpu.roll(h_end, 1, 0))
        h_start = jnp.concatenate(
            [jnp.broadcast_to(hs_ref[g:g + 1, :], (SUBLANES, c)) for g in range(groups)], axis=0)
        return a2 * h_start + b2
    ae = jnp.broadcast_to(a3[:, SUBLANES - 1:, :], a3.shape).reshape(t, c)
    be = jnp.broadcast_to(b3[:, SUBLANES - 1:, :], b3.shape).reshape(t, c)
    step = SUBLANES
    while step < t:
        be = jnp.concatenate([be[:step], ae[step:] * be[:-step] + be[step:]], axis=0)
        ae = jnp.concatenate([ae[:step], ae[step:] * ae[:-step]], axis=0)
        step *= 2
    h_end = ae * h_in + be
    h_start = jnp.concatenate([jnp.broadcast_to(h_in, (SUBLANES, c)), h_end[:t - SUBLANES]], axis=0)
    return a3.reshape(t, c) * h_start + b3.reshape(t, c)


def _lru_prompt_block(n, x_ref, g_ref, cw_ref, cb_ref, wa_ref, ba_ref, wx_ref, bx_ref, lam_ref,
                      o_ref, conv_ref, h_ref, px_ref, ph_ref, ab_ref, hs_ref):
    t = x_ref.shape[0]
    sl = slice(n * LRU_BLOCK, (n + 1) * LRU_BLOCK)
    x = x_ref[:, sl]
    out, h = _lru_tile(x, px_ref[:, sl], ph_ref[SUBLANES - 1:SUBLANES, sl], g_ref[:, sl],
                       cw_ref[:, sl], cb_ref[:, sl], wa_ref[n], ba_ref[:, sl], wx_ref[n],
                       bx_ref[:, sl], lam_ref[:, sl], (ab_ref.at[n], hs_ref.at[n]))
    o_ref[:, sl] = out.astype(BF16)
    px_ref[:, sl] = x[t - SUBLANES:]
    ph_ref[:, sl] = h[t - SUBLANES:]
    conv_ref[0, :, sl] = x[t - SUBLANES:]
    h_ref[0, :, sl] = h[t - SUBLANES:]


def _mix_prompt_kernel(q_ref, ka_ref, kb_ref, kc_ref, va_ref, vb_ref, vc_ref, ext_ref,
                       x_ref, g_ref, cw_ref, cb_ref, wa_ref, ba_ref, wx_ref, bx_ref, lam_ref,
                       c_ref, wm_ref, bm_ref,
                       attn_ref, lru_ref, conv_ref, h_ref, table_ref,
                       bias_ref, px_ref, ph_ref, ab_ref, hs_ref, *, scale, mod_dims):
    i = pl.program_id(2)

    @pl.when(i == 0)
    def _():
        _attn_prompt_bias(ext_ref, bias_ref, q_ref.shape[0])
        px_ref[...] = jnp.zeros_like(px_ref)
        ph_ref[...] = jnp.zeros_like(ph_ref)

    _mod_kernel(c_ref, wm_ref, bm_ref, table_ref, **mod_dims)

    for n in range(q_ref.shape[1] // HEAD_DIM):
        _attn_prompt_head(n, i, q_ref, (ka_ref, kb_ref, kc_ref), (va_ref, vb_ref, vc_ref),
                          bias_ref, attn_ref, scale)
        _lru_prompt_block(n, x_ref, g_ref, cw_ref, cb_ref, wa_ref, ba_ref, wx_ref, bx_ref, lam_ref,
                          lru_ref, conv_ref, h_ref, px_ref, ph_ref, ab_ref, hs_ref)


def _mix_prompt(z, rel_bias, lw, c_all, w_mod, b_mod, mod_col0, mod_dims,
                n_batch, seq, n_heads, col_x, col_g):
    conv_w, conv_b, w_a, b_a, w_x, b_x, lam = lw
    n_blocks = w_a.shape[0]
    tt = 256
    nb = 4
    lanes = nb * HEAD_DIM
    assert HEAD_DIM == LRU_BLOCK and n_heads == n_blocks and n_heads % nb == 0
    assert tt == (N_PREV_CHUNKS * CHUNK) // 2 and seq % tt == 0
    assert col_x % lanes == 0 and col_g % lanes == 0
    nt = seq // tt
    ng = n_heads // nb
    width = n_heads * HEAD_DIM
    ext = _bias_ramp(rel_bias, 3 * tt - 1, 4 * tt)
    cx, cg = col_x // lanes, col_g // lanes
    vec = lambda a: a.reshape(1, width)

    steps = n_batch * ng * nt
    d_model, mod_cols = w_mod.shape
    mtn = (mod_cols - mod_col0) // steps
    assert mtn % LANES == 0 and mtn * steps == mod_cols - mod_col0 and mod_col0 % mtn == 0
    n_groups = mod_dims["n_prompt"] * mod_dims["groups_per_prompt"] + mod_dims["n_sample"]
    step = lambda b, g, t: (b * ng + g) * nt + t

    def rows(back, col0):
        return pl.BlockSpec((tt, lanes),
                            lambda b, g, t: (b * nt + jnp.maximum(t - back, 0), col0 + g))

    vspec = pl.BlockSpec((1, lanes), lambda b, g, t: (0, g))
    wspec = pl.BlockSpec((nb, LRU_BLOCK, LRU_BLOCK), lambda b, g, t: (g, 0, 0))
    sspec = pl.BlockSpec((1, SUBLANES, lanes), lambda b, g, t: (b, 0, g))
    return pl.pallas_call(
        functools.partial(_mix_prompt_kernel, scale=HEAD_DIM ** -0.5, mod_dims=mod_dims),
        out_shape=(
            jax.ShapeDtypeStruct((n_batch * seq, width), BF16),
            jax.ShapeDtypeStruct((n_batch * seq, width), BF16),
            jax.ShapeDtypeStruct((n_batch, SUBLANES, width), F32),
            jax.ShapeDtypeStruct((n_batch, SUBLANES, width), F32),
            jax.ShapeDtypeStruct((n_groups, mod_cols - mod_col0), F32),
        ),
        grid=(n_batch, ng, nt),
        in_specs=[
            rows(0, 0),
            rows(2, ng), rows(1, ng), rows(0, ng),
            rows(2, 2 * ng), rows(1, 2 * ng), rows(0, 2 * ng),
            pl.BlockSpec((nb, 1, 4 * tt), lambda b, g, t: (g, 0, 0)),
            rows(0, cx), rows(0, cg),
            pl.BlockSpec((conv_w.shape[0], lanes), lambda b, g, t: (0, g)),
            vspec, wspec, vspec, wspec, vspec, vspec,
            pl.BlockSpec(c_all.shape, lambda b, g, t: (0, 0)),
            pl.BlockSpec((d_model, mtn), lambda b, g, t: (0, mod_col0 // mtn + step(b, g, t))),
            pl.BlockSpec((1, mtn), lambda b, g, t: (0, mod_col0 // mtn + step(b, g, t))),
        ],
        out_specs=(rows(0, 0), rows(0, 0), sspec, sspec,
                   pl.BlockSpec((n_groups, mtn), lambda b, g, t: (0, step(b, g, t)))),
        scratch_shapes=[pltpu.VMEM((nb, 3, tt, tt), F32),
                        pltpu.VMEM((SUBLANES, lanes), F32), pltpu.VMEM((SUBLANES, lanes), F32),
                        pltpu.VMEM((nb, 2, tt, LRU_BLOCK), F32),
                        pltpu.VMEM((nb, tt // SUBLANES, LRU_BLOCK), F32)],
        compiler_params=_params(("arbitrary", "arbitrary", "arbitrary"), 40),
        name="mix_prompt",
    )(z, z, z, z, z, z, z, ext, z, z, conv_w, vec(conv_b), w_a, vec(b_a), w_x, vec(b_x), vec(lam),
      c_all, w_mod, b_mod)


def _lru_sample_kernel(x_ref, g_ref, prev_ref, h0_ref, cw_ref, cb_ref, wa_ref, ba_ref, wx_ref,
                       bx_ref, lam_ref, o_ref, conv_ref, h_ref, *, n_blocks):
    t = x_ref.shape[0]
    for n in range(n_blocks):
        sl = slice(n * LRU_BLOCK, (n + 1) * LRU_BLOCK)
        x = x_ref[:, sl]
        out, h = _lru_tile(x, prev_ref[0, :, sl], h0_ref[0, :, sl], g_ref[:, sl],
                           cw_ref[:, sl], cb_ref[:, sl], wa_ref[n], ba_ref[:, sl], wx_ref[n],
                           bx_ref[:, sl], lam_ref[:, sl])
        o_ref[:, sl] = out.astype(BF16)
        conv_ref[0, :, sl] = x[t - SUBLANES:]
        h_ref[0, :, sl] = h[t - SUBLANES:]


def _lru_sample(z, state_conv, state_h, lw, row0, n_batch, t, col_x, col_g):
    conv_w, conv_b, w_a, b_a, w_x, b_x, lam = lw
    n_blocks = w_a.shape[0]
    width = n_blocks * LRU_BLOCK
    k = conv_w.shape[0]
    assert t >= SUBLANES and k - 1 <= SUBLANES
    prev = jnp.pad(state_conv, ((0, 0), (SUBLANES - (k - 1), 0), (0, 0)))
    h0 = state_h.reshape(n_batch, 1, width)
    rb0 = row0 // t
    cx, cg = col_x // width, col_g // width
    vec = lambda a: a.reshape(1, width)
    vspec = pl.BlockSpec((1, width), lambda b: (0, 0))
    wspec = pl.BlockSpec((n_blocks, LRU_BLOCK, LRU_BLOCK), lambda b: (0, 0, 0))
    sspec = pl.BlockSpec((1, SUBLANES, width), lambda b: (b, 0, 0))
    return pl.pallas_call(
        functools.partial(_lru_sample_kernel, n_blocks=n_blocks),
        out_shape=(
            jax.ShapeDtypeStruct((n_batch * t, width), BF16),
            jax.ShapeDtypeStruct((n_batch, SUBLANES, width), F32),
            jax.ShapeDtypeStruct((n_batch, SUBLANES, width), F32),
        ),
        grid=(n_batch,),
        in_specs=[
            pl.BlockSpec((t, width), lambda b: (rb0 + b, cx)),
            pl.BlockSpec((t, width), lambda b: (rb0 + b, cg)),
            sspec,
            pl.BlockSpec((1, 1, width), lambda b: (b, 0, 0)),
            pl.BlockSpec((k, width), lambda b: (0, 0)),
            vspec, wspec, vspec, wspec, vspec, vspec,
        ],
        out_specs=(pl.BlockSpec((t, width), lambda b: (b, 0)), sspec, sspec),
        compiler_params=_params(("arbitrary",), 32),
        name="lru_sample",
    )(z, z, prev, h0, conv_w, vec(conv_b), w_a, vec(b_a), w_x, vec(b_x), vec(lam))


def _out_proj_kernel(ap_ref, lp_ref, as_ref, ls_ref, w_ref, xp_ref, xs_ref, gp_ref, gs_ref,
                     op_ref, os_ref):
    i = pl.program_id(1)
    ka = ap_ref.shape[1]

    def tile(a_ref, l_ref, x_ref, g_ref, o_ref):
        acc = (jnp.dot(a_ref[...], w_ref[:ka, :], preferred_element_type=F32)
               + jnp.dot(l_ref[...], w_ref[ka:, :], preferred_element_type=F32))
        o_ref[...] = x_ref[...] + _group_scale(acc, g_ref[...])

    _per_row_group(i,
                   lambda: tile(ap_ref, lp_ref, xp_ref, gp_ref, op_ref),
                   lambda: tile(as_ref, ls_ref, xs_ref, gs_ref, os_ref))


def _out_proj(attn_p, lru_p, attn_s, lru_s, w_out, x_p, x_s, table, gate_col, group):
    d, n = w_out.shape
    ka, kl = attn_p.shape[1], lru_p.shape[1]
    assert ka + kl == d and w_out.dtype == BF16
    tn = min(1024, n)
    rg = _RowGroups(x_p.shape[0], x_s.shape[0], tile=512)
    gcol = lambda j: gate_col * (n // tn) + j
    sample_table_block = (rg.rows_p // group) // (rg.rows_s // group)
    return pl.pallas_call(
        _out_proj_kernel,
        out_shape=(jax.ShapeDtypeStruct((rg.rows_p, n), F32), jax.ShapeDtypeStruct((rg.rows_s, n), F32)),
        grid=(n // tn, rg.steps),
        in_specs=[
            rg.prompt_spec(ka, lambda j: 0),
            rg.prompt_spec(kl, lambda j: 0),
            rg.sample_spec(ka, lambda j: 0, resident=True),
            rg.sample_spec(kl, lambda j: 0, resident=True),
            pl.BlockSpec((d, tn), lambda j, i: (0, j)),
            rg.prompt_spec(tn, lambda j: j),
            rg.sample_spec(tn, lambda j: j),
            rg.prompt_spec(tn, gcol, rows_per_row=group),
            rg.sample_spec(tn, gcol, rows_per_row=group, row_block=sample_table_block),
        ],
        out_specs=(rg.prompt_spec(tn, lambda j: j, output=True), rg.sample_spec(tn, lambda j: j)),
        compiler_params=_params(("arbitrary", "arbitrary"), 56),
        name="out_proj",
    )(attn_p, lru_p, attn_s, lru_s, w_out, x_p, x_s, table, table)


def _ffn_up_kernel(xp_ref, xs_ref, wg_hbm, wu_hbm, wd_ref, op_ref, os_ref, wdb_ref,
                   stage_ref, wb_ref, sem, *, n_cols, n_wd_chunks):
    j = pl.program_id(0)
    i = pl.program_id(1)
    tn = wb_ref.shape[2]
    n_full, rem = divmod(n_cols, tn)

    rc = wd_ref.shape[0]
    chunk = j * pl.num_programs(1) + i

    @pl.when(chunk < n_wd_chunks)
    def _():
        row = lax.broadcasted_iota(jnp.int32, wd_ref.shape, 0)
        wdb_ref[...] = jnp.where(chunk * rc + row < n_cols, wd_ref[...], 0.0).astype(BF16)

    def copies(jj, width):
        col = pl.multiple_of(jj * tn, tn)
        return [pltpu.make_async_copy(src.at[:, pl.ds(col, width)],
                                      stage_ref.at[w, :, pl.ds(0, width)], sem.at[w])
                for w, src in enumerate((wg_hbm, wu_hbm))]

    def start(jj):
        @pl.when(jj < n_full)
        def _():
            for c in copies(jj, tn):
                c.start()

        if rem:
            @pl.when(jj == n_full)
            def _():
                for c in copies(jj, rem):
                    c.start()

    @pl.when((j == 0) & (i == 0))
    def _():
        start(j)

    @pl.when(i == 0)
    def _():
        @pl.when(j < n_full)
        def _():
            for c in copies(j, tn):
                c.wait()
            wb_ref[...] = stage_ref[...].astype(BF16)

        if rem:
            @pl.when(j == n_full)
            def _():
                for c in copies(j, rem):
                    c.wait()
                wb_ref[:, :, :rem] = stage_ref[:, :, :rem].astype(BF16)
                wb_ref[:, :, rem:] = jnp.zeros((2, wb_ref.shape[1], tn - rem), BF16)

        @pl.when(j >= n_full + (1 if rem else 0))
        def _():
            wb_ref[...] = jnp.zeros_like(wb_ref)

    @pl.when((i == 1) & (j + 1 < pl.num_programs(0)))
    def _():
        start(j + 1)

    def tile(x_ref, o_ref):
        def hidden(width):
            x = x_ref[...]
            g = jnp.dot(x, wb_ref[0, :, :width], preferred_element_type=F32)
            u = jnp.dot(x, wb_ref[1, :, :width], preferred_element_type=F32)
            o_ref[:, :width] = (g * _sigmoid(g) * u).astype(BF16)

        if rem:
            pl.when(j != n_full)(lambda: hidden(tn))

            @pl.when(j == n_full)
            def _():
                hidden(rem)
                o_ref[:, rem:] = jnp.zeros((o_ref.shape[0], tn - rem), BF16)
        else:
            hidden(tn)

    _per_row_group(i, lambda: tile(xp_ref, op_ref), lambda: tile(xs_ref, os_ref))


def _ffn_up(u_p, u_s, w_gate, w_up, w_down, ff_pad):
    d, ff = w_gate.shape
    n_out = w_down.shape[1]
    tn = 512
    assert ff % LANES == 0 and ff_pad % tn == 0
    rg = _RowGroups(u_p.shape[0], u_s.shape[0])
    assert rg.steps >= 2
    steps = (ff_pad // tn) * rg.steps
    rc = next(r for r in range(16, ff_pad + 1, 16)
              if ff_pad % r == 0 and ff % r == 0 and ff_pad // r <= steps)
    n_wd_chunks, n_real_chunks = ff_pad // rc, ff // rc
    chunk_of = lambda j, i: j * rg.steps + i
    hbm = pl.BlockSpec(memory_space=pl.ANY)
    return pl.pallas_call(
        functools.partial(_ffn_up_kernel, n_cols=ff, n_wd_chunks=n_wd_chunks),
        out_shape=(jax.ShapeDtypeStruct((rg.rows_p, ff_pad), BF16),
                   jax.ShapeDtypeStruct((rg.rows_s, ff_pad), BF16),
                   jax.ShapeDtypeStruct((ff_pad, n_out), BF16)),
        grid=(ff_pad // tn, rg.steps),
        in_specs=[rg.prompt_spec(d, lambda j: 0), rg.sample_spec(d, lambda j: 0, resident=True),
                  hbm, hbm,
                  pl.BlockSpec((rc, n_out),
                               lambda j, i: (jnp.minimum(chunk_of(j, i), n_real_chunks - 1), 0))],
        out_specs=(rg.prompt_spec(tn, lambda j: j, output=True), rg.sample_spec(tn, lambda j: j),
                   pl.BlockSpec((rc, n_out),
                                lambda j, i: (jnp.minimum(chunk_of(j, i), n_wd_chunks - 1), 0))),
        scratch_shapes=[pltpu.VMEM((2, d, tn), F32), pltpu.VMEM((2, d, tn), BF16),
                        pltpu.SemaphoreType.DMA((2,))],
        compiler_params=_params(("arbitrary", "arbitrary"), 62),
        name="ffn_up",
    )(u_p, u_s, w_gate, w_up, w_down)


def _ffn_down_kernel(h_ref, w_ref, x_ref, g_ref, o_ref):
    k = pl.program_id(2)
    last = pl.num_programs(2) - 1
    tm = o_ref.shape[0]
    rows = min(tm, 512)
    group = tm // g_ref.shape[0]

    def chunks():
        for r0 in range(0, tm, rows):
            sl = slice(r0, r0 + rows)
            yield sl, slice(r0 // group, (r0 + rows) // group), jnp.dot(
                h_ref[sl, :], w_ref[...], preferred_element_type=F32)

    @pl.when(k == 0)
    def _():
        for sl, _, part in chunks():
            o_ref[sl, :] = part

    @pl.when((k > 0) & (k < last))
    def _():
        for sl, _, part in chunks():
            o_ref[sl, :] += part

    @pl.when(k == last)
    def _():
        for sl, gsl, part in chunks():
            o_ref[sl, :] = x_ref[sl, :] + _group_scale(o_ref[sl, :] + part, g_ref[gsl, :])


def _ffn_down(h, w_down, x1, table, gate_col, group, group0, tm, tn):
    rows, ff_pad = h.shape
    n = w_down.shape[1]
    tk = 1024
    tn = min(tn, n)
    gt = tm // group
    assert w_down.shape[0] == ff_pad and w_down.dtype == BF16 and ff_pad // tk >= 2
    assert rows % tm == 0 and ff_pad % tk == 0 and n % tn == 0 and group0 % gt == 0
    g0 = group0 // gt
    gcol0 = gate_col * (n // tn)
    return pl.pallas_call(
        _ffn_down_kernel,
        out_shape=jax.ShapeDtypeStruct((rows, n), F32),
        grid=(rows // tm, n // tn, ff_pad // tk),
        in_specs=[
            pl.BlockSpec((tm, tk), lambda i, j, k: (i, k)),
            pl.BlockSpec((tk, tn), lambda i, j, k: (k, j)),
            pl.BlockSpec((tm, tn), lambda i, j, k: (i, j)),
            pl.BlockSpec((gt, tn), lambda i, j, k: (g0 + i, gcol0 + j)),
        ],
        out_specs=pl.BlockSpec((tm, tn), lambda i, j, k: (i, j)),
        compiler_params=_params(("arbitrary", "arbitrary", "arbitrary"), 58),
        name="ffn_down",
    )(h, w_down, x1, table)


def _layer(x_p, x_s, c_prompt, c_sample, cache_k, cache_v, state_conv, state_h, lw, dims):
    (norm_mix_g, norm_ffn_g, w_mod, b_mod, w_in, q_norm_g, k_norm_g, rel_bias,
     conv_w, conv_b, w_rg_a, b_rg_a, w_rg_x, b_rg_x, lru_lambda, w_out,
     w_ffn_gate, w_ffn_up, w_ffn_down) = lw
    n_prompt, seq, n_sample, t_s = dims
    group = t_s
    n_heads = rel_bias.shape[0]
    attn_w = n_heads * HEAD_DIM
    lru_w = w_rg_a.shape[0] * LRU_BLOCK
    rows_p = n_prompt * seq
    rows_s = n_sample * t_s
    groups_p = rows_p // group

    d = x_p.shape[1]
    c_all = _mod_inputs(c_prompt, c_sample)
    b_mod2 = b_mod.reshape(1, -1)
    mod_dims = dict(n_sample=n_sample, n_prompt=n_prompt, groups_per_prompt=seq // group)
    table1 = _mod_table(c_all, w_mod, b_mod2, 2 * d, **mod_dims)
    u_p = _norm_mod(x_p, norm_mix_g, table1, 0, 1, group, 0)
    u_s = _norm_mod(x_s, norm_mix_g, table1, 0, 1, group, groups_p)
    qk_gain = jnp.concatenate([jnp.tile(q_norm_g, n_heads), jnp.tile(k_norm_g, n_heads)]).reshape(1, -1)
    z_p, z_s, w_out_b = _in_proj(u_p, u_s, w_in, qk_gain, attn_w, w_out)

    lru_params = (conv_w, conv_b, w_rg_a, b_rg_a, w_rg_x, b_rg_x, lru_lambda)
    col_x, col_g = 3 * attn_w, 3 * attn_w + lru_w
    attn_p, lru_p, conv_p, h_p, table2 = _mix_prompt(
        z_p, rel_bias, lru_params, c_all, w_mod, b_mod2, 2 * d, mod_dims,
        n_prompt, seq, n_heads, col_x, col_g)
    attn_s = _attn_sample(z_s, cache_k, cache_v, rel_bias, 0, n_sample, t_s, n_heads)
    lru_s, conv_s, h_s = _lru_sample(z_s, state_conv, state_h, lru_params, 0, n_sample, t_s,
                                     col_x, col_g)

    x1_p, x1_s = _out_proj(attn_p, lru_p, attn_s, lru_s, w_out_b, x_p, x_s, table2, 0, group)
    u2_p = _norm_mod(x1_p, norm_ffn_g, table2, 1, 2, group, 0)
    u2_s = _norm_mod(x1_s, norm_ffn_g, table2, 1, 2, group, groups_p)
    ff = w_ffn_gate.shape[1]
    ff_pad = -(-ff // 1024) * 1024
    hid_p, hid_s, w_down_b = _ffn_up(u2_p, u2_s, w_ffn_gate, w_ffn_up, w_ffn_down, ff_pad)
    y_p = _ffn_down(hid_p, w_down_b, x1_p, table2, 3, group, 0, 2048, 1024)
    y_s = _ffn_down(hid_s, w_down_b, x1_s, table2, 3, group, groups_p, rows_s, 2048)

    keep = min(N_PREV_CHUNKS * CHUNK, seq)
    kv_p = z_p.reshape(n_prompt, seq, -1)[:, seq - keep:, attn_w:3 * attn_w]
    kv_p = kv_p.reshape(n_prompt, keep, 2, n_heads, HEAD_DIM)
    kv_s = z_s[:, attn_w:3 * attn_w].reshape(n_sample, t_s, 2, n_heads, HEAD_DIM)
    k_conv = conv_w.shape[0] - 1
    state = dict(
        k_p=kv_p[:, :, 0], v_p=kv_p[:, :, 1], conv_p=conv_p[:, SUBLANES - k_conv:], h_p=h_p[:, SUBLANES - 1],
        k_s=kv_s[:, :, 0], v_s=kv_s[:, :, 1], conv_s=conv_s[:, SUBLANES - k_conv:], h_s=h_s[:, SUBLANES - 1])
    return y_p, y_s, state


def kernel(x_prompt, x_sample, cache_k, cache_v, state_conv, state_h, c_prompt, c_sample, norm_mix_g, norm_ffn_g, w_mod, b_mod, w_in, q_norm_g, k_norm_g, rel_bias, conv_w, conv_b, w_rg_a, b_rg_a, w_rg_x, b_rg_x, lru_lambda, w_out, w_ffn_gate, w_ffn_up, w_ffn_down):
    n_prompt, seq, d = x_prompt.shape
    n_sample, t_s, _ = x_sample.shape
    depth = w_in.shape[0]
    dims = (n_prompt, seq, n_sample, t_s)
    yp = x_prompt.reshape(n_prompt * seq, d)
    ys = x_sample.reshape(n_sample * t_s, d)
    states = []
    for l in range(depth):
        lw = (norm_mix_g[l], norm_ffn_g[l], w_mod[l], b_mod[l], w_in[l], q_norm_g[l], k_norm_g[l],
              rel_bias[l], conv_w[l], conv_b[l], w_rg_a[l], b_rg_a[l], w_rg_x[l], b_rg_x[l],
              lru_lambda[l], w_out[l], w_ffn_gate[l], w_ffn_up[l], w_ffn_down[l])
        yp, ys, st = _layer(yp, ys, c_prompt, c_sample, cache_k[l], cache_v[l], state_conv[l],
                            state_h[l], lw, dims)
        states.append(st)
    stack = lambda name: jnp.stack([s[name] for s in states])
    return (yp.reshape(n_prompt, seq, d), ys.reshape(n_sample, t_s, d),
            stack("k_p"), stack("v_p"), stack("conv_p"), stack("h_p"),
            stack("k_s"), stack("v_s"), stack("conv_s"), stack("h_s"))
```

```python
import functools
import math

import jax
import jax.numpy as jnp
from jax import lax
from jax.experimental import pallas as pl
from jax.experimental.pallas import tpu as pltpu

F32 = jnp.float32
BF16 = jnp.bfloat16

CHUNK = 64
N_PREV_CHUNKS = 8
HEAD_DIM = 128
LRU_BLOCK = 128
LRU_C = 8.0
NEG_INF = -1e30
EPS = 1e-6
LOG2E = math.log2(math.e)

LANES = 128
SUBLANES = 8
MIB = 1024 * 1024


def _params(semantics, vmem_mib):
    return pltpu.CompilerParams(dimension_semantics=semantics, vmem_limit_bytes=vmem_mib * MIB)


def _sigmoid(x):
    return 0.5 * jnp.tanh(0.5 * x) + 0.5


def _gelu_tanh(x):
    c = math.sqrt(2.0 / math.pi)
    return x * (0.5 * (1.0 + jnp.tanh(c * (x + 0.044715 * (x * x * x)))))


def _group_scale(y, g):
    rows, n = y.shape
    groups = g.shape[0]
    return (y.reshape(groups, rows // groups, n) * g[:, None, :]).reshape(rows, n)


PROMPT_TILE_ROWS = 1024


class _RowGroups:
    def __init__(self, rows_p, rows_s, tile=PROMPT_TILE_ROWS):
        assert rows_p % tile == 0
        self.rows_p, self.rows_s, self.tile = rows_p, rows_s, tile
        self.np_tiles = rows_p // tile
        self.steps = self.np_tiles + 1

    def prompt_spec(self, cols, col_block, rows_per_row=1, output=False):
        first = 0 if output else self.np_tiles - 1
        return pl.BlockSpec((self.tile // rows_per_row, cols),
                            lambda j, i: (jnp.where(i == 0, first, i - 1), col_block(j)))

    def sample_spec(self, cols, col_block, rows_per_row=1, row_block=0, resident=False):
        mode = dict(pipeline_mode=pl.Buffered(1)) if resident else {}
        return pl.BlockSpec((self.rows_s // rows_per_row, cols),
                            lambda j, i: (row_block, col_block(j)), **mode)


def _per_row_group(i, prompt_fn, sample_fn):
    pl.when(i == 0)(sample_fn)
    pl.when(i > 0)(prompt_fn)


def _mod_kernel(c_ref, w_ref, b_ref, o_ref, *, n_sample, n_prompt, groups_per_prompt):
    c = c_ref[...]
    s = (c * _sigmoid(c)).astype(BF16)
    m = jnp.dot(s, w_ref[...].astype(BF16), preferred_element_type=F32) + b_ref[...]
    tn = m.shape[1]
    for b in range(n_prompt):
        row = m[n_sample + b:n_sample + b + 1, :]
        o_ref[b * groups_per_prompt:(b + 1) * groups_per_prompt, :] = jnp.broadcast_to(
            row, (groups_per_prompt, tn))
    o_ref[n_prompt * groups_per_prompt:n_prompt * groups_per_prompt + n_sample, :] = m[:n_sample, :]


def _mod_inputs(c_prompt, c_sample):
    rows = c_sample.shape[0] + c_prompt.shape[0]
    rows_pad = -(-rows // SUBLANES) * SUBLANES
    return jnp.concatenate(
        [c_sample, c_prompt, jnp.zeros((rows_pad - rows, c_prompt.shape[1]), c_prompt.dtype)], axis=0)


def _mod_table(c_all, w_mod, b_mod, n_cols, n_sample, n_prompt, groups_per_prompt):
    rows_pad, d = c_all.shape
    n_groups = n_prompt * groups_per_prompt + n_sample
    tn = 512
    return pl.pallas_call(
        functools.partial(_mod_kernel, n_sample=n_sample, n_prompt=n_prompt,
                          groups_per_prompt=groups_per_prompt),
        out_shape=jax.ShapeDtypeStruct((n_groups, n_cols), F32),
        grid=(n_cols // tn,),
        in_specs=[
            pl.BlockSpec((rows_pad, d), lambda j: (0, 0)),
            pl.BlockSpec((d, tn), lambda j: (0, j)),
            pl.BlockSpec((1, tn), lambda j: (0, j)),
        ],
        out_specs=pl.BlockSpec((n_groups, tn), lambda j: (0, j)),
        compiler_params=_params(("arbitrary",), 40),
        name="mod_table",
    )(c_all, w_mod, b_mod)


def _norm_mod_kernel(x_ref, gain_ref, shift_ref, scale_ref, o_ref):
    x = x_ref[...]
    ms = jnp.mean(x * x, axis=-1, keepdims=True)
    y = x * lax.rsqrt(ms + EPS) * gain_ref[...]
    rows, d = y.shape
    groups = scale_ref.shape[0]
    y3 = y.reshape(groups, rows // groups, d)
    u = y3 * (1.0 + scale_ref[...][:, None, :]) + shift_ref[...][:, None, :]
    o_ref[...] = u.reshape(rows, d).astype(BF16)


def _norm_mod(x, gain, table, shift_col, scale_col, group, group0):
    rows, d = x.shape
    tm = 512
    gt = tm // group
    assert rows % tm == 0 and group0 % gt == 0
    g0 = group0 // gt
    return pl.pallas_call(
        _norm_mod_kernel,
        out_shape=jax.ShapeDtypeStruct((rows, d), BF16),
        grid=(rows // tm,),
        in_specs=[
            pl.BlockSpec((tm, d), lambda i: (i, 0)),
            pl.BlockSpec((1, d), lambda i: (0, 0)),
            pl.BlockSpec((gt, d), lambda i: (g0 + i, shift_col)),
            pl.BlockSpec((gt, d), lambda i: (g0 + i, scale_col)),
        ],
        out_specs=pl.BlockSpec((tm, d), lambda i: (i, 0)),
        compiler_params=_params(("arbitrary",), 40),
        name="norm_mod",
    )(x, gain.reshape(1, d), table, table)


def _row_chunk(rows, granule, max_chunks):
    return next(r for r in range(granule, rows + 1, granule)
                if rows % r == 0 and rows // r <= max_chunks)


def _in_proj_kernel(xp_ref, xs_ref, w_hbm, gain_ref, wo_ref, op_ref, os_ref, wob_ref,
                    stage_ref, wb_ref, sem, *, n_norm_tiles, n_wo_chunks):
    j = pl.program_id(0)
    i = pl.program_id(1)
    tn = wb_ref.shape[1]

    def copy(jj):
        return pltpu.make_async_copy(w_hbm.at[:, pl.ds(pl.multiple_of(jj * tn, tn), tn)],
                                     stage_ref, sem.at[0])

    @pl.when((j == 0) & (i == 0))
    def _():
        copy(j).start()

    @pl.when(i == 0)
    def _():
        copy(j).wait()
        wb_ref[...] = stage_ref[...].astype(BF16)

    @pl.when((i == 1) & (j + 1 < pl.num_programs(0)))
    def _():
        copy(j + 1).start()

    @pl.when(j * pl.num_programs(1) + i < n_wo_chunks)
    def _():
        wob_ref[...] = wo_ref[...].astype(BF16)

    def tile(x_ref, o_ref):
        rows, tn = o_ref.shape

        @pl.when(j < n_norm_tiles)
        def _():
            half = rows // 2
            accs = [jnp.dot(x_ref[r0:r0 + half, :], wb_ref[...], preferred_element_type=F32)
                    for r0 in (0, half)]
            for r0, acc in zip((0, half), accs):
                for g in range(tn // HEAD_DIM):
                    sl = slice(g * HEAD_DIM, (g + 1) * HEAD_DIM)
                    zg = acc[:, sl]
                    ms = jnp.mean(zg * zg, axis=-1, keepdims=True)
                    o_ref[r0:r0 + half, sl] = zg * lax.rsqrt(ms + EPS) * gain_ref[:, sl]

        @pl.when(j >= n_norm_tiles)
        def _():
            o_ref[...] = jnp.dot(x_ref[...], wb_ref[...], preferred_element_type=F32)

    _per_row_group(i, lambda: tile(xp_ref, op_ref), lambda: tile(xs_ref, os_ref))


def _in_proj(u_p, u_s, w_in, qk_gain, attn_width, w_out):
    d, n = w_in.shape
    tn = 1024
    assert n % tn == 0 and (2 * attn_width) % tn == 0
    n_norm_tiles = 2 * attn_width // tn
    rg = _RowGroups(u_p.shape[0], u_s.shape[0], tile=512)
    assert rg.steps >= 2
    ko, no = w_out.shape
    rc = _row_chunk(ko, 16, (n // tn) * rg.steps)
    n_wo_chunks = ko // rc
    wo_spec = pl.BlockSpec((rc, no), lambda j, i: (jnp.minimum(j * rg.steps + i, n_wo_chunks - 1), 0))
    return pl.pallas_call(
        functools.partial(_in_proj_kernel, n_norm_tiles=n_norm_tiles, n_wo_chunks=n_wo_chunks),
        out_shape=(jax.ShapeDtypeStruct((rg.rows_p, n), F32), jax.ShapeDtypeStruct((rg.rows_s, n), F32),
                   jax.ShapeDtypeStruct((ko, no), BF16)),
        grid=(n // tn, rg.steps),
        in_specs=[
            rg.prompt_spec(d, lambda j: 0),
            rg.sample_spec(d, lambda j: 0, resident=True),
            pl.BlockSpec(memory_space=pl.ANY),
            pl.BlockSpec((1, tn), lambda j, i: (0, jnp.minimum(j, n_norm_tiles - 1))),
            wo_spec,
        ],
        out_specs=(rg.prompt_spec(tn, lambda j: j, output=True), rg.sample_spec(tn, lambda j: j),
                   wo_spec),
        scratch_shapes=[pltpu.VMEM((d, tn), F32), pltpu.VMEM((d, tn), BF16),
                        pltpu.SemaphoreType.DMA((1,))],
        compiler_params=_params(("arbitrary", "arbitrary"), 58),
        name="in_proj",
    )(u_p, u_s, w_in, qk_gain, w_out)


def _bias_ramp(rel_bias, top, length):
    n_heads, n_rel = rel_bias.shape
    clip = (n_rel - 1) // 2
    lead = top - clip
    tail = length - lead - n_rel
    assert lead >= 0 and tail >= 0
    rev = rel_bias[:, ::-1]
    ext = jnp.concatenate([
        jnp.broadcast_to(rel_bias[:, n_rel - 1:], (n_heads, lead)),
        rev,
        jnp.broadcast_to(rel_bias[:, :1], (n_heads, tail)),
    ], axis=1)
    return ext.reshape(n_heads, 1, length)


def _lane_reduce(tiles, combine, reduce):
    parts = [t[:, c:c + LANES] for t in tiles for c in range(0, t.shape[1], LANES)]
    return reduce(functools.reduce(combine, parts), axis=-1, keepdims=True)


def _toeplitz(base, rows):
    w = base.shape[1]
    return pltpu.roll(jnp.broadcast_to(base, (rows, w)), w - (rows - 1), 1, stride=1, stride_axis=0)


def _attn_prompt_bias(ext_ref, bias_ref, tq):
    shift = CHUNK.bit_length() - 1
    rowc = jnp.right_shift(lax.broadcasted_iota(jnp.int32, (tq, tq), 0), shift)
    colc = jnp.right_shift(lax.broadcasted_iota(jnp.int32, (tq, tq), 1), shift)
    for h in range(bias_ref.shape[0]):
        for m in range(3):
            off = (2 - m) * tq
            t = _toeplitz(ext_ref[h, :, off:off + 2 * tq], tq)[:, :tq]
            if m == 0:
                t = jnp.where(colc <= rowc, t, NEG_INF)
            if m == 2:
                t = jnp.where(rowc <= colc, t, NEG_INF)
            bias_ref[h, m] = t * LOG2E


def _attn_prompt_head(h, i, q_ref, k_refs, v_refs, bias_ref, o_ref, scale):
    dn = (((1,), (1,)), ((), ()))
    sl = slice(h * HEAD_DIM, (h + 1) * HEAD_DIM)
    q = q_ref[:, sl].astype(BF16)
    scores = []
    for m, k_ref in zip((2, 1, 0), k_refs):
        s = lax.dot_general(q, k_ref[:, sl].astype(BF16), dn, preferred_element_type=F32)
        s = s * (scale * LOG2E) + bias_ref[h, m]
        if m > 0:
            s = s + jnp.where(i >= m, 0.0, NEG_INF)
        scores.append(s)
    mx = _lane_reduce(scores, jnp.maximum, jnp.max)
    probs = [jnp.exp2(s - mx) for s in scores]
    denom = _lane_reduce(probs, jnp.add, jnp.sum)
    out = functools.reduce(jnp.add, [
        jnp.dot(p.astype(BF16), v_ref[:, sl].astype(BF16), preferred_element_type=F32)
        for p, v_ref in zip(probs, v_refs)])
    o_ref[:, sl] = (out / denom).astype(BF16)


def _attn_sample_kernel(q_ref, kn_ref, vn_ref, ck_ref, cv_ref, ext_ref, o_ref, *, scale, n_heads):
    t = q_ref.shape[0]
    r = ck_ref.shape[1] // n_heads
    dn = (((1,), (1,)), ((), ()))
    for h in range(n_heads):
        sl = slice(h * HEAD_DIM, (h + 1) * HEAD_DIM)
        rows_h = pl.ds(h, r, stride=n_heads)
        bias = _toeplitz(ext_ref[h], t)
        q = q_ref[:, sl].astype(BF16)
        s1 = lax.dot_general(q, ck_ref[0, rows_h, :].astype(BF16), dn, preferred_element_type=F32)
        s1 = s1 * scale + bias[:, :r]
        s2 = lax.dot_general(q, kn_ref[:, sl].astype(BF16), dn, preferred_element_type=F32)
        s2 = s2 * scale + bias[:, r:r + t]
        mx = jnp.maximum(jnp.max(s1, axis=-1, keepdims=True), jnp.max(s2, axis=-1, keepdims=True))
        p1 = jnp.exp(s1 - mx)
        p2 = jnp.exp(s2 - mx)
        denom = jnp.sum(p1, axis=-1, keepdims=True) + jnp.sum(p2, axis=-1, keepdims=True)
        out = (jnp.dot(p1.astype(BF16), cv_ref[0, rows_h, :].astype(BF16), preferred_element_type=F32)
               + jnp.dot(p2.astype(BF16), vn_ref[:, sl].astype(BF16), preferred_element_type=F32))
        o_ref[:, sl] = (out / denom).astype(BF16)


def _attn_sample(z, cache_k, cache_v, rel_bias, row0, n_batch, t, n_heads):
    r = cache_k.shape[1]
    width = n_heads * HEAD_DIM
    ck = cache_k.reshape(n_batch, r * n_heads, HEAD_DIM)
    cv = cache_v.reshape(n_batch, r * n_heads, HEAD_DIM)
    top = r + t - 1
    clip = (rel_bias.shape[1] - 1) // 2
    ramp = -(-max(r + 2 * t, top + clip + 1) // LANES) * LANES
    ext = _bias_ramp(rel_bias, top, ramp)
    rb0 = row0 // t
    return pl.pallas_call(
        functools.partial(_attn_sample_kernel, scale=HEAD_DIM ** -0.5, n_heads=n_heads),
        out_shape=jax.ShapeDtypeStruct((n_batch * t, width), BF16),
        grid=(n_batch,),
        in_specs=[
            pl.BlockSpec((t, width), lambda b: (rb0 + b, 0)),
            pl.BlockSpec((t, width), lambda b: (rb0 + b, 1)),
            pl.BlockSpec((t, width), lambda b: (rb0 + b, 2)),
            pl.BlockSpec((1, r * n_heads, HEAD_DIM), lambda b: (b, 0, 0)),
            pl.BlockSpec((1, r * n_heads, HEAD_DIM), lambda b: (b, 0, 0)),
            pl.BlockSpec((n_heads, 1, ramp), lambda b: (0, 0, 0)),
        ],
        out_specs=pl.BlockSpec((t, width), lambda b: (b, 0)),
        compiler_params=_params(("arbitrary",), 40),
        name="attn_sample",
    )(z, z, z, ck, cv, ext)


def _lru_tile(x, prev, h_in, gate_in, cw, cb, wa, ba, wx, bx, lam, scan_scratch=None):
    t = x.shape[0]
    p = prev.shape[0]
    k = cw.shape[0]
    xcat = jnp.concatenate([prev, x], axis=0)
    xc = cb
    for i in range(k):
        lo = p - (k - 1) + i
        xc = xc + xcat[lo:lo + t] * cw[i:i + 1]
    xcb = xc.astype(BF16)
    rg = _sigmoid(jnp.dot(xcb, wa.astype(BF16), preferred_element_type=F32) + ba)
    ig = _sigmoid(jnp.dot(xcb, wx.astype(BF16), preferred_element_type=F32) + bx)
    neg = -lam
    softplus = jnp.maximum(neg, 0.0) + jnp.log1p(jnp.exp(-jnp.abs(neg)))
    log_a = (-LRU_C) * rg * softplus
    a = jnp.exp(log_a)
    var = -jnp.tanh(log_a) * (a * a + 1.0)
    std = jnp.where(var == 0.0, 0.0, var * lax.rsqrt(var))
    b = std * (ig * xc)
    h = _linear_scan(a, b, h_in, scan_scratch)
    return h * _gelu_tanh(gate_in), h


def _doubling_scan(a, b, axis):
    n = a.shape[axis]
    idx = lax.broadcasted_iota(jnp.int32, a.shape, axis)
    step = 1
    while step < n:
        keep = idx >= step
        a_prev = jnp.where(keep, pltpu.roll(a, step, axis), 1.0)
        b_prev = jnp.where(keep, pltpu.roll(b, step, axis), 0.0)
        b = a * b_prev + b
        a = a * a_prev
        step *= 2
    return a, b


def _linear_scan(a, b, h_in, scratch=None):
    t, c = a.shape
    groups = t // SUBLANES
    a3, b3 = _doubling_scan(a.reshape(groups, SUBLANES, c), b.reshape(groups, SUBLANES, c), 1)
    if scratch is not None:
        ab_ref, hs_ref = scratch
        a2, b2 = a3.reshape(t, c), b3.reshape(t, c)
        ab_ref[0] = a2
        ab_ref[1] = b2
        ends = pl.ds(SUBLANES - 1, groups, stride=SUBLANES)
        a_cum, b_cum = _doubling_scan(ab_ref[0, ends, :], ab_ref[1, ends, :], 0)
        h_end = a_cum * h_in + b_cum
        first = lax.broadcasted_iota(jnp.int32, h_end.shape, 0) == 0
        hs_ref[...] = jnp.where(first, h_in, pltpu.roll(h_end, 1, 0))
        h_start = jnp.concatenate(
            [jnp.broadcast_to(hs_ref[g:g + 1, :], (SUBLANES, c)) for g in range(groups)], axis=0)
        return a2 * h_start + b2
    ae = jnp.broadcast_to(a3[:, SUBLANES - 1:, :], a3.shape).reshape(t, c)
    be = jnp.broadcast_to(b3[:, SUBLANES - 1:, :], b3.shape).reshape(t, c)
    step = SUBLANES
    while step < t:
        be = jnp.concatenate([be[:step], ae[step:] * be[:-step] + be[step:]], axis=0)
        ae = jnp.concatenate([ae[:step], ae[step:] * ae[:-step]], axis=0)
        step *= 2
    h_end = ae * h_in + be
    h_start = jnp.concatenate([jnp.broadcast_to(h_in, (SUBLANES, c)), h_end[:t - SUBLANES]], axis=0)
    return a3.reshape(t, c) * h_start + b3.reshape(t, c)


def _lru_prompt_block(n, x_ref, g_ref, cw_ref, cb_ref, wa_ref, ba_ref, wx_ref, bx_ref, lam_ref,
                      o_ref, conv_ref, h_ref, px_ref, ph_ref, ab_ref, hs_ref):
    t = x_ref.shape[0]
    sl = slice(n * LRU_BLOCK, (n + 1) * LRU_BLOCK)
    x = x_ref[:, sl]
    out, h = _lru_tile(x, px_ref[:, sl], ph_ref[SUBLANES - 1:SUBLANES, sl], g_ref[:, sl],
                       cw_ref[:, sl], cb_ref[:, sl], wa_ref[n], ba_ref[:, sl], wx_ref[n],
                       bx_ref[:, sl], lam_ref[:, sl], (ab_ref.at[n], hs_ref.at[n]))
    o_ref[:, sl] = out.astype(BF16)
    px_ref[:, sl] = x[t - SUBLANES:]
    ph_ref[:, sl] = h[t - SUBLANES:]
    conv_ref[0, :, sl] = x[t - SUBLANES:]
    h_ref[0, :, sl] = h[t - SUBLANES:]


def _mix_prompt_kernel(q_ref, ka_ref, kb_ref, kc_ref, va_ref, vb_ref, vc_ref, ext_ref,
                       x_ref, g_ref, cw_ref, cb_ref, wa_ref, ba_ref, wx_ref, bx_ref, lam_ref,
                       c_ref, wm_ref, bm_ref,
                       attn_ref, lru_ref, conv_ref, h_ref, table_ref,
                       bias_ref, px_ref, ph_ref, ab_ref, hs_ref, *, scale, mod_dims):
    i = pl.program_id(2)

    @pl.when(i == 0)
    def _():
        _attn_prompt_bias(ext_ref, bias_ref, q_ref.shape[0])
        px_ref[...] = jnp.zeros_like(px_ref)
        ph_ref[...] = jnp.zeros_like(ph_ref)

    _mod_kernel(c_ref, wm_ref, bm_ref, table_ref, **mod_dims)

    for n in range(q_ref.shape[1] // HEAD_DIM):
        _attn_prompt_head(n, i, q_ref, (ka_ref, kb_ref, kc_ref), (va_ref, vb_ref, vc_ref),
                          bias_ref, attn_ref, scale)
        _lru_prompt_block(n, x_ref, g_ref, cw_ref, cb_ref, wa_ref, ba_ref, wx_ref, bx_ref, lam_ref,
                          lru_ref, conv_ref, h_ref, px_ref, ph_ref, ab_ref, hs_ref)


def _mix_prompt(z, rel_bias, lw, c_all, w_mod, b_mod, mod_col0, mod_dims,
                n_batch, seq, n_heads, col_x, col_g):
    conv_w, conv_b, w_a, b_a, w_x, b_x, lam = lw
    n_blocks = w_a.shape[0]
    tt = 256
    nb = 4
    lanes = nb * HEAD_DIM
    assert HEAD_DIM == LRU_BLOCK and n_heads == n_blocks and n_heads % nb == 0
    assert tt == (N_PREV_CHUNKS * CHUNK) // 2 and seq % tt == 0
    assert col_x % lanes == 0 and col_g % lanes == 0
    nt = seq // tt
    ng = n_heads // nb
    width = n_heads * HEAD_DIM
    ext = _bias_ramp(rel_bias, 3 * tt - 1, 4 * tt)
    cx, cg = col_x // lanes, col_g // lanes
    vec = lambda a: a.reshape(1, width)

    steps = n_batch * ng * nt
    d_model, mod_cols = w_mod.shape
    mtn = (mod_cols - mod_col0) // steps
    assert mtn % LANES == 0 and mtn * steps == mod_cols - mod_col0 and mod_col0 % mtn == 0
    n_groups = mod_dims["n_prompt"] * mod_dims["groups_per_prompt"] + mod_dims["n_sample"]
    step = lambda b, g, t: (b * ng + g) * nt + t

    def rows(back, col0):
        return pl.BlockSpec((tt, lanes),
                            lambda b, g, t: (b * nt + jnp.maximum(t - back, 0), col0 + g))

    vspec = pl.BlockSpec((1, lanes), lambda b, g, t: (0, g))
    wspec = pl.BlockSpec((nb, LRU_BLOCK, LRU_BLOCK), lambda b, g, t: (g, 0, 0))
    sspec = pl.BlockSpec((1, SUBLANES, lanes), lambda b, g, t: (b, 0, g))
    return pl.pallas_call(
        functools.partial(_mix_prompt_kernel, scale=HEAD_DIM ** -0.5, mod_dims=mod_dims),
        out_shape=(
            jax.ShapeDtypeStruct((n_batch * seq, width), BF16),
            jax.ShapeDtypeStruct((n_batch * seq, width), BF16),
            jax.ShapeDtypeStruct((n_batch, SUBLANES, width), F32),
            jax.ShapeDtypeStruct((n_batch, SUBLANES, width), F32),
            jax.ShapeDtypeStruct((n_groups, mod_cols - mod_col0), F32),
        ),
        grid=(n_batch, ng, nt),
        in_specs=[
            rows(0, 0),
            rows(2, ng), rows(1, ng), rows(0, ng),
            rows(2, 2 * ng), rows(1, 2 * ng), rows(0, 2 * ng),
            pl.BlockSpec((nb, 1, 4 * tt), lambda b, g, t: (g, 0, 0)),
            rows(0, cx), rows(0, cg),
            pl.BlockSpec((conv_w.shape[0], lanes), lambda b, g, t: (0, g)),
            vspec, wspec, vspec, wspec, vspec, vspec,
            pl.BlockSpec(c_all.shape, lambda b, g, t: (0, 0)),
            pl.BlockSpec((d_model, mtn), lambda b, g, t: (0, mod_col0 // mtn + step(b, g, t))),
            pl.BlockSpec((1, mtn), lambda b, g, t: (0, mod_col0 // mtn + step(b, g, t))),
        ],
        out_specs=(rows(0, 0), rows(0, 0), sspec, sspec,
                   pl.BlockSpec((n_groups, mtn), lambda b, g, t: (0, step(b, g, t)))),
        scratch_shapes=[pltpu.VMEM((nb, 3, tt, tt), F32),
                        pltpu.VMEM((SUBLANES, lanes), F32), pltpu.VMEM((SUBLANES, lanes), F32),
                        pltpu.VMEM((nb, 2, tt, LRU_BLOCK), F32),
                        pltpu.VMEM((nb, tt // SUBLANES, LRU_BLOCK), F32)],
        compiler_params=_params(("arbitrary", "arbitrary", "arbitrary"), 40),
        name="mix_prompt",
    )(z, z, z, z, z, z, z, ext, z, z, conv_w, vec(conv_b), w_a, vec(b_a), w_x, vec(b_x), vec(lam),
      c_all, w_mod, b_mod)


def _lru_sample_kernel(x_ref, g_ref, prev_ref, h0_ref, cw_ref, cb_ref, wa_ref, ba_ref, wx_ref,
                       bx_ref, lam_ref, o_ref, conv_ref, h_ref, *, n_blocks):
    t = x_ref.shape[0]
    for n in range(n_blocks):
        sl = slice(n * LRU_BLOCK, (n + 1) * LRU_BLOCK)
        x = x_ref[:, sl]
        out, h = _lru_tile(x, prev_ref[0, :, sl], h0_ref[0, :, sl], g_ref[:, sl],
                           cw_ref[:, sl], cb_ref[:, sl], wa_ref[n], ba_ref[:, sl], wx_ref[n],
                           bx_ref[:, sl], lam_ref[:, sl])
        o_ref[:, sl] = out.astype(BF16)
        conv_ref[0, :, sl] = x[t - SUBLANES:]
        h_ref[0, :, sl] = h[t - SUBLANES:]


def _lru_sample(z, state_conv, state_h, lw, row0, n_batch, t, col_x, col_g):
    conv_w, conv_b, w_a, b_a, w_x, b_x, lam = lw
    n_blocks = w_a.shape[0]
    width = n_blocks * LRU_BLOCK
    k = conv_w.shape[0]
    assert t >= SUBLANES and k - 1 <= SUBLANES
    prev = jnp.pad(state_conv, ((0, 0), (SUBLANES - (k - 1), 0), (0, 0)))
    h0 = state_h.reshape(n_batch, 1, width)
    rb0 = row0 // t
    cx, cg = col_x // width, col_g // width
    vec = lambda a: a.reshape(1, width)
    vspec = pl.BlockSpec((1, width), lambda b: (0, 0))
    wspec = pl.BlockSpec((n_blocks, LRU_BLOCK, LRU_BLOCK), lambda b: (0, 0, 0))
    sspec = pl.BlockSpec((1, SUBLANES, width), lambda b: (b, 0, 0))
    return pl.pallas_call(
        functools.partial(_lru_sample_kernel, n_blocks=n_blocks),
        out_shape=(
            jax.ShapeDtypeStruct((n_batch * t, width), BF16),
            jax.ShapeDtypeStruct((n_batch, SUBLANES, width), F32),
            jax.ShapeDtypeStruct((n_batch, SUBLANES, width), F32),
        ),
        grid=(n_batch,),
        in_specs=[
            pl.BlockSpec((t, width), lambda b: (rb0 + b, cx)),
            pl.BlockSpec((t, width), lambda b: (rb0 + b, cg)),
            sspec,
            pl.BlockSpec((1, 1, width), lambda b: (b, 0, 0)),
            pl.BlockSpec((k, width), lambda b: (0, 0)),
            vspec, wspec, vspec, wspec, vspec, vspec,
        ],
        out_specs=(pl.BlockSpec((t, width), lambda b: (b, 0)), sspec, sspec),
        compiler_params=_params(("arbitrary",), 32),
        name="lru_sample",
    )(z, z, prev, h0, conv_w, vec(conv_b), w_a, vec(b_a), w_x, vec(b_x), vec(lam))


def _out_proj_kernel(ap_ref, lp_ref, as_ref, ls_ref, w_ref, xp_ref, xs_ref, gp_ref, gs_ref,
                     op_ref, os_ref):
    i = pl.program_id(1)
    ka = ap_ref.shape[1]

    def tile(a_ref, l_ref, x_ref, g_ref, o_ref):
        acc = (jnp.dot(a_ref[...], w_ref[:ka, :], preferred_element_type=F32)
               + jnp.dot(l_ref[...], w_ref[ka:, :], preferred_element_type=F32))
        o_ref[...] = x_ref[...] + _group_scale(acc, g_ref[...])

    _per_row_group(i,
                   lambda: tile(ap_ref, lp_ref, xp_ref, gp_ref, op_ref),
                   lambda: tile(as_ref, ls_ref, xs_ref, gs_ref, os_ref))


def _out_proj(attn_p, lru_p, attn_s, lru_s, w_out, x_p, x_s, table, gate_col, group):
    d, n = w_out.shape
    ka, kl = attn_p.shape[1], lru_p.shape[1]
    assert ka + kl == d and w_out.dtype == BF16
    tn = min(1024, n)
    rg = _RowGroups(x_p.shape[0], x_s.shape[0], tile=512)
    gcol = lambda j: gate_col * (n // tn) + j
    sample_table_block = (rg.rows_p // group) // (rg.rows_s // group)
    return pl.pallas_call(
        _out_proj_kernel,
        out_shape=(jax.ShapeDtypeStruct((rg.rows_p, n), F32), jax.ShapeDtypeStruct((rg.rows_s, n), F32)),
        grid=(n // tn, rg.steps),
        in_specs=[
            rg.prompt_spec(ka, lambda j: 0),
            rg.prompt_spec(kl, lambda j: 0),
            rg.sample_spec(ka, lambda j: 0, resident=True),
            rg.sample_spec(kl, lambda j: 0, resident=True),
            pl.BlockSpec((d, tn), lambda j, i: (0, j)),
            rg.prompt_spec(tn, lambda j: j),
            rg.sample_spec(tn, lambda j: j),
            rg.prompt_spec(tn, gcol, rows_per_row=group),
            rg.sample_spec(tn, gcol, rows_per_row=group, row_block=sample_table_block),
        ],
        out_specs=(rg.prompt_spec(tn, lambda j: j, output=True), rg.sample_spec(tn, lambda j: j)),
        compiler_params=_params(("arbitrary", "arbitrary"), 56),
        name="out_proj",
    )(attn_p, lru_p, attn_s, lru_s, w_out, x_p, x_s, table, table)


def _ffn_up_kernel(xp_ref, xs_ref, wg_hbm, wu_hbm, wd_ref, op_ref, os_ref, wdb_ref,
                   stage_ref, wb_ref, sem, *, n_cols, n_wd_chunks):
    j = pl.program_id(0)
    i = pl.program_id(1)
    tn = wb_ref.shape[2]
    n_full, rem = divmod(n_cols, tn)

    rc = wd_ref.shape[0]
    chunk = j * pl.num_programs(1) + i

    @pl.when(chunk < n_wd_chunks)
    def _():
        row = lax.broadcasted_iota(jnp.int32, wd_ref.shape, 0)
        wdb_ref[...] = jnp.where(chunk * rc + row < n_cols, wd_ref[...], 0.0).astype(BF16)

    def copies(jj, width):
        col = pl.multiple_of(jj * tn, tn)
        return [pltpu.make_async_copy(src.at[:, pl.ds(col, width)],
                                      stage_ref.at[w, :, pl.ds(0, width)], sem.at[w])
                for w, src in enumerate((wg_hbm, wu_hbm))]

    def start(jj):
        @pl.when(jj < n_full)
        def _():
            for c in copies(jj, tn):
                c.start()

        if rem:
            @pl.when(jj == n_full)
            def _():
                for c in copies(jj, rem):
                    c.start()

    @pl.when((j == 0) & (i == 0))
    def _():
        start(j)

    @pl.when(i == 0)
    def _():
        @pl.when(j < n_full)
        def _():
            for c in copies(j, tn):
                c.wait()
            wb_ref[...] = stage_ref[...].astype(BF16)

        if rem:
            @pl.when(j == n_full)
            def _():
                for c in copies(j, rem):
                    c.wait()
                wb_ref[:, :, :rem] = stage_ref[:, :, :rem].astype(BF16)
                wb_ref[:, :, rem:] = jnp.zeros((2, wb_ref.shape[1], tn - rem), BF16)

        @pl.when(j >= n_full + (1 if rem else 0))
        def _():
            wb_ref[...] = jnp.zeros_like(wb_ref)

    @pl.when((i == 1) & (j + 1 < pl.num_programs(0)))
    def _():
        start(j + 1)

    def tile(x_ref, o_ref):
        def hidden(width):
            x = x_ref[...]
            g = jnp.dot(x, wb_ref[0, :, :width], preferred_element_type=F32)
            u = jnp.dot(x, wb_ref[1, :, :width], preferred_element_type=F32)
            o_ref[:, :width] = (g * _sigmoid(g) * u).astype(BF16)

        if rem:
            pl.when(j != n_full)(lambda: hidden(tn))

            @pl.when(j == n_full)
            def _():
                hidden(rem)
                o_ref[:, rem:] = jnp.zeros((o_ref.shape[0], tn - rem), BF16)
        else:
            hidden(tn)

    _per_row_group(i, lambda: tile(xp_ref, op_ref), lambda: tile(xs_ref, os_ref))


def _ffn_up(u_p, u_s, w_gate, w_up, w_down, ff_pad):
    d, ff = w_gate.shape
    n_out = w_down.shape[1]
    tn = 512
    assert ff % LANES == 0 and ff_pad % tn == 0
    rg = _RowGroups(u_p.shape[0], u_s.shape[0])
    assert rg.steps >= 2
    steps = (ff_pad // tn) * rg.steps
    rc = next(r for r in range(16, ff_pad + 1, 16)
              if ff_pad % r == 0 and ff % r == 0 and ff_pad // r <= steps)
    n_wd_chunks, n_real_chunks = ff_pad // rc, ff // rc
    chunk_of = lambda j, i: j * rg.steps + i
    hbm = pl.BlockSpec(memory_space=pl.ANY)
    return pl.pallas_call(
        functools.partial(_ffn_up_kernel, n_cols=ff, n_wd_chunks=n_wd_chunks),
        out_shape=(jax.ShapeDtypeStruct((rg.rows_p, ff_pad), BF16),
                   jax.ShapeDtypeStruct((rg.rows_s, ff_pad), BF16),
                   jax.ShapeDtypeStruct((ff_pad, n_out), BF16)),
        grid=(ff_pad // tn, rg.steps),
        in_specs=[rg.prompt_spec(d, lambda j: 0), rg.sample_spec(d, lambda j: 0, resident=True),
                  hbm, hbm,
                  pl.BlockSpec((rc, n_out),
                               lambda j, i: (jnp.minimum(chunk_of(j, i), n_real_chunks - 1), 0))],
        out_specs=(rg.prompt_spec(tn, lambda j: j, output=True), rg.sample_spec(tn, lambda j: j),
                   pl.BlockSpec((rc, n_out),
                                lambda j, i: (jnp.minimum(chunk_of(j, i), n_wd_chunks - 1), 0))),
        scratch_shapes=[pltpu.VMEM((2, d, tn), F32), pltpu.VMEM((2, d, tn), BF16),
                        pltpu.SemaphoreType.DMA((2,))],
        compiler_params=_params(("arbitrary", "arbitrary"), 62),
        name="ffn_up",
    )(u_p, u_s, w_gate, w_up, w_down)


def _ffn_down_kernel(h_ref, w_ref, x_ref, g_ref, o_ref):
    k = pl.program_id(2)
    last = pl.num_programs(2) - 1
    tm = o_ref.shape[0]
    rows = min(tm, 512)
    group = tm // g_ref.shape[0]

    def chunks():
        for r0 in range(0, tm, rows):
            sl = slice(r0, r0 + rows)
            yield sl, slice(r0 // group, (r0 + rows) // group), jnp.dot(
                h_ref[sl, :], w_ref[...], preferred_element_type=F32)

    @pl.when(k == 0)
    def _():
        for sl, _, part in chunks():
            o_ref[sl, :] = part

    @pl.when((k > 0) & (k < last))
    def _():
        for sl, _, part in chunks():
            o_ref[sl, :] += part

    @pl.when(k == last)
    def _():
        for sl, gsl, part in chunks():
            o_ref[sl, :] = x_ref[sl, :] + _group_scale(o_ref[sl, :] + part, g_ref[gsl, :])


def _ffn_down(h, w_down, x1, table, gate_col, group, group0, tm, tn):
    rows, ff_pad = h.shape
    n = w_down.shape[1]
    tk = 1024
    tn = min(tn, n)
    gt = tm // group
    assert w_down.shape[0] == ff_pad and w_down.dtype == BF16 and ff_pad // tk >= 2
    assert rows % tm == 0 and ff_pad % tk == 0 and n % tn == 0 and group0 % gt == 0
    g0 = group0 // gt
    gcol0 = gate_col * (n // tn)
    return pl.pallas_call(
        _ffn_down_kernel,
        out_shape=jax.ShapeDtypeStruct((rows, n), F32),
        grid=(rows // tm, n // tn, ff_pad // tk),
        in_specs=[
            pl.BlockSpec((tm, tk), lambda i, j, k: (i, k)),
            pl.BlockSpec((tk, tn), lambda i, j, k: (k, j)),
            pl.BlockSpec((tm, tn), lambda i, j, k: (i, j)),
            pl.BlockSpec((gt, tn), lambda i, j, k: (g0 + i, gcol0 + j)),
        ],
        out_specs=pl.BlockSpec((tm, tn), lambda i, j, k: (i, j)),
        compiler_params=_params(("arbitrary", "arbitrary", "arbitrary"), 58),
        name="ffn_down",
    )(h, w_down, x1, table)


def _layer(x_p, x_s, c_prompt, c_sample, cache_k, cache_v, state_conv, state_h, lw, dims):
    (norm_mix_g, norm_ffn_g, w_mod, b_mod, w_in, q_norm_g, k_norm_g, rel_bias,
     conv_w, conv_b, w_rg_a, b_rg_a, w_rg_x, b_rg_x, lru_lambda, w_out,
     w_ffn_gate, w_ffn_up, w_ffn_down) = lw
    n_prompt, seq, n_sample, t_s = dims
    group = t_s
    n_heads = rel_bias.shape[0]
    attn_w = n_heads * HEAD_DIM
    lru_w = w_rg_a.shape[0] * LRU_BLOCK
    rows_p = n_prompt * seq
    rows_s = n_sample * t_s
    groups_p = rows_p // group

    d = x_p.shape[1]
    c_all = _mod_inputs(c_prompt, c_sample)
    b_mod2 = b_mod.reshape(1, -1)
    mod_dims = dict(n_sample=n_sample, n_prompt=n_prompt, groups_per_prompt=seq // group)
    table1 = _mod_table(c_all, w_mod, b_mod2, 2 * d, **mod_dims)
    u_p = _norm_mod(x_p, norm_mix_g, table1, 0, 1, group, 0)
    u_s = _norm_mod(x_s, norm_mix_g, table1, 0, 1, group, groups_p)
    qk_gain = jnp.concatenate([jnp.tile(q_norm_g, n_heads), jnp.tile(k_norm_g, n_heads)]).reshape(1, -1)
    z_p, z_s, w_out_b = _in_proj(u_p, u_s, w_in, qk_gain, attn_w, w_out)

    lru_params = (conv_w, conv_b, w_rg_a, b_rg_a, w_rg_x, b_rg_x, lru_lambda)
    col_x, col_g = 3 * attn_w, 3 * attn_w + lru_w
    attn_p, lru_p, conv_p, h_p, table2 = _mix_prompt(
        z_p, rel_bias, lru_params, c_all, w_mod, b_mod2, 2 * d, mod_dims,
        n_prompt, seq, n_heads, col_x, col_g)
    attn_s = _attn_sample(z_s, cache_k, cache_v, rel_bias, 0, n_sample, t_s, n_heads)
    lru_s, conv_s, h_s = _lru_sample(z_s, state_conv, state_h, lru_params, 0, n_sample, t_s,
                                     col_x, col_g)

    x1_p, x1_s = _out_proj(attn_p, lru_p, attn_s, lru_s, w_out_b, x_p, x_s, table2, 0, group)
    u2_p = _norm_mod(x1_p, norm_ffn_g, table2, 1, 2, group, 0)
    u2_s = _norm_mod(x1_s, norm_ffn_g, table2, 1, 2, group, groups_p)
    ff = w_ffn_gate.shape[1]
    ff_pad = -(-ff // 1024) * 1024
    hid_p, hid_s, w_down_b = _ffn_up(u2_p, u2_s, w_ffn_gate, w_ffn_up, w_ffn_down, ff_pad)
    y_p = _ffn_down(hid_p, w_down_b, x1_p, table2, 3, group, 0, 2048, 1024)
    y_s = _ffn_down(hid_s, w_down_b, x1_s, table2, 3, group, groups_p, rows_s, 4096)

    keep = min(N_PREV_CHUNKS * CHUNK, seq)
    kv_p = z_p.reshape(n_prompt, seq, -1)[:, seq - keep:, attn_w:3 * attn_w]
    kv_p = kv_p.reshape(n_prompt, keep, 2, n_heads, HEAD_DIM)
    kv_s = z_s[:, attn_w:3 * attn_w].reshape(n_sample, t_s, 2, n_heads, HEAD_DIM)
    k_conv = conv_w.shape[0] - 1
    state = dict(
        k_p=kv_p[:, :, 0], v_p=kv_p[:, :, 1], conv_p=conv_p[:, SUBLANES - k_conv:], h_p=h_p[:, SUBLANES - 1],
        k_s=kv_s[:, :, 0], v_s=kv_s[:, :, 1], conv_s=conv_s[:, SUBLANES - k_conv:], h_s=h_s[:, SUBLANES - 1])
    return y_p, y_s, state


def kernel(x_prompt, x_sample, cache_k, cache_v, state_conv, state_h, c_prompt, c_sample, norm_mix_g, norm_ffn_g, w_mod, b_mod, w_in, q_norm_g, k_norm_g, rel_bias, conv_w, conv_b, w_rg_a, b_rg_a, w_rg_x, b_rg_x, lru_lambda, w_out, w_ffn_gate, w_ffn_up, w_ffn_down):
    n_prompt, seq, d = x_prompt.shape
    n_sample, t_s, _ = x_sample.shape
    depth = w_in.shape[0]
    dims = (n_prompt, seq, n_sample, t_s)
    yp = x_prompt.reshape(n_prompt * seq, d)
    ys = x_sample.reshape(n_sample * t_s, d)
    states = []
    for l in range(depth):
        lw = (norm_mix_g[l], norm_ffn_g[l], w_mod[l], b_mod[l], w_in[l], q_norm_g[l], k_norm_g[l],
              rel_bias[l], conv_w[l], conv_b[l], w_rg_a[l], b_rg_a[l], w_rg_x[l], b_rg_x[l],
              lru_lambda[l], w_out[l], w_ffn_gate[l], w_ffn_up[l], w_ffn_down[l])
        yp, ys, st = _layer(yp, ys, c_prompt, c_sample, cache_k[l], cache_v[l], state_conv[l],
                            state_h[l], lw, dims)
        states.append(st)
    stack = lambda name: jnp.stack([s[name] for s in states])
    return (yp.reshape(n_prompt, seq, d), ys.reshape(n_sample, t_s, d),
            stack("k_p"), stack("v_p"), stack("conv_p"), stack("h_p"),
            stack("k_s"), stack("v_s"), stack("conv_s"), stack("h_s"))
```

```python
import functools
import math

import jax
import jax.numpy as jnp
from jax import lax
from jax.experimental import pallas as pl
from jax.experimental.pallas import tpu as pltpu

F32 = jnp.float32
BF16 = jnp.bfloat16

CHUNK = 64
N_PREV_CHUNKS = 8
HEAD_DIM = 128
LRU_BLOCK = 128
LRU_C = 8.0
NEG_INF = -1e30
EPS = 1e-6
LOG2E = math.log2(math.e)

LANES = 128
SUBLANES = 8
MIB = 1024 * 1024


def _params(semantics, vmem_mib):
    return pltpu.CompilerParams(dimension_semantics=semantics, vmem_limit_bytes=vmem_mib * MIB)


def _sigmoid(x):
    return 0.5 * jnp.tanh(0.5 * x) + 0.5


def _gelu_tanh(x):
    c = math.sqrt(2.0 / math.pi)
    return x * (0.5 * (1.0 + jnp.tanh(c * (x + 0.044715 * (x * x * x)))))


def _group_scale(y, g):
    rows, n = y.shape
    groups = g.shape[0]
    return (y.reshape(groups, rows // groups, n) * g[:, None, :]).reshape(rows, n)


PROMPT_TILE_ROWS = 1024


class _RowGroups:
    def __init__(self, rows_p, rows_s, tile=PROMPT_TILE_ROWS):
        assert rows_p % tile == 0
        self.rows_p, self.rows_s, self.tile = rows_p, rows_s, tile
        self.np_tiles = rows_p // tile
        self.steps = self.np_tiles + 1

    def prompt_spec(self, cols, col_block, rows_per_row=1, output=False):
        first = 0 if output else self.np_tiles - 1
        return pl.BlockSpec((self.tile // rows_per_row, cols),
                            lambda j, i: (jnp.where(i == 0, first, i - 1), col_block(j)))

    def sample_spec(self, cols, col_block, rows_per_row=1, row_block=0, resident=False):
        mode = dict(pipeline_mode=pl.Buffered(1)) if resident else {}
        return pl.BlockSpec((self.rows_s // rows_per_row, cols),
                            lambda j, i: (row_block, col_block(j)), **mode)


def _per_row_group(i, prompt_fn, sample_fn):
    pl.when(i == 0)(sample_fn)
    pl.when(i > 0)(prompt_fn)


def _mod_kernel(c_ref, w_ref, b_ref, o_ref, *, n_sample, n_prompt, groups_per_prompt):
    c = c_ref[...]
    s = (c * _sigmoid(c)).astype(BF16)
    m = jnp.dot(s, w_ref[...].astype(BF16), preferred_element_type=F32) + b_ref[...]
    tn = m.shape[1]
    for b in range(n_prompt):
        row = m[n_sample + b:n_sample + b + 1, :]
        o_ref[b * groups_per_prompt:(b + 1) * groups_per_prompt, :] = jnp.broadcast_to(
            row, (groups_per_prompt, tn))
    o_ref[n_prompt * groups_per_prompt:n_prompt * groups_per_prompt + n_sample, :] = m[:n_sample, :]


def _mod_inputs(c_prompt, c_sample):
    rows = c_sample.shape[0] + c_prompt.shape[0]
    rows_pad = -(-rows // SUBLANES) * SUBLANES
    return jnp.concatenate(
        [c_sample, c_prompt, jnp.zeros((rows_pad - rows, c_prompt.shape[1]), c_prompt.dtype)], axis=0)


def _mod_table(c_all, w_mod, b_mod, n_cols, n_sample, n_prompt, groups_per_prompt):
    rows_pad, d = c_all.shape
    n_groups = n_prompt * groups_per_prompt + n_sample
    tn = 512
    return pl.pallas_call(
        functools.partial(_mod_kernel, n_sample=n_sample, n_prompt=n_prompt,
                          groups_per_prompt=groups_per_prompt),
        out_shape=jax.ShapeDtypeStruct((n_groups, n_cols), F32),
        grid=(n_cols // tn,),
        in_specs=[
            pl.BlockSpec((rows_pad, d), lambda j: (0, 0)),
            pl.BlockSpec((d, tn), lambda j: (0, j)),
            pl.BlockSpec((1, tn), lambda j: (0, j)),
        ],
        out_specs=pl.BlockSpec((n_groups, tn), lambda j: (0, j)),
        compiler_params=_params(("arbitrary",), 40),
        name="mod_table",
    )(c_all, w_mod, b_mod)


def _norm_mod_kernel(x_ref, gain_ref, shift_ref, scale_ref, o_ref):
    x = x_ref[...]
    ms = jnp.mean(x * x, axis=-1, keepdims=True)
    y = x * lax.rsqrt(ms + EPS) * gain_ref[...]
    rows, d = y.shape
    groups = scale_ref.shape[0]
    y3 = y.reshape(groups, rows // groups, d)
    u = y3 * (1.0 + scale_ref[...][:, None, :]) + shift_ref[...][:, None, :]
    o_ref[...] = u.reshape(rows, d).astype(BF16)


def _norm_mod(x, gain, table, shift_col, scale_col, group, group0):
    rows, d = x.shape
    tm = 512
    gt = tm // group
    assert rows % tm == 0 and group0 % gt == 0
    g0 = group0 // gt
    return pl.pallas_call(
        _norm_mod_kernel,
        out_shape=jax.ShapeDtypeStruct((rows, d), BF16),
        grid=(rows // tm,),
        in_specs=[
            pl.BlockSpec((tm, d), lambda i: (i, 0)),
            pl.BlockSpec((1, d), lambda i: (0, 0)),
            pl.BlockSpec((gt, d), lambda i: (g0 + i, shift_col)),
            pl.BlockSpec((gt, d), lambda i: (g0 + i, scale_col)),
        ],
        out_specs=pl.BlockSpec((tm, d), lambda i: (i, 0)),
        compiler_params=_params(("arbitrary",), 40),
        name="norm_mod",
    )(x, gain.reshape(1, d), table, table)


def _row_chunk(rows, granule, max_chunks):
    return next(r for r in range(granule, rows + 1, granule)
                if rows % r == 0 and rows // r <= max_chunks)


def _in_proj_kernel(xp_ref, xs_ref, w_hbm, gain_ref, wo_ref, op_ref, os_ref, wob_ref,
                    stage_ref, wb_ref, sem, *, n_norm_tiles, n_wo_chunks):
    j = pl.program_id(0)
    i = pl.program_id(1)
    tn = wb_ref.shape[1]

    def copy(jj):
        return pltpu.make_async_copy(w_hbm.at[:, pl.ds(pl.multiple_of(jj * tn, tn), tn)],
                                     stage_ref, sem.at[0])

    @pl.when((j == 0) & (i == 0))
    def _():
        copy(j).start()

    @pl.when(i == 0)
    def _():
        copy(j).wait()
        wb_ref[...] = stage_ref[...].astype(BF16)

    @pl.when((i == 1) & (j + 1 < pl.num_programs(0)))
    def _():
        copy(j + 1).start()

    @pl.when(j * pl.num_programs(1) + i < n_wo_chunks)
    def _():
        wob_ref[...] = wo_ref[...].astype(BF16)

    def tile(x_ref, o_ref):
        rows, tn = o_ref.shape

        @pl.when(j < n_norm_tiles)
        def _():
            half = rows // 2
            accs = [jnp.dot(x_ref[r0:r0 + half, :], wb_ref[...], preferred_element_type=F32)
                    for r0 in (0, half)]
            for r0, acc in zip((0, half), accs):
                for g in range(tn // HEAD_DIM):
                    sl = slice(g * HEAD_DIM, (g + 1) * HEAD_DIM)
                    zg = acc[:, sl]
                    ms = jnp.mean(zg * zg, axis=-1, keepdims=True)
                    o_ref[r0:r0 + half, sl] = zg * lax.rsqrt(ms + EPS) * gain_ref[:, sl]

        @pl.when(j >= n_norm_tiles)
        def _():
            o_ref[...] = jnp.dot(x_ref[...], wb_ref[...], preferred_element_type=F32)

    _per_row_group(i, lambda: tile(xp_ref, op_ref), lambda: tile(xs_ref, os_ref))


def _in_proj(u_p, u_s, w_in, qk_gain, attn_width, w_out):
    d, n = w_in.shape
    tn = 1024
    assert n % tn == 0 and (2 * attn_width) % tn == 0
    n_norm_tiles = 2 * attn_width // tn
    rg = _RowGroups(u_p.shape[0], u_s.shape[0], tile=512)
    assert rg.steps >= 2
    ko, no = w_out.shape
    rc = _row_chunk(ko, 16, (n // tn) * rg.steps)
    n_wo_chunks = ko // rc
    wo_spec = pl.BlockSpec((rc, no), lambda j, i: (jnp.minimum(j * rg.steps + i, n_wo_chunks - 1), 0))
    return pl.pallas_call(
        functools.partial(_in_proj_kernel, n_norm_tiles=n_norm_tiles, n_wo_chunks=n_wo_chunks),
        out_shape=(jax.ShapeDtypeStruct((rg.rows_p, n), F32), jax.ShapeDtypeStruct((rg.rows_s, n), F32),
                   jax.ShapeDtypeStruct((ko, no), BF16)),
        grid=(n // tn, rg.steps),
        in_specs=[
            rg.prompt_spec(d, lambda j: 0),
            rg.sample_spec(d, lambda j: 0, resident=True),
            pl.BlockSpec(memory_space=pl.ANY),
            pl.BlockSpec((1, tn), lambda j, i: (0, jnp.minimum(j, n_norm_tiles - 1))),
            wo_spec,
        ],
        out_specs=(rg.prompt_spec(tn, lambda j: j, output=True), rg.sample_spec(tn, lambda j: j),
                   wo_spec),
        scratch_shapes=[pltpu.VMEM((d, tn), F32), pltpu.VMEM((d, tn), BF16),
                        pltpu.SemaphoreType.DMA((1,))],
        compiler_params=_params(("arbitrary", "arbitrary"), 58),
        name="in_proj",
    )(u_p, u_s, w_in, qk_gain, w_out)


def _bias_ramp(rel_bias, top, length):
    n_heads, n_rel = rel_bias.shape
    clip = (n_rel - 1) // 2
    lead = top - clip
    tail = length - lead - n_rel
    assert lead >= 0 and tail >= 0
    rev = rel_bias[:, ::-1]
    ext = jnp.concatenate([
        jnp.broadcast_to(rel_bias[:, n_rel - 1:], (n_heads, lead)),
        rev,
        jnp.broadcast_to(rel_bias[:, :1], (n_heads, tail)),
    ], axis=1)
    return ext.reshape(n_heads, 1, length)


def _lane_reduce(tiles, combine, reduce):
    parts = [t[:, c:c + LANES] for t in tiles for c in range(0, t.shape[1], LANES)]
    return reduce(functools.reduce(combine, parts), axis=-1, keepdims=True)


def _toeplitz(base, rows):
    w = base.shape[1]
    return pltpu.roll(jnp.broadcast_to(base, (rows, w)), w - (rows - 1), 1, stride=1, stride_axis=0)


def _attn_prompt_bias(ext_ref, bias_ref, tq):
    shift = CHUNK.bit_length() - 1
    rowc = jnp.right_shift(lax.broadcasted_iota(jnp.int32, (tq, tq), 0), shift)
    colc = jnp.right_shift(lax.broadcasted_iota(jnp.int32, (tq, tq), 1), shift)
    for h in range(bias_ref.shape[0]):
        for m in range(3):
            off = (2 - m) * tq
            t = _toeplitz(ext_ref[h, :, off:off + 2 * tq], tq)[:, :tq]
            if m == 0:
                t = jnp.where(colc <= rowc, t, NEG_INF)
            if m == 2:
                t = jnp.where(rowc <= colc, t, NEG_INF)
            bias_ref[h, m] = t * LOG2E


def _attn_prompt_head(h, i, q_ref, k_refs, v_refs, bias_ref, o_ref, scale):
    dn = (((1,), (1,)), ((), ()))
    sl = slice(h * HEAD_DIM, (h + 1) * HEAD_DIM)
    q = q_ref[:, sl].astype(BF16)
    scores = []
    for m, k_ref in zip((2, 1, 0), k_refs):
        s = lax.dot_general(q, k_ref[:, sl].astype(BF16), dn, preferred_element_type=F32)
        s = s * (scale * LOG2E) + bias_ref[h, m]
        if m > 0:
            s = s + jnp.where(i >= m, 0.0, NEG_INF)
        scores.append(s)
    mx = _lane_reduce(scores, jnp.maximum, jnp.max)
    probs = [jnp.exp2(s - mx) for s in scores]
    denom = _lane_reduce(probs, jnp.add, jnp.sum)
    out = functools.reduce(jnp.add, [
        jnp.dot(p.astype(BF16), v_ref[:, sl].astype(BF16), preferred_element_type=F32)
        for p, v_ref in zip(probs, v_refs)])
    o_ref[:, sl] = (out / denom).astype(BF16)


def _attn_sample_kernel(q_ref, kn_ref, vn_ref, ck_ref, cv_ref, ext_ref, o_ref, *, scale, n_heads):
    t = q_ref.shape[0]
    r = ck_ref.shape[1] // n_heads
    dn = (((1,), (1,)), ((), ()))
    for h in range(n_heads):
        sl = slice(h * HEAD_DIM, (h + 1) * HEAD_DIM)
        rows_h = pl.ds(h, r, stride=n_heads)
        bias = _toeplitz(ext_ref[h], t)
        q = q_ref[:, sl].astype(BF16)
        s1 = lax.dot_general(q, ck_ref[0, rows_h, :].astype(BF16), dn, preferred_element_type=F32)
        s1 = s1 * scale + bias[:, :r]
        s2 = lax.dot_general(q, kn_ref[:, sl].astype(BF16), dn, preferred_element_type=F32)
        s2 = s2 * scale + bias[:, r:r + t]
        mx = jnp.maximum(jnp.max(s1, axis=-1, keepdims=True), jnp.max(s2, axis=-1, keepdims=True))
        p1 = jnp.exp(s1 - mx)
        p2 = jnp.exp(s2 - mx)
        denom = jnp.sum(p1, axis=-1, keepdims=True) + jnp.sum(p2, axis=-1, keepdims=True)
        out = (jnp.dot(p1.astype(BF16), cv_ref[0, rows_h, :].astype(BF16), preferred_element_type=F32)
               + jnp.dot(p2.astype(BF16), vn_ref[:, sl].astype(BF16), preferred_element_type=F32))
        o_ref[:, sl] = (out / denom).astype(BF16)


def _attn_sample(z, cache_k, cache_v, rel_bias, row0, n_batch, t, n_heads):
    r = cache_k.shape[1]
    width = n_heads * HEAD_DIM
    ck = cache_k.reshape(n_batch, r * n_heads, HEAD_DIM)
    cv = cache_v.reshape(n_batch, r * n_heads, HEAD_DIM)
    top = r + t - 1
    clip = (rel_bias.shape[1] - 1) // 2
    ramp = -(-max(r + 2 * t, top + clip + 1) // LANES) * LANES
    ext = _bias_ramp(rel_bias, top, ramp)
    rb0 = row0 // t
    return pl.pallas_call(
        functools.partial(_attn_sample_kernel, scale=HEAD_DIM ** -0.5, n_heads=n_heads),
        out_shape=jax.ShapeDtypeStruct((n_batch * t, width), BF16),
        grid=(n_batch,),
        in_specs=[
            pl.BlockSpec((t, width), lambda b: (rb0 + b, 0)),
            pl.BlockSpec((t, width), lambda b: (rb0 + b, 1)),
            pl.BlockSpec((t, width), lambda b: (rb0 + b, 2)),
            pl.BlockSpec((1, r * n_heads, HEAD_DIM), lambda b: (b, 0, 0)),
            pl.BlockSpec((1, r * n_heads, HEAD_DIM), lambda b: (b, 0, 0)),
            pl.BlockSpec((n_heads, 1, ramp), lambda b: (0, 0, 0)),
        ],
        out_specs=pl.BlockSpec((t, width), lambda b: (b, 0)),
        compiler_params=_params(("arbitrary",), 40),
        name="attn_sample",
    )(z, z, z, ck, cv, ext)


def _lru_tile(x, prev, h_in, gate_in, cw, cb, wa, ba, wx, bx, lam, scan_scratch=None):
    t = x.shape[0]
    p = prev.shape[0]
    k = cw.shape[0]
    xcat = jnp.concatenate([prev, x], axis=0)
    xc = cb
    for i in range(k):
        lo = p - (k - 1) + i
        xc = xc + xcat[lo:lo + t] * cw[i:i + 1]
    xcb = xc.astype(BF16)
    rg = _sigmoid(jnp.dot(xcb, wa.astype(BF16), preferred_element_type=F32) + ba)
    ig = _sigmoid(jnp.dot(xcb, wx.astype(BF16), preferred_element_type=F32) + bx)
    neg = -lam
    softplus = jnp.maximum(neg, 0.0) + jnp.log1p(jnp.exp(-jnp.abs(neg)))
    log_a = (-LRU_C) * rg * softplus
    a = jnp.exp(log_a)
    var = -jnp.tanh(log_a) * (a * a + 1.0)
    std = jnp.where(var == 0.0, 0.0, var * lax.rsqrt(var))
    b = std * (ig * xc)
    h = _linear_scan(a, b, h_in, scan_scratch)
    return h * _gelu_tanh(gate_in), h


def _doubling_scan(a, b, axis):
    n = a.shape[axis]
    idx = lax.broadcasted_iota(jnp.int32, a.shape, axis)
    step = 1
    while step < n:
        keep = idx >= step
        a_prev = jnp.where(keep, pltpu.roll(a, step, axis), 1.0)
        b_prev = jnp.where(keep, pltpu.roll(b, step, axis), 0.0)
        b = a * b_prev + b
        a = a * a_prev
        step *= 2
    return a, b


def _linear_scan(a, b, h_in, scratch=None):
    t, c = a.shape
    groups = t // SUBLANES
    a3, b3 = _doubling_scan(a.reshape(groups, SUBLANES, c), b.reshape(groups, SUBLANES, c), 1)
    if scratch is not None:
        ab_ref, hs_ref = scratch
        a2, b2 = a3.reshape(t, c), b3.reshape(t, c)
        ab_ref[0] = a2
        ab_ref[1] = b2
        ends = pl.ds(SUBLANES - 1, groups, stride=SUBLANES)
        a_cum, b_cum = _doubling_scan(ab_ref[0, ends, :], ab_ref[1, ends, :], 0)
        h_end = a_cum * h_in + b_cum
        first = lax.broadcasted_iota(jnp.int32, h_end.shape, 0) == 0
        hs_ref[...] = jnp.where(first, h_in, pltpu.roll(h_end, 1, 0))
        h_start = jnp.concatenate(
            [jnp.broadcast_to(hs_ref[g:g + 1, :], (SUBLANES, c)) for g in range(groups)], axis=0)
        return a2 * h_start + b2
    ae = jnp.broadcast_to(a3[:, SUBLANES - 1:, :], a3.shape).reshape(t, c)
    be = jnp.broadcast_to(b3[:, SUBLANES - 1:, :], b3.shape).reshape(t, c)
    step = SUBLANES
    while step < t:
        be = jnp.concatenate([be[:step], ae[step:] * be[:-step] + be[step:]], axis=0)
        ae = jnp.concatenate([ae[:step], ae[step:] * ae[:-step]], axis=0)
        step *= 2
    h_end = ae * h_in + be
    h_start = jnp.concatenate([jnp.broadcast_to(h_in, (SUBLANES, c)), h_end[:t - SUBLANES]], axis=0)
    return a3.reshape(t, c) * h_start + b3.reshape(t, c)


def _lru_prompt_block(n, x_ref, g_ref, cw_ref, cb_ref, wa_ref, ba_ref, wx_ref, bx_ref, lam_ref,
                      o_ref, conv_ref, h_ref, px_ref, ph_ref, ab_ref, hs_ref):
    t = x_ref.shape[0]
    sl = slice(n * LRU_BLOCK, (n + 1) * LRU_BLOCK)
    x = x_ref[:, sl]
    out, h = _lru_tile(x, px_ref[:, sl], ph_ref[SUBLANES - 1:SUBLANES, sl], g_ref[:, sl],
                       cw_ref[:, sl], cb_ref[:, sl], wa_ref[n], ba_ref[:, sl], wx_ref[n],
                       bx_ref[:, sl], lam_ref[:, sl], (ab_ref.at[n], hs_ref.at[n]))
    o_ref[:, sl] = out.astype(BF16)
    px_ref[:, sl] = x[t - SUBLANES:]
    ph_ref[:, sl] = h[t - SUBLANES:]
    conv_ref[0, :, sl] = x[t - SUBLANES:]
    h_ref[0, :, sl] = h[t - SUBLANES:]


def _attn_bias_kernel(ext_ref, o_ref):
    _attn_prompt_bias(ext_ref, o_ref, o_ref.shape[2])


def _attn_bias_tiles(ext, nb, tq):
    n_heads = ext.shape[0]
    return pl.pallas_call(
        _attn_bias_kernel,
        out_shape=jax.ShapeDtypeStruct((n_heads, 3, tq, tq), F32),
        grid=(n_heads // nb,),
        in_specs=[pl.BlockSpec((nb, 1, ext.shape[2]), lambda g: (g, 0, 0))],
        out_specs=pl.BlockSpec((nb, 3, tq, tq), lambda g: (g, 0, 0, 0)),
        compiler_params=_params(("arbitrary",), 32),
        name="attn_bias",
    )(ext)


def _mix_prompt_kernel(q_ref, ka_ref, kb_ref, kc_ref, va_ref, vb_ref, vc_ref, bias_ref,
                       x_ref, g_ref, cw_ref, cb_ref, wa_ref, ba_ref, wx_ref, bx_ref, lam_ref,
                       c_ref, wm_ref, bm_ref,
                       attn_ref, lru_ref, conv_ref, h_ref, table_ref,
                       px_ref, ph_ref, ab_ref, hs_ref, *, scale, mod_dims):
    i = pl.program_id(2)

    @pl.when(i == 0)
    def _():
        px_ref[...] = jnp.zeros_like(px_ref)
        ph_ref[...] = jnp.zeros_like(ph_ref)

    _mod_kernel(c_ref, wm_ref, bm_ref, table_ref, **mod_dims)

    for n in range(q_ref.shape[1] // HEAD_DIM):
        _attn_prompt_head(n, i, q_ref, (ka_ref, kb_ref, kc_ref), (va_ref, vb_ref, vc_ref),
                          bias_ref, attn_ref, scale)
        _lru_prompt_block(n, x_ref, g_ref, cw_ref, cb_ref, wa_ref, ba_ref, wx_ref, bx_ref, lam_ref,
                          lru_ref, conv_ref, h_ref, px_ref, ph_ref, ab_ref, hs_ref)


def _mix_prompt(z, rel_bias, lw, c_all, w_mod, b_mod, mod_col0, mod_dims,
                n_batch, seq, n_heads, col_x, col_g):
    conv_w, conv_b, w_a, b_a, w_x, b_x, lam = lw
    n_blocks = w_a.shape[0]
    tt = 256
    nb = 4
    lanes = nb * HEAD_DIM
    assert HEAD_DIM == LRU_BLOCK and n_heads == n_blocks and n_heads % nb == 0
    assert tt == (N_PREV_CHUNKS * CHUNK) // 2 and seq % tt == 0
    assert col_x % lanes == 0 and col_g % lanes == 0
    nt = seq // tt
    ng = n_heads // nb
    width = n_heads * HEAD_DIM
    bias = _attn_bias_tiles(_bias_ramp(rel_bias, 3 * tt - 1, 4 * tt), nb, tt)
    cx, cg = col_x // lanes, col_g // lanes
    vec = lambda a: a.reshape(1, width)

    steps = n_batch * ng * nt
    d_model, mod_cols = w_mod.shape
    mtn = (mod_cols - mod_col0) // steps
    assert mtn % LANES == 0 and mtn * steps == mod_cols - mod_col0 and mod_col0 % mtn == 0
    n_groups = mod_dims["n_prompt"] * mod_dims["groups_per_prompt"] + mod_dims["n_sample"]
    step = lambda b, g, t: (b * ng + g) * nt + t

    def rows(back, col0):
        return pl.BlockSpec((tt, lanes),
                            lambda b, g, t: (b * nt + jnp.maximum(t - back, 0), col0 + g))

    vspec = pl.BlockSpec((1, lanes), lambda b, g, t: (0, g))
    wspec = pl.BlockSpec((nb, LRU_BLOCK, LRU_BLOCK), lambda b, g, t: (g, 0, 0))
    sspec = pl.BlockSpec((1, SUBLANES, lanes), lambda b, g, t: (b, 0, g))
    return pl.pallas_call(
        functools.partial(_mix_prompt_kernel, scale=HEAD_DIM ** -0.5, mod_dims=mod_dims),
        out_shape=(
            jax.ShapeDtypeStruct((n_batch * seq, width), BF16),
            jax.ShapeDtypeStruct((n_batch * seq, width), BF16),
            jax.ShapeDtypeStruct((n_batch, SUBLANES, width), F32),
            jax.ShapeDtypeStruct((n_batch, SUBLANES, width), F32),
            jax.ShapeDtypeStruct((n_groups, mod_cols - mod_col0), F32),
        ),
        grid=(n_batch, ng, nt),
        in_specs=[
            rows(0, 0),
            rows(2, ng), rows(1, ng), rows(0, ng),
            rows(2, 2 * ng), rows(1, 2 * ng), rows(0, 2 * ng),
            pl.BlockSpec((nb, 3, tt, tt), lambda b, g, t: (g, 0, 0, 0)),
            rows(0, cx), rows(0, cg),
            pl.BlockSpec((conv_w.shape[0], lanes), lambda b, g, t: (0, g)),
            vspec, wspec, vspec, wspec, vspec, vspec,
            pl.BlockSpec(c_all.shape, lambda b, g, t: (0, 0)),
            pl.BlockSpec((d_model, mtn), lambda b, g, t: (0, mod_col0 // mtn + step(b, g, t))),
            pl.BlockSpec((1, mtn), lambda b, g, t: (0, mod_col0 // mtn + step(b, g, t))),
        ],
        out_specs=(rows(0, 0), rows(0, 0), sspec, sspec,
                   pl.BlockSpec((n_groups, mtn), lambda b, g, t: (0, step(b, g, t)))),
        scratch_shapes=[pltpu.VMEM((SUBLANES, lanes), F32), pltpu.VMEM((SUBLANES, lanes), F32),
                        pltpu.VMEM((nb, 2, tt, LRU_BLOCK), F32),
                        pltpu.VMEM((nb, tt // SUBLANES, LRU_BLOCK), F32)],
        compiler_params=_params(("arbitrary", "arbitrary", "arbitrary"), 40),
        name="mix_prompt",
    )(z, z, z, z, z, z, z, bias, z, z, conv_w, vec(conv_b), w_a, vec(b_a), w_x, vec(b_x), vec(lam),
      c_all, w_mod, b_mod)


def _lru_sample_kernel(x_ref, g_ref, prev_ref, h0_ref, cw_ref, cb_ref, wa_ref, ba_ref, wx_ref,
                       bx_ref, lam_ref, o_ref, conv_ref, h_ref, *, n_blocks):
    t = x_ref.shape[0]
    for n in range(n_blocks):
        sl = slice(n * LRU_BLOCK, (n + 1) * LRU_BLOCK)
        x = x_ref[:, sl]
        out, h = _lru_tile(x, prev_ref[0, :, sl], h0_ref[0, :, sl], g_ref[:, sl],
                           cw_ref[:, sl], cb_ref[:, sl], wa_ref[n], ba_ref[:, sl], wx_ref[n],
                           bx_ref[:, sl], lam_ref[:, sl])
        o_ref[:, sl] = out.astype(BF16)
        conv_ref[0, :, sl] = x[t - SUBLANES:]
        h_ref[0, :, sl] = h[t - SUBLANES:]


def _lru_sample(z, state_conv, state_h, lw, row0, n_batch, t, col_x, col_g):
    conv_w, conv_b, w_a, b_a, w_x, b_x, lam = lw
    n_blocks = w_a.shape[0]
    width = n_blocks * LRU_BLOCK
    k = conv_w.shape[0]
    assert t >= SUBLANES and k - 1 <= SUBLANES
    prev = jnp.pad(state_conv, ((0, 0), (SUBLANES - (k - 1), 0), (0, 0)))
    h0 = state_h.reshape(n_batch, 1, width)
    rb0 = row0 // t
    cx, cg = col_x // width, col_g // width
    vec = lambda a: a.reshape(1, width)
    vspec = pl.BlockSpec((1, width), lambda b: (0, 0))
    wspec = pl.BlockSpec((n_blocks, LRU_BLOCK, LRU_BLOCK), lambda b: (0, 0, 0))
    sspec = pl.BlockSpec((1, SUBLANES, width), lambda b: (b, 0, 0))
    return pl.pallas_call(
        functools.partial(_lru_sample_kernel, n_blocks=n_blocks),
        out_shape=(
            jax.ShapeDtypeStruct((n_batch * t, width), BF16),
            jax.ShapeDtypeStruct((n_batch, SUBLANES, width), F32),
            jax.ShapeDtypeStruct((n_batch, SUBLANES, width), F32),
        ),
        grid=(n_batch,),
        in_specs=[
            pl.BlockSpec((t, width), lambda b: (rb0 + b, cx)),
            pl.BlockSpec((t, width), lambda b: (rb0 + b, cg)),
            sspec,
            pl.BlockSpec((1, 1, width), lambda b: (b, 0, 0)),
            pl.BlockSpec((k, width), lambda b: (0, 0)),
            vspec, wspec, vspec, wspec, vspec, vspec,
        ],
        out_specs=(pl.BlockSpec((t, width), lambda b: (b, 0)), sspec, sspec),
        compiler_params=_params(("arbitrary",), 32),
        name="lru_sample",
    )(z, z, prev, h0, conv_w, vec(conv_b), w_a, vec(b_a), w_x, vec(b_x), vec(lam))


def _out_proj_kernel(ap_ref, lp_ref, as_ref, ls_ref, w_ref, xp_ref, xs_ref, gp_ref, gs_ref,
                     op_ref, os_ref):
    i = pl.program_id(1)
    ka = ap_ref.shape[1]

    def tile(a_ref, l_ref, x_ref, g_ref, o_ref):
        acc = (jnp.dot(a_ref[...], w_ref[:ka, :], preferred_element_type=F32)
               + jnp.dot(l_ref[...], w_ref[ka:, :], preferred_element_type=F32))
        o_ref[...] = x_ref[...] + _group_scale(acc, g_ref[...])

    _per_row_group(i,
                   lambda: tile(ap_ref, lp_ref, xp_ref, gp_ref, op_ref),
                   lambda: tile(as_ref, ls_ref, xs_ref, gs_ref, os_ref))


def _out_proj(attn_p, lru_p, attn_s, lru_s, w_out, x_p, x_s, table, gate_col, group):
    d, n = w_out.shape
    ka, kl = attn_p.shape[1], lru_p.shape[1]
    assert ka + kl == d and w_out.dtype == BF16
    tn = min(1024, n)
    rg = _RowGroups(x_p.shape[0], x_s.shape[0], tile=512)
    gcol = lambda j: gate_col * (n // tn) + j
    sample_table_block = (rg.rows_p // group) // (rg.rows_s // group)
    return pl.pallas_call(
        _out_proj_kernel,
        out_shape=(jax.ShapeDtypeStruct((rg.rows_p, n), F32), jax.ShapeDtypeStruct((rg.rows_s, n), F32)),
        grid=(n // tn, rg.steps),
        in_specs=[
            rg.prompt_spec(ka, lambda j: 0),
            rg.prompt_spec(kl, lambda j: 0),
            rg.sample_spec(ka, lambda j: 0, resident=True),
            rg.sample_spec(kl, lambda j: 0, resident=True),
            pl.BlockSpec((d, tn), lambda j, i: (0, j)),
            rg.prompt_spec(tn, lambda j: j),
            rg.sample_spec(tn, lambda j: j),
            rg.prompt_spec(tn, gcol, rows_per_row=group),
            rg.sample_spec(tn, gcol, rows_per_row=group, row_block=sample_table_block),
        ],
        out_specs=(rg.prompt_spec(tn, lambda j: j, output=True), rg.sample_spec(tn, lambda j: j)),
        compiler_params=_params(("arbitrary", "arbitrary"), 56),
        name="out_proj",
    )(attn_p, lru_p, attn_s, lru_s, w_out, x_p, x_s, table, table)


def _ffn_up_kernel(xp_ref, xs_ref, wg_hbm, wu_hbm, wd_ref, op_ref, os_ref, wdb_ref,
                   stage_ref, wb_ref, sem, *, n_cols, n_wd_chunks):
    j = pl.program_id(0)
    i = pl.program_id(1)
    tn = wb_ref.shape[2]
    n_full, rem = divmod(n_cols, tn)

    rc = wd_ref.shape[0]
    chunk = j * pl.num_programs(1) + i

    @pl.when(chunk < n_wd_chunks)
    def _():
        row = lax.broadcasted_iota(jnp.int32, wd_ref.shape, 0)
        wdb_ref[...] = jnp.where(chunk * rc + row < n_cols, wd_ref[...], 0.0).astype(BF16)

    def copies(jj, width):
        col = pl.multiple_of(jj * tn, tn)
        return [pltpu.make_async_copy(src.at[:, pl.ds(col, width)],
                                      stage_ref.at[w, :, pl.ds(0, width)], sem.at[w])
                for w, src in enumerate((wg_hbm, wu_hbm))]

    def start(jj):
        @pl.when(jj < n_full)
        def _():
            for c in copies(jj, tn):
                c.start()

        if rem:
            @pl.when(jj == n_full)
            def _():
                for c in copies(jj, rem):
                    c.start()

    @pl.when((j == 0) & (i == 0))
    def _():
        start(j)

    @pl.when(i == 0)
    def _():
        @pl.when(j < n_full)
        def _():
            for c in copies(j, tn):
                c.wait()
            wb_ref[...] = stage_ref[...].astype(BF16)

        if rem:
            @pl.when(j == n_full)
            def _():
                for c in copies(j, rem):
                    c.wait()
                wb_ref[:, :, :rem] = stage_ref[:, :, :rem].astype(BF16)
                wb_ref[:, :, rem:] = jnp.zeros((2, wb_ref.shape[1], tn - rem), BF16)

        @pl.when(j >= n_full + (1 if rem else 0))
        def _():
            wb_ref[...] = jnp.zeros_like(wb_ref)

    @pl.when((i == 1) & (j + 1 < pl.num_programs(0)))
    def _():
        start(j + 1)

    def tile(x_ref, o_ref):
        def hidden(width):
            x = x_ref[...]
            g = jnp.dot(x, wb_ref[0, :, :width], preferred_element_type=F32)
            u = jnp.dot(x, wb_ref[1, :, :width], preferred_element_type=F32)
            o_ref[:, :width] = (g * _sigmoid(g) * u).astype(BF16)

        if rem:
            pl.when(j != n_full)(lambda: hidden(tn))

            @pl.when(j == n_full)
            def _():
                hidden(rem)
                o_ref[:, rem:] = jnp.zeros((o_ref.shape[0], tn - rem), BF16)
        else:
            hidden(tn)

    _per_row_group(i, lambda: tile(xp_ref, op_ref), lambda: tile(xs_ref, os_ref))


def _ffn_up(u_p, u_s, w_gate, w_up, w_down, ff_pad):
    d, ff = w_gate.shape
    n_out = w_down.shape[1]
    tn = 512
    assert ff % LANES == 0 and ff_pad % tn == 0
    rg = _RowGroups(u_p.shape[0], u_s.shape[0])
    assert rg.steps >= 2
    steps = (ff_pad // tn) * rg.steps
    rc = next(r for r in range(16, ff_pad + 1, 16)
              if ff_pad % r == 0 and ff % r == 0 and ff_pad // r <= steps)
    n_wd_chunks, n_real_chunks = ff_pad // rc, ff // rc
    chunk_of = lambda j, i: j * rg.steps + i
    hbm = pl.BlockSpec(memory_space=pl.ANY)
    return pl.pallas_call(
        functools.partial(_ffn_up_kernel, n_cols=ff, n_wd_chunks=n_wd_chunks),
        out_shape=(jax.ShapeDtypeStruct((rg.rows_p, ff_pad), BF16),
                   jax.ShapeDtypeStruct((rg.rows_s, ff_pad), BF16),
                   jax.ShapeDtypeStruct((ff_pad, n_out), BF16)),
        grid=(ff_pad // tn, rg.steps),
        in_specs=[rg.prompt_spec(d, lambda j: 0), rg.sample_spec(d, lambda j: 0, resident=True),
                  hbm, hbm,
                  pl.BlockSpec((rc, n_out),
                               lambda j, i: (jnp.minimum(chunk_of(j, i), n_real_chunks - 1), 0))],
        out_specs=(rg.prompt_spec(tn, lambda j: j, output=True), rg.sample_spec(tn, lambda j: j),
                   pl.BlockSpec((rc, n_out),
                                lambda j, i: (jnp.minimum(chunk_of(j, i), n_wd_chunks - 1), 0))),
        scratch_shapes=[pltpu.VMEM((2, d, tn), F32), pltpu.VMEM((2, d, tn), BF16),
                        pltpu.SemaphoreType.DMA((2,))],
        compiler_params=_params(("arbitrary", "arbitrary"), 62),
        name="ffn_up",
    )(u_p, u_s, w_gate, w_up, w_down)


def _ffn_down_kernel(h_ref, w_ref, x_ref, g_ref, o_ref):
    k = pl.program_id(2)
    last = pl.num_programs(2) - 1
    tm = o_ref.shape[0]
    rows = min(tm, 512)
    group = tm // g_ref.shape[0]

    def chunks():
        for r0 in range(0, tm, rows):
            sl = slice(r0, r0 + rows)
            yield sl, slice(r0 // group, (r0 + rows) // group), jnp.dot(
                h_ref[sl, :], w_ref[...], preferred_element_type=F32)

    @pl.when(k == 0)
    def _():
        for sl, _, part in chunks():
            o_ref[sl, :] = part

    @pl.when((k > 0) & (k < last))
    def _():
        for sl, _, part in chunks():
            o_ref[sl, :] += part

    @pl.when(k == last)
    def _():
        for sl, gsl, part in chunks():
            o_ref[sl, :] = x_ref[sl, :] + _group_scale(o_ref[sl, :] + part, g_ref[gsl, :])


def _ffn_down(h, w_down, x1, table, gate_col, group, group0, tm, tn):
    rows, ff_pad = h.shape
    n = w_down.shape[1]
    tk = 1024
    tn = min(tn, n)
    gt = tm // group
    assert w_down.shape[0] == ff_pad and w_down.dtype == BF16 and ff_pad // tk >= 2
    assert rows % tm == 0 and ff_pad % tk == 0 and n % tn == 0 and group0 % gt == 0
    g0 = group0 // gt
    gcol0 = gate_col * (n // tn)
    return pl.pallas_call(
        _ffn_down_kernel,
        out_shape=jax.ShapeDtypeStruct((rows, n), F32),
        grid=(rows // tm, n // tn, ff_pad // tk),
        in_specs=[
            pl.BlockSpec((tm, tk), lambda i, j, k: (i, k)),
            pl.BlockSpec((tk, tn), lambda i, j, k: (k, j)),
            pl.BlockSpec((tm, tn), lambda i, j, k: (i, j)),
            pl.BlockSpec((gt, tn), lambda i, j, k: (g0 + i, gcol0 + j)),
        ],
        out_specs=pl.BlockSpec((tm, tn), lambda i, j, k: (i, j)),
        compiler_params=_params(("arbitrary", "arbitrary", "arbitrary"), 58),
        name="ffn_down",
    )(h, w_down, x1, table)


def _layer(x_p, x_s, c_prompt, c_sample, cache_k, cache_v, state_conv, state_h, lw, dims):
    (norm_mix_g, norm_ffn_g, w_mod, b_mod, w_in, q_norm_g, k_norm_g, rel_bias,
     conv_w, conv_b, w_rg_a, b_rg_a, w_rg_x, b_rg_x, lru_lambda, w_out,
     w_ffn_gate, w_ffn_up, w_ffn_down) = lw
    n_prompt, seq, n_sample, t_s = dims
    group = t_s
    n_heads = rel_bias.shape[0]
    attn_w = n_heads * HEAD_DIM
    lru_w = w_rg_a.shape[0] * LRU_BLOCK
    rows_p = n_prompt * seq
    rows_s = n_sample * t_s
    groups_p = rows_p // group

    d = x_p.shape[1]
    c_all = _mod_inputs(c_prompt, c_sample)
    b_mod2 = b_mod.reshape(1, -1)
    mod_dims = dict(n_sample=n_sample, n_prompt=n_prompt, groups_per_prompt=seq // group)
    table1 = _mod_table(c_all, w_mod, b_mod2, 2 * d, **mod_dims)
    u_p = _norm_mod(x_p, norm_mix_g, table1, 0, 1, group, 0)
    u_s = _norm_mod(x_s, norm_mix_g, table1, 0, 1, group, groups_p)
    qk_gain = jnp.concatenate([jnp.tile(q_norm_g, n_heads), jnp.tile(k_norm_g, n_heads)]).reshape(1, -1)
    z_p, z_s, w_out_b = _in_proj(u_p, u_s, w_in, qk_gain, attn_w, w_out)

    lru_params = (conv_w, conv_b, w_rg_a, b_rg_a, w_rg_x, b_rg_x, lru_lambda)
    col_x, col_g = 3 * attn_w, 3 * attn_w + lru_w
    attn_p, lru_p, conv_p, h_p, table2 = _mix_prompt(
        z_p, rel_bias, lru_params, c_all, w_mod, b_mod2, 2 * d, mod_dims,
        n_prompt, seq, n_heads, col_x, col_g)
    attn_s = _attn_sample(z_s, cache_k, cache_v, rel_bias, 0, n_sample, t_s, n_heads)
    lru_s, conv_s, h_s = _lru_sample(z_s, state_conv, state_h, lru_params, 0, n_sample, t_s,
                                     col_x, col_g)

    x1_p, x1_s = _out_proj(attn_p, lru_p, attn_s, lru_s, w_out_b, x_p, x_s, table2, 0, group)
    u2_p = _norm_mod(x1_p, norm_ffn_g, table2, 1, 2, group, 0)
    u2_s = _norm_mod(x1_s, norm_ffn_g, table2, 1, 2, group, groups_p)
    ff = w_ffn_gate.shape[1]
    ff_pad = -(-ff // 1024) * 1024
    hid_p, hid_s, w_down_b = _ffn_up(u2_p, u2_s, w_ffn_gate, w_ffn_up, w_ffn_down, ff_pad)
    y_p = _ffn_down(hid_p, w_down_b, x1_p, table2, 3, group, 0, 2048, 1024)
    y_s = _ffn_down(hid_s, w_down_b, x1_s, table2, 3, group, groups_p, rows_s, 4096)

    keep = min(N_PREV_CHUNKS * CHUNK, seq)
    kv_p = z_p.reshape(n_prompt, seq, -1)[:, seq - keep:, attn_w:3 * attn_w]
    kv_p = kv_p.reshape(n_prompt, keep, 2, n_heads, HEAD_DIM)
    kv_s = z_s[:, attn_w:3 * attn_w].reshape(n_sample, t_s, 2, n_heads, HEAD_DIM)
    k_conv = conv_w.shape[0] - 1
    state = dict(
        k_p=kv_p[:, :, 0], v_p=kv_p[:, :, 1], conv_p=conv_p[:, SUBLANES - k_conv:], h_p=h_p[:, SUBLANES - 1],
        k_s=kv_s[:, :, 0], v_s=kv_s[:, :, 1], conv_s=conv_s[:, SUBLANES - k_conv:], h_s=h_s[:, SUBLANES - 1])
    return y_p, y_s, state


def kernel(x_prompt, x_sample, cache_k, cache_v, state_conv, state_h, c_prompt, c_sample, norm_mix_g, norm_ffn_g, w_mod, b_mod, w_in, q_norm_g, k_norm_g, rel_bias, conv_w, conv_b, w_rg_a, b_rg_a, w_rg_x, b_rg_x, lru_lambda, w_out, w_ffn_gate, w_ffn_up, w_ffn_down):
    n_prompt, seq, d = x_prompt.shape
    n_sample, t_s, _ = x_sample.shape
    depth = w_in.shape[0]
    dims = (n_prompt, seq, n_sample, t_s)
    yp = x_prompt.reshape(n_prompt * seq, d)
    ys = x_sample.reshape(n_sample * t_s, d)
    states = []
    for l in range(depth):
        lw = (norm_mix_g[l], norm_ffn_g[l], w_mod[l], b_mod[l], w_in[l], q_norm_g[l], k_norm_g[l],
              rel_bias[l], conv_w[l], conv_b[l], w_rg_a[l], b_rg_a[l], w_rg_x[l], b_rg_x[l],
              lru_lambda[l], w_out[l], w_ffn_gate[l], w_ffn_up[l], w_ffn_down[l])
        yp, ys, st = _layer(yp, ys, c_prompt, c_sample, cache_k[l], cache_v[l], state_conv[l],
                            state_h[l], lw, dims)
        states.append(st)
    stack = lambda name: jnp.stack([s[name] for s in states])
    return (yp.reshape(n_prompt, seq, d), ys.reshape(n_sample, t_s, d),
            stack("k_p"), stack("v_p"), stack("conv_p"), stack("h_p"),
            stack("k_s"), stack("v_s"), stack("conv_s"), stack("h_s"))
```
